```python
import math
import jax
import jax.numpy as jnp
from jax import lax
import numpy as np

D_MODEL = 1024
BATCH = 16
SEQ = 2048
DEPTH = 2
DEC_BATCH = 32
DEC_SEQ = 8
PAST_LEN = 16384
PAGE_SIZE = 128

EPS = 1e-6
ROPE_THETA = 500000.0
NEG = -1e30
BIG = 1e9
N_AB = (DEPTH + 1) // 2
N_NSA = DEPTH // 2
SCONV_DIM = D_MODEL
SCONV_W = 3
SSD_HEADS = 16
SSD_HD = 64
SSD_DIM = SSD_HEADS * SSD_HD
SSD_GROUPS = 4
SSD_N = 128
SSD_CONV_W = 4
SSD_CONV_DIM = SSD_DIM + 2 * SSD_GROUPS * SSD_N
SSD_CHUNK = 64
IN0_DIM = 3 * SCONV_DIM + SSD_DIM + SSD_CONV_DIM + SSD_HEADS
NSA_HEADS = 16
NSA_KV = 4
NSA_HD = 64
NSA_HPG = NSA_HEADS // NSA_KV
ROT_DIM = NSA_HD // 4
CMP_BLOCK = 32
CMP_STRIDE = CMP_BLOCK // 2
SLC_BLOCK = 64
SLC_TOP = 16
WINDOW = 512
NSA_Q_CHUNK = 32
KV_W = NSA_KV * NSA_HD
IN1_DIM = NSA_HEADS * NSA_HD + 6 * KV_W + 3 * NSA_HEADS
N_MEM = 256
MEM_HEADS = 4
MEM_HD = 128
D_FF = -(-8 * D_MODEL // (3 * 256)) * 256

kernel_name = 'hybrid_shortconv_ssd_nsa_memory_step'


def rmsnorm(x, g):
    xf = x.astype(jnp.float32)
    y = xf * lax.rsqrt(jnp.mean(xf * xf, axis=-1, keepdims=True) + EPS)
    return (y * g.astype(jnp.float32)).astype(x.dtype)


def rope_partial(x, pos):
    half = ROT_DIM // 2
    inv = ROPE_THETA ** (-jnp.arange(half, dtype=jnp.float32) / half)
    ang = pos.astype(jnp.float32)[:, None] * inv[None, :]
    cos = jnp.cos(ang)[:, None, :]
    sin = jnp.sin(ang)[:, None, :]
    xr = x[..., :ROT_DIM].astype(jnp.float32)
    x1, x2 = xr[..., :half], xr[..., half:]
    rot = jnp.concatenate([x1 * cos - x2 * sin, x2 * cos + x1 * sin], axis=-1)
    return jnp.concatenate([rot.astype(x.dtype), x[..., ROT_DIM:]], axis=-1)


def causal_dwconv(u, buf, w):
    width = w.shape[0]
    T = u.shape[1]
    full = jnp.concatenate([buf.astype(u.dtype), u], axis=1)
    y = full[:, 0:T] * w[0]
    for k in range(1, width):
        y = y + full[:, k:k + T] * w[k]
    return y, full[:, T:]


def swiglu(h, wg, wu, wd):
    return (jax.nn.silu(h @ wg) * (h @ wu)) @ wd


def ssd_scan(x, dt, A, Bm, Cm, h0):
    Bsz, T = x.shape[:2]
    G, Hg, P, N = SSD_GROUPS, SSD_HEADS // SSD_GROUPS, SSD_HD, SSD_N
    Lc = SSD_CHUNK if T % SSD_CHUNK == 0 else T
    nc = T // Lc
    f32 = jnp.float32
    xf = x.astype(f32).reshape(Bsz, nc, Lc, G, Hg, P)
    dtc = dt.reshape(Bsz, nc, Lc, G, Hg)
    Bc = Bm.astype(f32).reshape(Bsz, nc, Lc, G, N)
    Cc = Cm.astype(f32).reshape(Bsz, nc, Lc, G, N)
    cum = jnp.cumsum(dtc * A.reshape(G, Hg), axis=2)
    seg = cum[:, :, :, None] - cum[:, :, None, :]
    causal = jnp.tril(jnp.ones((Lc, Lc), dtype=bool))[:, :, None, None]
    Lmat = jnp.exp(jnp.where(causal, seg, -jnp.inf))
    xdt = xf * dtc[..., None]
    CB = jnp.einsum('bcign,bcjgn->bcijg', Cc, Bc)
    y_diag = jnp.einsum('bcijg,bcijgh,bcjghp->bcighp', CB, Lmat, xdt)
    decay_to_end = jnp.exp(cum[:, :, -1:] - cum)
    chunk_states = jnp.einsum('bcjgn,bcjgh,bcjghp->bcghpn', Bc, decay_to_end, xdt)
    chunk_decay = jnp.exp(cum[:, :, -1])

    def step(h, inp):
        st, dec = inp
        return h * dec[..., None, None] + st, h

    h_final, h_starts = lax.scan(step, h0.astype(f32).reshape(Bsz, G, Hg, P, N),
                                 (jnp.moveaxis(chunk_states, 1, 0), jnp.moveaxis(chunk_decay, 1, 0)))
    h_starts = jnp.moveaxis(h_starts, 0, 1)
    y_off = jnp.einsum('bcign,bcghpn,bcigh->bcighp', Cc, h_starts, jnp.exp(cum))
    y = (y_diag + y_off).reshape(Bsz, T, SSD_HEADS, P)
    return y, h_final.reshape(Bsz, SSD_HEADS, P, N)


def mixer_ab(h, sconv_buf, ssd_conv_buf, ssd_h0, w_in, sconv_w, ssd_conv_w, ssd_conv_b,
             dt_bias, a_log, d_skip, ssd_norm, w_out):
    Bsz, T, _ = h.shape
    f32 = jnp.float32
    proj = h @ w_in
    s_h, s_b, s_c, z, xbc, dt_raw = jnp.split(
        proj, [SCONV_DIM, 2 * SCONV_DIM, 3 * SCONV_DIM, 3 * SCONV_DIM + SSD_DIM,
               3 * SCONV_DIM + SSD_DIM + SSD_CONV_DIM], axis=-1)
    conv_u, new_sconv = causal_dwconv(s_c * s_h, sconv_buf, sconv_w)
    y_a = s_b * conv_u
    xbc_c, new_ssd_conv = causal_dwconv(xbc, ssd_conv_buf, ssd_conv_w)
    xbc_c = jax.nn.silu(xbc_c + ssd_conv_b)
    xs, Bm, Cm = jnp.split(xbc_c, [SSD_DIM, SSD_DIM + SSD_GROUPS * SSD_N], axis=-1)
    xs = xs.reshape(Bsz, T, SSD_HEADS, SSD_HD)
    dt = jax.nn.softplus(dt_raw.astype(f32) + dt_bias.astype(f32))
    A = -jnp.exp(a_log.astype(f32))
    y_ssd, h_new = ssd_scan(xs, dt, A, Bm.reshape(Bsz, T, SSD_GROUPS, SSD_N),
                            Cm.reshape(Bsz, T, SSD_GROUPS, SSD_N), ssd_h0)
    y_ssd = y_ssd + d_skip.astype(f32)[:, None] * xs.astype(f32)
    yg = (y_ssd.reshape(Bsz, T, SSD_DIM) * jax.nn.silu(z.astype(f32))).reshape(Bsz, T, SSD_GROUPS, -1)
    yg = yg * lax.rsqrt(jnp.mean(yg * yg, axis=-1, keepdims=True) + EPS)
    y_b = (yg.reshape(Bsz, T, SSD_DIM) * ssd_norm.astype(f32)).astype(h.dtype)
    y = jnp.concatenate([y_a, y_b], axis=-1) @ w_out
    return y, new_sconv, new_ssd_conv, h_new.astype(ssd_h0.dtype)


def compress_blocks(rows, pe, w1, w2):
    Bsz, L = rows.shape[:2]
    nh = L // CMP_STRIDE
    halves = rows[:, :nh * CMP_STRIDE].reshape(Bsz, nh, CMP_STRIDE, NSA_KV, NSA_HD)
    a = jnp.einsum('bnjgd,jde->bnge', halves + pe[:CMP_STRIDE, None, :], w1[:CMP_STRIDE])
    b = jnp.einsum('bnjgd,jde->bnge', halves + pe[CMP_STRIDE:, None, :], w1[CMP_STRIDE:])
    return jax.nn.silu(a[:, :-1] + b[:, 1:]) @ w2


def block_overlap(n_cmp, n_slc):
    cs = jnp.arange(n_cmp)[:, None] * CMP_STRIDE
    ss = jnp.arange(n_slc)[None, :] * SLC_BLOCK
    ov = jnp.minimum(cs + CMP_BLOCK, ss + SLC_BLOCK) - jnp.maximum(cs, ss)
    return jnp.clip(ov, 0, None).astype(jnp.float32) / CMP_BLOCK


def nsa_attend(q, pos_q, gates, kc, vc, pos_c, m_ov, ks_blk, vs_blk, win_rows, pos_w):
    Bsz = q.shape[0]
    f32 = jnp.float32
    qf = q.astype(f32) * NSA_HD ** -0.5
    ok_c = (pos_c[None, :] <= pos_q[:, None])[None, :, None, None, :]
    s_c = jnp.where(ok_c, jnp.einsum('btghd,bngd->btghn', qf, kc.astype(f32)), NEG)
    p_c = jax.nn.softmax(s_c, axis=-1) * ok_c
    o_c = jnp.einsum('btghn,bngd->btghd', p_c, vc.astype(f32))
    n_slc = m_ov.shape[1]
    imp = jnp.einsum('btghn,ns->btgs', p_c, m_ov)
    blk = jnp.arange(n_slc)[None, :]
    cur = (pos_q // SLC_BLOCK)[:, None]
    imp = jnp.where(((blk == cur) | (blk == 0))[None, :, None, :], BIG, imp)
    imp = jnp.where((blk > cur)[None, :, None, :], NEG, imp)
    top_v, top_i = lax.top_k(imp, min(SLC_TOP, n_slc))
    bi = jnp.arange(Bsz)[:, None, None, None]
    gi = jnp.arange(NSA_KV)[None, None, :, None]
    k_sel = ks_blk[bi, gi, top_i].astype(f32)
    v_sel = vs_blk[bi, gi, top_i].astype(f32)
    key_pos = top_i[..., None] * SLC_BLOCK + jnp.arange(SLC_BLOCK)
    ok_s = ((top_v > 0.5 * NEG)[..., None] & (key_pos <= pos_q[None, :, None, None, None]))[:, :, :, None]
    s_s = jnp.where(ok_s, jnp.einsum('btghd,btgnsd->btghns', qf, k_sel), NEG)
    sh = s_s.shape
    p_s = jax.nn.softmax(s_s.reshape(sh[:4] + (-1,)), axis=-1).reshape(sh)
    o_s = jnp.einsum('btghns,btgnsd->btghd', p_s, v_sel)
    dpos = pos_q[:, None] - pos_w[None, :]
    ok_w = ((dpos >= 0) & (dpos < WINDOW) & (pos_w[None, :] >= 0))[None, :, None, None, :]
    s_w = jnp.where(ok_w, jnp.einsum('btghd,bsgd->btghs', qf, win_rows[:, :, 0].astype(f32)), NEG)
    p_w = jax.nn.softmax(s_w, axis=-1)
    o_w = jnp.einsum('btghs,bsgd->btghd', p_w, win_rows[:, :, 1].astype(f32))
    return gates[..., 0:1] * o_c + gates[..., 1:2] * o_s + gates[..., 2:3] * o_w


def mixer_nsa(h, pos, past_rows, win_buf, w_in, q_norm, k_norm, cmp_pe, cmp_w1, cmp_w2, w_out):
    Bsz, T, _ = h.shape
    G, Hg, D = NSA_KV, NSA_HPG, NSA_HD
    proj = h @ w_in
    q, kv, gate_raw = jnp.split(proj, [NSA_HEADS * D, NSA_HEADS * D + 6 * KV_W], axis=-1)
    q = rope_partial(rmsnorm(q.reshape(Bsz, T, NSA_HEADS, D), q_norm), pos)
    kv = kv.reshape(Bsz, T, 6, G, D)
    k_slc = rope_partial(rmsnorm(kv[:, :, 2], k_norm[1]), pos)
    k_win = rope_partial(rmsnorm(kv[:, :, 4], k_norm[2]), pos)
    new_rows = jnp.stack([kv[:, :, 0], kv[:, :, 1], k_slc, kv[:, :, 3]], axis=2)
    new_win = jnp.stack([k_win, kv[:, :, 5]], axis=2)
    if past_rows is None:
        rows = new_rows
        win_rows = jnp.concatenate([jnp.zeros((Bsz, WINDOW, 2, G, D), new_win.dtype), new_win], axis=1)
        new_win_buf = new_win[:, T - min(WINDOW, T):]
    else:
        rows = jnp.concatenate([past_rows.astype(new_rows.dtype), new_rows], axis=1)
        wb = win_buf.shape[1]
        win_rows = jnp.concatenate([win_buf.astype(new_win.dtype), new_win], axis=1)
        win_pos = past_rows.shape[1] - wb + jnp.arange(wb + T)
        new_win_buf = win_rows[:, T:]
    L = rows.shape[1]
    kc = compress_blocks(rows[:, :, 0], cmp_pe[0], cmp_w1[0], cmp_w2[0])
    vc = compress_blocks(rows[:, :, 1], cmp_pe[1], cmp_w1[1], cmp_w2[1])
    n_cmp = kc.shape[1]
    pos_c = jnp.arange(n_cmp) * CMP_STRIDE + CMP_BLOCK - 1
    kc = rope_partial(rmsnorm(kc, k_norm[0]), pos_c)
    n_slc = -(-L // SLC_BLOCK)
    pad = n_slc * SLC_BLOCK - L

    def to_blocks(r):
        r = jnp.pad(r, ((0, 0), (0, pad), (0, 0), (0, 0)))
        return r.reshape(Bsz, n_slc, SLC_BLOCK, G, D).transpose(0, 3, 1, 2, 4)

    ks_blk = to_blocks(rows[:, :, 2])
    vs_blk = to_blocks(rows[:, :, 3])
    m_ov = block_overlap(n_cmp, n_slc)
    q = q.reshape(Bsz, T, G, Hg, D)
    gates = jax.nn.sigmoid(gate_raw.astype(jnp.float32)).reshape(Bsz, T, G, Hg, 3)
    if past_rows is None:
        qc = NSA_Q_CHUNK

        def chunk(c):
            s = c * qc
            return nsa_attend(lax.dynamic_slice_in_dim(q, s, qc, 1), s + jnp.arange(qc),
                              lax.dynamic_slice_in_dim(gates, s, qc, 1), kc, vc, pos_c, m_ov,
                              ks_blk, vs_blk, lax.dynamic_slice_in_dim(win_rows, s, qc + WINDOW, 1),
                              s - WINDOW + jnp.arange(qc + WINDOW))

        o = lax.map(chunk, jnp.arange(T // qc))
        o = jnp.moveaxis(o, 0, 1).reshape(Bsz, T, NSA_HEADS * D)
    else:
        o = nsa_attend(q, pos, gates, kc, vc, pos_c, m_ov, ks_blk, vs_blk, win_rows, win_pos)
        o = o.reshape(Bsz, T, NSA_HEADS * D)
    y = o.astype(h.dtype) @ w_out
    return y, new_rows, new_win_buf


def memory_kv(mem, g_src, w_kv, k_norm):
    Bsz, M, _ = mem.shape
    kv = (rmsnorm(mem, g_src) @ w_kv).reshape(Bsz, M, 2, MEM_HEADS, MEM_HD)
    return jnp.stack([rmsnorm(kv[:, :, 0], k_norm), kv[:, :, 1]], axis=2)


def mem_attend(h, kv, w_q, q_norm, w_o):
    Bsz, T, _ = h.shape
    f32 = jnp.float32
    q = rmsnorm((h @ w_q).reshape(Bsz, T, MEM_HEADS, MEM_HD), q_norm)
    s = jnp.einsum('bthd,bmhd->bhtm', q.astype(f32), kv[:, :, 0].astype(f32)) * MEM_HD ** -0.5
    p = jax.nn.softmax(s, axis=-1)
    o = jnp.einsum('bhtm,bmhd->bthd', p, kv[:, :, 1].astype(f32))
    return o.reshape(Bsz, T, MEM_HEADS * MEM_HD).astype(h.dtype) @ w_o


def run_group(x, pos, mem_kvs, sconv_st, ssd_conv_st, ssd_st, nsa_past, nsa_win, p):
    new_sconv, new_ssdc, new_ssd, new_rows, new_win = [], [], [], [], []
    for layer in range(DEPTH):
        j = layer // 2
        h = rmsnorm(x, p['norm_mix'][layer])
        if layer % 2 == 0:
            y, s_a, s_b, s_c = mixer_ab(h, sconv_st[j], ssd_conv_st[j], ssd_st[j], p['ab_w_in'][j],
                                        p['ab_sconv_w'][j], p['ab_ssd_conv_w'][j], p['ab_ssd_conv_b'][j],
                                        p['ab_dt_bias'][j], p['ab_a_log'][j], p['ab_d'][j],
                                        p['ab_ssd_norm'][j], p['ab_w_out'][j])
            new_sconv.append(s_a)
            new_ssdc.append(s_b)
            new_ssd.append(s_c)
        else:
            past_j = None if nsa_past is None else nsa_past[:, :, j]
            win_j = None if nsa_win is None else nsa_win[j]
            y, rows, win = mixer_nsa(h, pos, past_j, win_j, p['nsa_w_in'][j], p['nsa_q_norm'][j],
                                     p['nsa_k_norm'][j], p['nsa_cmp_pe'][j], p['nsa_cmp_w1'][j],
                                     p['nsa_cmp_w2'][j], p['nsa_w_out'][j])
            new_rows.append(rows)
            new_win.append(win)
        x = x + y
        x = x + mem_attend(rmsnorm(x, p['norm_mem'][layer]), mem_kvs[layer], p['mem_w_q'][layer],
                           p['mem_q_norm'][layer], p['mem_w_o'][layer])
        x = x + swiglu(rmsnorm(x, p['norm_ffn'][layer]), p['ffn_w_gate'][layer], p['ffn_w_up'][layer],
                       p['ffn_w_down'][layer])
    return (x, jnp.stack(new_sconv), jnp.stack(new_ssdc), jnp.stack(new_ssd),
            jnp.stack(new_rows, axis=2), jnp.stack(new_win))


def setup_inputs(seed: int = 0) -> dict:
    key = jax.random.key(seed)
    ks = iter(jax.random.split(key, 64))
    f32 = jnp.float32

    def nrm(shape, scale):
        return jax.random.normal(next(ks), shape, f32) * scale

    def gain(shape):
        return 1.0 + nrm(shape, 0.05)

    D = D_MODEL
    n_pages = PAST_LEN // PAGE_SIZE
    n_pool = (DEC_BATCH * n_pages * 5) // 4
    w_buf = min(WINDOW, PAST_LEN)
    page_table = jax.random.permutation(next(ks), n_pool)[:DEC_BATCH * n_pages]
    page_table = page_table.reshape(DEC_BATCH, n_pages).astype(jnp.int32)
    dt0 = jnp.exp(jax.random.uniform(next(ks), (N_AB, SSD_HEADS), f32, math.log(1e-3), math.log(1e-1)))
    dt_bias = dt0 + jnp.log(-jnp.expm1(-dt0))
    a_log = jnp.log(jax.random.uniform(next(ks), (N_AB, SSD_HEADS), f32, 1.0, 16.0))
    return {
        'x_prompt': nrm((BATCH, SEQ, D), 1.0),
        'x_sample': nrm((DEC_BATCH, DEC_SEQ, D), 1.0),
        'mem_prompt': nrm((BATCH, N_MEM, D), 1.0),
        'state_sconv': nrm((N_AB, DEC_BATCH, SCONV_W - 1, SCONV_DIM), 1.0),
        'state_ssd_conv': nrm((N_AB, DEC_BATCH, SSD_CONV_W - 1, SSD_CONV_DIM), 1.0),
        'state_ssd': nrm((N_AB, DEC_BATCH, SSD_HEADS, SSD_HD, SSD_N), 0.1),
        'cache_nsa_kv': nrm((n_pool, PAGE_SIZE, N_NSA, 4, NSA_KV, NSA_HD), 1.0),
        'cache_nsa_win': nrm((N_NSA, DEC_BATCH, w_buf, 2, NSA_KV, NSA_HD), 1.0),
        'cache_mem_kv': nrm((DEPTH, DEC_BATCH, N_MEM, 2, MEM_HEADS, MEM_HD), 1.0),
        'page_table': page_table,
        'norm_mix': gain((DEPTH, D)),
        'norm_mem': gain((DEPTH, D)),
        'norm_memsrc': gain((DEPTH, D)),
        'norm_ffn': gain((DEPTH, D)),
        'ab_w_in': nrm((N_AB, D, IN0_DIM), D ** -0.5),
        'ab_sconv_w': nrm((N_AB, SCONV_W, SCONV_DIM), SCONV_W ** -0.5),
        'ab_ssd_conv_w': nrm((N_AB, SSD_CONV_W, SSD_CONV_DIM), SSD_CONV_W ** -0.5),
        'ab_ssd_conv_b': nrm((N_AB, SSD_CONV_DIM), 0.02),
        'ab_dt_bias': dt_bias,
        'ab_a_log': a_log,
        'ab_d': gain((N_AB, SSD_HEADS)),
        'ab_ssd_norm': gain((N_AB, SSD_DIM)),
        'ab_w_out': nrm((N_AB, SCONV_DIM + SSD_DIM, D), (SCONV_DIM + SSD_DIM) ** -0.5),
        'nsa_w_in': nrm((N_NSA, D, IN1_DIM), D ** -0.5),
        'nsa_q_norm': gain((N_NSA, NSA_HD)),
        'nsa_k_norm': gain((N_NSA, 3, NSA_HD)),
        'nsa_cmp_pe': nrm((N_NSA, 2, CMP_BLOCK, NSA_HD), 0.02),
        'nsa_cmp_w1': nrm((N_NSA, 2, CMP_BLOCK, NSA_HD, NSA_HD), (CMP_BLOCK * NSA_HD) ** -0.5),
        'nsa_cmp_w2': nrm((N_NSA, 2, NSA_HD, NSA_HD), NSA_HD ** -0.5),
        'nsa_w_out': nrm((N_NSA, NSA_HEADS * NSA_HD, D), (NSA_HEADS * NSA_HD) ** -0.5),
        'mem_w_q': nrm((DEPTH, D, MEM_HEADS * MEM_HD), D ** -0.5),
        'mem_q_norm': gain((DEPTH, MEM_HD)),
        'mem_w_kv': nrm((DEPTH, D, 2 * MEM_HEADS * MEM_HD), D ** -0.5),
        'mem_k_norm': gain((DEPTH, MEM_HD)),
        'mem_w_o': nrm((DEPTH, MEM_HEADS * MEM_HD, D), (MEM_HEADS * MEM_HD) ** -0.5),
        'ffn_w_gate': nrm((DEPTH, D, D_FF), D ** -0.5),
        'ffn_w_up': nrm((DEPTH, D, D_FF), D ** -0.5),
        'ffn_w_down': nrm((DEPTH, D_FF, D), D_FF ** -0.5),
    }


def reference(x_prompt, x_sample, mem_prompt, state_sconv, state_ssd_conv, state_ssd, cache_nsa_kv,
              cache_nsa_win, cache_mem_kv, page_table, norm_mix, norm_mem, norm_memsrc, norm_ffn,
              ab_w_in, ab_sconv_w, ab_ssd_conv_w, ab_ssd_conv_b, ab_dt_bias, ab_a_log, ab_d, ab_ssd_norm,
              ab_w_out, nsa_w_in, nsa_q_norm, nsa_k_norm, nsa_cmp_pe, nsa_cmp_w1, nsa_cmp_w2, nsa_w_out,
              mem_w_q, mem_q_norm, mem_w_kv, mem_k_norm, mem_w_o, ffn_w_gate, ffn_w_up, ffn_w_down):
    p = dict(norm_mix=norm_mix, norm_mem=norm_mem, norm_ffn=norm_ffn, ab_w_in=ab_w_in,
             ab_sconv_w=ab_sconv_w, ab_ssd_conv_w=ab_ssd_conv_w, ab_ssd_conv_b=ab_ssd_conv_b,
             ab_dt_bias=ab_dt_bias, ab_a_log=ab_a_log, ab_d=ab_d, ab_ssd_norm=ab_ssd_norm,
             ab_w_out=ab_w_out, nsa_w_in=nsa_w_in, nsa_q_norm=nsa_q_norm, nsa_k_norm=nsa_k_norm,
             nsa_cmp_pe=nsa_cmp_pe, nsa_cmp_w1=nsa_cmp_w1, nsa_cmp_w2=nsa_cmp_w2, nsa_w_out=nsa_w_out,
             mem_w_q=mem_w_q, mem_q_norm=mem_q_norm, mem_w_o=mem_w_o,
             ffn_w_gate=ffn_w_gate, ffn_w_up=ffn_w_up, ffn_w_down=ffn_w_down)
    Bp, T = x_prompt.shape[:2]
    Bs, Ts = x_sample.shape[:2]
    dt = x_prompt.dtype
    p_mem_kv = jnp.stack([memory_kv(mem_prompt, norm_memsrc[l], mem_w_kv[l], mem_k_norm[l])
                          for l in range(DEPTH)], axis=0)
    y_prompt, p_sconv, p_ssdc, p_ssd, p_rows, p_win = run_group(
        x_prompt, jnp.arange(T), p_mem_kv,
        jnp.zeros((N_AB, Bp, SCONV_W - 1, SCONV_DIM), dt),
        jnp.zeros((N_AB, Bp, SSD_CONV_W - 1, SSD_CONV_DIM), dt),
        jnp.zeros((N_AB, Bp, SSD_HEADS, SSD_HD, SSD_N), dt),
        None, None, p)
    n_pages = page_table.shape[1]
    past = cache_nsa_kv[page_table].reshape(Bs, n_pages * cache_nsa_kv.shape[1], N_NSA, 4, NSA_KV, NSA_HD)
    y_sample, s_sconv, s_ssdc, s_ssd, s_rows, s_win = run_group(
        x_sample, past.shape[1] + jnp.arange(Ts), cache_mem_kv, state_sconv, state_ssd_conv, state_ssd,
        past, cache_nsa_win, p)
    return (y_prompt, y_sample, p_sconv, p_ssdc, p_ssd, p_rows, p_win, p_mem_kv,
            s_sconv, s_ssdc, s_ssd, s_rows, s_win)
```

```python
import functools
import math

import jax
import jax.numpy as jnp
from jax import lax
from jax.experimental import pallas as pl
from jax.experimental.pallas import tpu as pltpu

F32 = jnp.float32
BF16 = jnp.bfloat16
HI = lax.Precision.HIGHEST

EPS = 1e-6
ROPE_THETA = 500000.0
NEG = -1e30
BIG = 1e9

LANES = 128
SUBLANES = 8
VMEM_LIMIT = 56 * 1024 * 1024

D_MODEL = 1024
SCONV_W = 3
SSD_HEADS = 16
SSD_HD = 64
SSD_DIM = SSD_HEADS * SSD_HD
SSD_GROUPS = 4
SSD_HPG = SSD_HEADS // SSD_GROUPS
SSD_N = 128
SSD_CONV_W = 4
SSD_CONV_DIM = SSD_DIM + 2 * SSD_GROUPS * SSD_N
IN0_DIM = 3 * D_MODEL + SSD_DIM + SSD_CONV_DIM + SSD_HEADS
IN0_PAD = 3 * D_MODEL + SSD_DIM + SSD_CONV_DIM + LANES
NSA_HEADS = 16
NSA_KV = 4
NSA_HD = 64
NSA_HPG = NSA_HEADS // NSA_KV
ROT_HALF = NSA_HD // 8
CMP_BLOCK = 32
CMP_STRIDE = 16
SLC_BLOCK = 64
SLC_TOP = 16
WINDOW = 512
KV_W = NSA_KV * NSA_HD
MEM_HEADS = 4
MEM_HD = 128
CARRY = SUBLANES


def _cparams(sem):
    return pltpu.CompilerParams(dimension_semantics=sem, vmem_limit_bytes=VMEM_LIMIT)


def _const_spec(shape):
    n = len(shape)
    return pl.BlockSpec(shape, lambda *_: (0,) * n)


def _rms(x, g):
    ms = jnp.mean(x * x, axis=-1, keepdims=True)
    return x * lax.rsqrt(ms + EPS) * g


def _dot(a, b):
    return jnp.dot(a, b, preferred_element_type=F32)


def _dot_nt(a, b, precision=None):
    return lax.dot_general(a, b, (((1,), (1,)), ((), ())), precision=precision,
                           preferred_element_type=F32)


def _dot_tn(a, b):
    return lax.dot_general(a, b, (((0,), (0,)), ((), ())), preferred_element_type=F32)


def _silu(x):
    return x * jax.nn.sigmoid(x)


def _softplus(x):
    return jnp.maximum(x, 0.0) + jnp.log1p(jnp.exp(-jnp.abs(x)))


def _seg_mean_matrix():
    r = lax.broadcasted_iota(jnp.int32, (LANES, LANES), 0) // NSA_HD
    c = lax.broadcasted_iota(jnp.int32, (LANES, LANES), 1) // NSA_HD
    return jnp.where(r == c, 1.0 / NSA_HD, 0.0).astype(F32)


def _headnorm_rope_tok(x, gain, bd, c, s1, s2):
    ms = jnp.dot(x * x, bd, precision=HI, preferred_element_type=F32)
    xn = x * lax.rsqrt(ms + EPS) * gain
    return xn * c + pltpu.roll(xn, LANES - ROT_HALF, 1) * s1 + pltpu.roll(xn, ROT_HALF, 1) * s2


def _headnorm_rope_ch(x, gain_col, cos_t, sin_t):
    ms = jnp.mean(x * x, axis=0, keepdims=True)
    xn = x * lax.rsqrt(ms + EPS) * gain_col
    x1 = xn[0:ROT_HALF]
    x2 = xn[ROT_HALF:2 * ROT_HALF]
    return x1 * cos_t - x2 * sin_t, x2 * cos_t + x1 * sin_t, xn[2 * ROT_HALF:]


def _rope_tables(pos):
    inv = ROPE_THETA ** (-jnp.arange(ROT_HALF, dtype=F32) / ROT_HALF)
    ang = pos.astype(F32)[:, None] * inv[None, :]
    cos, sin = jnp.cos(ang), jnp.sin(ang)
    n = pos.shape[0]
    ones = jnp.ones((n, NSA_HD - 2 * ROT_HALF), F32)
    zeros8 = jnp.zeros((n, ROT_HALF), F32)
    zeros = jnp.zeros((n, NSA_HD - 2 * ROT_HALF), F32)
    c = jnp.concatenate([cos, cos, ones], axis=1)
    s1 = jnp.concatenate([-sin, zeros8, zeros], axis=1)
    s2 = jnp.concatenate([zeros8, sin, zeros], axis=1)
    tile2 = lambda t: jnp.concatenate([t, t], axis=1)
    return tile2(c), tile2(s1), tile2(s2), cos.T, sin.T


def _memkv_kernel(x_ref, g_ref, w_ref, kn_ref, o_ref):
    hn = _rms(x_ref[...], g_ref[...]).astype(BF16)
    kv = _dot(hn, w_ref[...])
    half = MEM_HEADS * MEM_HD
    for h in range(MEM_HEADS):
        sl = slice(h * MEM_HD, (h + 1) * MEM_HD)
        o_ref[:, sl] = _rms(kv[:, sl], kn_ref[...])
    o_ref[:, half:] = kv[:, half:]


def _memkv(mem2d, g, w_bf, kn):
    m = mem2d.shape[0]
    tm = min(512, m)
    n = w_bf.shape[1]
    return pl.pallas_call(
        _memkv_kernel,
        grid=(m // tm,),
        in_specs=[pl.BlockSpec((tm, D_MODEL), lambda i: (i, 0)), _const_spec((1, D_MODEL)),
                  _const_spec((D_MODEL, n)), _const_spec((1, MEM_HD))],
        out_specs=pl.BlockSpec((tm, n), lambda i: (i, 0)),
        out_shape=jax.ShapeDtypeStruct((m, n), F32),
        compiler_params=_cparams(("parallel",)),
        name="memkv",
    )(mem2d, g, w_bf, kn)


def _mixer0_kernel(x_ref, g_ref, win_ref, scw_ref, cw_ref, cb_ref, dtb_ref, alog_ref, dsk_ref,
                   ssdn_ref, wout_ref, sa0_ref, sb0_ref, h0_ref,
                   y_ref, sa_ref, sb_ref, hout_ref,
                   proj, ca, cbuf, hst, ycat, *, tq, n_col_chunks):
    i = pl.program_id(1)
    last = pl.num_programs(1) - 1

    @pl.when(i == 0)
    def _():
        ca[0:CARRY, :] = sa0_ref[0]
        cbuf[0:CARRY, :] = sb0_ref[0]
        hst[...] = h0_ref[0]

    x = x_ref[0]
    hn = _rms(x, g_ref[...]).astype(BF16)
    cw_cols = IN0_PAD // n_col_chunks
    for c in range(n_col_chunks):
        sl = slice(c * cw_cols, (c + 1) * cw_cols)
        proj[:, sl] = _dot(hn, win_ref[:, sl])

    d = D_MODEL
    ca[CARRY:CARRY + tq, :] = proj[:, 2 * d:3 * d] * proj[:, 0:d]
    conv_u = ca[pl.ds(CARRY - 2, tq), :] * scw_ref[0:1, :]
    for k in range(1, SCONV_W):
        conv_u = conv_u + ca[pl.ds(CARRY - 2 + k, tq), :] * scw_ref[k:k + 1, :]
    ycat[:, 0:d] = (proj[:, d:2 * d] * conv_u).astype(BF16)

    x0 = 3 * d + SSD_DIM
    cbuf[CARRY:CARRY + tq, :] = proj[:, x0:x0 + SSD_CONV_DIM]
    xbc = cbuf[pl.ds(CARRY - 3, tq), :] * cw_ref[0:1, :]
    for k in range(1, SSD_CONV_W):
        xbc = xbc + cbuf[pl.ds(CARRY - 3 + k, tq), :] * cw_ref[k:k + 1, :]
    xbc = _silu(xbc + cb_ref[...])

    dt = _softplus(proj[:, x0 + SSD_CONV_DIM:IN0_PAD] + dtb_ref[...])
    a_neg = -jnp.exp(alog_ref[...])
    row = lax.broadcasted_iota(jnp.int32, (tq, tq), 0)
    col = lax.broadcasted_iota(jnp.int32, (tq, tq), 1)
    causal = row >= col
    tril = jnp.where(causal, 1.0, 0.0).astype(F32)
    cum = jnp.dot(tril, dt * a_neg, precision=HI, preferred_element_type=F32)
    eye = jnp.where(lax.broadcasted_iota(jnp.int32, (LANES, LANES), 0)
                    == lax.broadcasted_iota(jnp.int32, (LANES, LANES), 1), 1.0, 0.0).astype(F32)
    cum_t = _dot_nt(eye, cum, precision=HI)
    cum_last = cum[tq - 1:tq, :]
    e_cum = jnp.exp(cum)
    e_last = jnp.exp(cum_last)
    e_end = jnp.exp(cum_last - cum)
    z = proj[:, 3 * d:3 * d + SSD_DIM]
    dsk = dsk_ref[...]

    for g in range(SSD_GROUPS):
        b_g = xbc[:, SSD_DIM + g * SSD_N:SSD_DIM + (g + 1) * SSD_N].astype(BF16)
        c_g = xbc[:, SSD_DIM + (SSD_GROUPS + g) * SSD_N:
                  SSD_DIM + (SSD_GROUPS + g + 1) * SSD_N].astype(BF16)
        cb_g = _dot_nt(c_g, b_g)
        y_parts = []
        for hh in range(SSD_HPG):
            h = g * SSD_HPG + hh
            x_h = xbc[:, h * SSD_HD:(h + 1) * SSD_HD]
            xdt = x_h * dt[:, h:h + 1]
            lmat = jnp.where(causal, jnp.exp(cum[:, h:h + 1] - cum_t[h:h + 1, :]), 0.0)
            y_h = _dot((cb_g * lmat).astype(BF16), xdt.astype(BF16))
            st = hst[h]
            y_h = y_h + e_cum[:, h:h + 1] * _dot_nt(c_g, st.astype(BF16))
            hst[h] = e_last[:, h:h + 1] * st + _dot_tn((xdt * e_end[:, h:h + 1]).astype(BF16), b_g)
            y_parts.append(y_h + dsk[:, h:h + 1] * x_h)
        yg = jnp.concatenate(y_parts, axis=1)
        gs = slice(g * SSD_HPG * SSD_HD, (g + 1) * SSD_HPG * SSD_HD)
        yg = yg * _silu(z[:, gs])
        yg = yg * lax.rsqrt(jnp.mean(yg * yg, axis=-1, keepdims=True) + EPS)
        ycat[:, d + gs.start:d + gs.stop] = (yg * ssdn_ref[:, gs]).astype(BF16)

    y_ref[0] = x + _dot(ycat[...], wout_ref[...])

    ca[0:CARRY, :] = ca[tq:tq + CARRY, :]
    cbuf[0:CARRY, :] = cbuf[tq:tq + CARRY, :]

    @pl.when(i == last)
    def _():
        sa_ref[0] = ca[0:CARRY, :]
        sb_ref[0] = cbuf[0:CARRY, :]
        hout_ref[0] = hst[...]


def _mixer0(x, tq, g, w_in_bf, scw, cw, cb, dtb, alog, dsk, ssdn, w_out_bf, sa0, sb0, h0):
    bsz, t, d = x.shape
    n_col_chunks = 7
    kern = functools.partial(_mixer0_kernel, tq=tq, n_col_chunks=n_col_chunks)
    per_b = lambda b, i: (b, 0, 0)
    return pl.pallas_call(
        kern,
        grid=(bsz, t // tq),
        in_specs=[pl.BlockSpec((1, tq, d), lambda b, i: (b, i, 0)),
                  _const_spec((1, d)), _const_spec((d, IN0_PAD)),
                  _const_spec((SUBLANES, d)), _const_spec((SUBLANES, SSD_CONV_DIM)),
                  _const_spec((1, SSD_CONV_DIM)), _const_spec((1, LANES)), _const_spec((1, LANES)),
                  _const_spec((1, LANES)), _const_spec((1, SSD_DIM)),
                  _const_spec((d + SSD_DIM, d)),
                  pl.BlockSpec((1, CARRY, d), per_b),
                  pl.BlockSpec((1, CARRY, SSD_CONV_DIM), per_b),
                  pl.BlockSpec((1, SSD_HEADS, SSD_HD, SSD_N), lambda b, i: (b, 0, 0, 0))],
        out_specs=[pl.BlockSpec((1, tq, d), lambda b, i: (b, i, 0)),
                   pl.BlockSpec((1, CARRY, d), per_b),
                   pl.BlockSpec((1, CARRY, SSD_CONV_DIM), per_b),
                   pl.BlockSpec((1, SSD_HEADS, SSD_HD, SSD_N), lambda b, i: (b, 0, 0, 0))],
        out_shape=[jax.ShapeDtypeStruct((bsz, t, d), F32),
                   jax.ShapeDtypeStruct((bsz, CARRY, d), F32),
                   jax.ShapeDtypeStruct((bsz, CARRY, SSD_CONV_DIM), F32),
                   jax.ShapeDtypeStruct((bsz, SSD_HEADS, SSD_HD, SSD_N), F32)],
        scratch_shapes=[pltpu.VMEM((tq, IN0_PAD), F32),
                        pltpu.VMEM((CARRY + tq, d), F32),
                        pltpu.VMEM((CARRY + tq, SSD_CONV_DIM), F32),
                        pltpu.VMEM((SSD_HEADS, SSD_HD, SSD_N), F32),
                        pltpu.VMEM((tq, d + SSD_DIM), BF16)],
        compiler_params=_cparams(("parallel", "arbitrary")),
        name="mixer0",
    )(x, g, w_in_bf, scw, cw, cb, dtb, alog, dsk, ssdn, w_out_bf, sa0, sb0, h0)


def _memattn_kernel(x_ref, g_ref, wq_ref, qn_ref, kv_ref, wo_ref, o_ref):
    x = x_ref[0]
    hn = _rms(x, g_ref[...]).astype(BF16)
    q = _dot(hn, wq_ref[...])
    kv = kv_ref[0]
    half = MEM_HEADS * MEM_HD
    outs = []
    for h in range(MEM_HEADS):
        sl = slice(h * MEM_HD, (h + 1) * MEM_HD)
        qh = _rms(q[:, sl], qn_ref[...]).astype(BF16)
        s = _dot_nt(qh, kv[:, sl].astype(BF16)) * (MEM_HD ** -0.5)
        m = jnp.max(s, axis=-1, keepdims=True)
        e = jnp.exp(s - m)
        p = e / jnp.sum(e, axis=-1, keepdims=True)
        outs.append(_dot(p.astype(BF16), kv[:, half + h * MEM_HD:half + (h + 1) * MEM_HD].astype(BF16)))
    o = jnp.concatenate(outs, axis=1).astype(BF16)
    o_ref[0] = x + _dot(o, wo_ref[...])


def _memattn(x, tq, g, wq_bf, qn, kv, wo_bf):
    bsz, t, d = x.shape
    n_mem = kv.shape[1]
    hw = MEM_HEADS * MEM_HD
    return pl.pallas_call(
        _memattn_kernel,
        grid=(bsz, t // tq),
        in_specs=[pl.BlockSpec((1, tq, d), lambda b, i: (b, i, 0)),
                  _const_spec((1, d)), _const_spec((d, hw)), _const_spec((1, MEM_HD)),
                  pl.BlockSpec((1, n_mem, 2 * hw), lambda b, i: (b, 0, 0)),
                  _const_spec((hw, d))],
        out_specs=pl.BlockSpec((1, tq, d), lambda b, i: (b, i, 0)),
        out_shape=jax.ShapeDtypeStruct((bsz, t, d), F32),
        compiler_params=_cparams(("parallel", "parallel")),
        name="memattn",
    )(x, g, wq_bf, qn, kv, wo_bf)


def _ffn_kernel(x_ref, g_ref, wg_ref, wu_ref, wd_ref, o_ref, hn_scr, acc):
    f = pl.program_id(1)

    @pl.when(f == 0)
    def _():
        hn_scr[...] = _rms(x_ref[...], g_ref[...]).astype(BF16)
        acc[...] = jnp.zeros_like(acc)

    hn = hn_scr[...]
    a = _silu(_dot(hn, wg_ref[...])) * _dot(hn, wu_ref[...])
    acc[...] += _dot(a.astype(BF16), wd_ref[...])

    @pl.when(f == pl.num_programs(1) - 1)
    def _():
        o_ref[...] = x_ref[...] + acc[...]


def _ffn(x2d, tm, g, wg_bf, wu_bf, wd_bf):
    m, d = x2d.shape
    dff = wg_bf.shape[1]
    tf = 256
    return pl.pallas_call(
        _ffn_kernel,
        grid=(m // tm, dff // tf),
        in_specs=[pl.BlockSpec((tm, d), lambda i, f: (i, 0)), _const_spec((1, d)),
                  pl.BlockSpec((d, tf), lambda i, f: (0, f)),
                  pl.BlockSpec((d, tf), lambda i, f: (0, f)),
                  pl.BlockSpec((tf, d), lambda i, f: (f, 0))],
        out_specs=pl.BlockSpec((tm, d), lambda i, f: (i, 0)),
        out_shape=jax.ShapeDtypeStruct((m, d), F32),
        scratch_shapes=[pltpu.VMEM((tm, d), BF16), pltpu.VMEM((tm, d), F32)],
        compiler_params=_cparams(("parallel", "arbitrary")),
        name="ffn",
    )(x2d, g, wg_bf, wu_bf, wd_bf)


def _proj_res_kernel(a_ref, w_ref, x_ref, o_ref):
    o_ref[...] = x_ref[...] + _dot(a_ref[...].astype(BF16), w_ref[...])


def _proj_res(a2d, w_bf, x2d):
    m, k = a2d.shape
    n = w_bf.shape[1]
    tm = min(256, m)
    return pl.pallas_call(
        _proj_res_kernel,
        grid=(m // tm,),
        in_specs=[pl.BlockSpec((tm, k), lambda i: (i, 0)), _const_spec((k, n)),
                  pl.BlockSpec((tm, n), lambda i: (i, 0))],
        out_specs=pl.BlockSpec((tm, n), lambda i: (i, 0)),
        out_shape=jax.ShapeDtypeStruct((m, n), F32),
        compiler_params=_cparams(("parallel",)),
        name="proj_res",
    )(a2d, w_bf, x2d)


def _nsa_proj_kernel(x_ref, g_ref, wq_ref, wg_ref, wkv_ref, bd_ref, qg_ref, c_ref, s1_ref, s2_ref,
                     kg_ref, cos_ref, sin_ref,
                     q_ref, gate_ref, rows_ref, win_ref, kvbf_ref):
    hn = _rms(x_ref[0], g_ref[...]).astype(BF16)
    q = _dot(hn, wq_ref[...])
    bd = bd_ref[...]
    scale = NSA_HD ** -0.5
    for c in range(NSA_HEADS * NSA_HD // LANES):
        sl = slice(c * LANES, (c + 1) * LANES)
        q_ref[0, :, sl] = scale * _headnorm_rope_tok(q[:, sl], qg_ref[...], bd, c_ref[...],
                                                     s1_ref[...], s2_ref[...])
    gate_ref[0] = jax.nn.sigmoid(_dot(hn, wg_ref[...]))
    kv_t = _dot_nt(wkv_ref[...], hn)
    cos_t, sin_t = cos_ref[...], sin_ref[...]

    def put_normed(src_row, gain_idx, dst_ref, dst_row, bf_row):
        for g in range(NSA_KV):
            r1, r2, rest = _headnorm_rope_ch(kv_t[src_row + g * NSA_HD:src_row + (g + 1) * NSA_HD],
                                             kg_ref[gain_idx], cos_t, sin_t)
            full = jnp.concatenate([r1, r2, rest], axis=0)
            dst_ref[0, dst_row + g * NSA_HD:dst_row + (g + 1) * NSA_HD, :] = full
            kvbf_ref[0, bf_row + g * NSA_HD:bf_row + (g + 1) * NSA_HD, :] = full.astype(BF16)

    w = KV_W
    rows_ref[0, 0:2 * w, :] = kv_t[0:2 * w]
    put_normed(2 * w, 0, rows_ref, 2 * w, 0)
    rows_ref[0, 3 * w:4 * w, :] = kv_t[3 * w:4 * w]
    kvbf_ref[0, w:2 * w, :] = kv_t[3 * w:4 * w].astype(BF16)
    put_normed(4 * w, 1, win_ref, 0, 2 * w)
    win_ref[0, w:2 * w, :] = kv_t[5 * w:6 * w]
    kvbf_ref[0, 3 * w:4 * w, :] = kv_t[5 * w:6 * w].astype(BF16)


def _nsa_proj(x, tm, g, wq_bf, wg_bf, wkv_t_bf, qgain2, kgain, tabs):
    bsz, t, d = x.shape
    c, s1, s2, cos_t, sin_t = tabs
    hq = NSA_HEADS * NSA_HD
    w = KV_W
    tok = lambda b, i: (b, i, 0)
    chan = lambda b, i: (b, 0, i)
    return pl.pallas_call(
        _nsa_proj_kernel,
        grid=(bsz, t // tm),
        in_specs=[pl.BlockSpec((1, tm, d), tok), _const_spec((1, d)),
                  _const_spec((d, hq)), _const_spec((d, LANES)), _const_spec((6 * w, d)),
                  _const_spec((LANES, LANES)), _const_spec((1, LANES)),
                  pl.BlockSpec((tm, LANES), lambda b, i: (i, 0)),
                  pl.BlockSpec((tm, LANES), lambda b, i: (i, 0)),
                  pl.BlockSpec((tm, LANES), lambda b, i: (i, 0)),
                  _const_spec((2, NSA_HD, 1)),
                  pl.BlockSpec((ROT_HALF, tm), lambda b, i: (0, i)),
                  pl.BlockSpec((ROT_HALF, tm), lambda b, i: (0, i))],
        out_specs=[pl.BlockSpec((1, tm, hq), tok), pl.BlockSpec((1, tm, LANES), tok),
                   pl.BlockSpec((1, 4 * w, tm), chan), pl.BlockSpec((1, 2 * w, tm), chan),
                   pl.BlockSpec((1, 4 * w, tm), chan)],
        out_shape=[jax.ShapeDtypeStruct((bsz, t, hq), F32),
                   jax.ShapeDtypeStruct((bsz, t, LANES), F32),
                   jax.ShapeDtypeStruct((bsz, 4 * w, t), F32),
                   jax.ShapeDtypeStruct((bsz, 2 * w, t), F32),
                   jax.ShapeDtypeStruct((bsz, 4 * w, t), BF16)],
        compiler_params=_cparams(("parallel", "parallel")),
        name="nsa_proj",
    )(x, g, wq_bf, wg_bf, wkv_t_bf, _seg_mean_matrix(), qgain2, c, s1, s2, kgain, cos_t, sin_t)


def _half_sums(xt, pe_ref, w1_ref, nh):
    out = []
    tiles = KV_W // LANES
    for kind in range(2):
        a = jnp.zeros((nh, KV_W), F32)
        b = jnp.zeros((nh, KV_W), F32)
        for j in range(CMP_STRIDE):
            r = jnp.concatenate([xt[kind * tiles + c, pl.ds(j, nh, stride=CMP_STRIDE), :]
                                 for c in range(tiles)], axis=1)
            a = a + _dot((r + pe_ref[kind, j:j + 1, :]).astype(BF16), w1_ref[kind, j])
            b = b + _dot((r + pe_ref[kind, CMP_STRIDE + j:CMP_STRIDE + j + 1, :]).astype(BF16),
                         w1_ref[kind, CMP_STRIDE + j])
        out += [a, b]
    return out


def _finish_compress(a_k, b_k, a_v, b_v, w2_ref, kg_ref, bd, c, s1, s2):
    nh = a_k.shape[0]
    kc = _dot(_silu(a_k + pltpu.roll(b_k, nh - 1, 0)).astype(BF16), w2_ref[0])
    vc = _dot(_silu(a_v + pltpu.roll(b_v, nh - 1, 0)).astype(BF16), w2_ref[1])
    kc = jnp.concatenate([_headnorm_rope_tok(kc[:, h * LANES:(h + 1) * LANES], kg_ref[...], bd, c, s1, s2)
                          for h in range(KV_W // LANES)], axis=1)
    return kc, vc


def _p_compress_kernel(rows_ref, pe_ref, w1_ref, w2_ref, kg_ref, bd_ref, c_ref, s1_ref, s2_ref,
                       kc_ref, vc_ref, xt, *, nh):
    for c in range(2 * KV_W // LANES):
        xt[c] = rows_ref[0, c * LANES:(c + 1) * LANES, :].T
    a_k, b_k, a_v, b_v = _half_sums(xt, pe_ref, w1_ref, nh)
    kc, vc = _finish_compress(a_k, b_k, a_v, b_v, w2_ref, kg_ref, bd_ref[...], c_ref[...],
                              s1_ref[...], s2_ref[...])
    kc_ref[0] = kc.astype(BF16)
    vc_ref[0] = vc.astype(BF16)


def _p_compress(rows_t, pe4, w1bd, w2bd, kgain2, ctabs):
    bsz, _, t = rows_t.shape
    nh = t // CMP_STRIDE
    c, s1, s2 = ctabs
    w = KV_W
    return pl.pallas_call(
        functools.partial(_p_compress_kernel, nh=nh),
        grid=(bsz,),
        in_specs=[pl.BlockSpec((1, 2 * w, t), lambda b: (b, 0, 0)),
                  _const_spec((2, CMP_BLOCK, w)), _const_spec((2, CMP_BLOCK, w, w)),
                  _const_spec((2, w, w)), _const_spec((1, LANES)), _const_spec((LANES, LANES)),
                  _const_spec((nh, LANES)), _const_spec((nh, LANES)), _const_spec((nh, LANES))],
        out_specs=[pl.BlockSpec((1, nh, w), lambda b: (b, 0, 0)),
                   pl.BlockSpec((1, nh, w), lambda b: (b, 0, 0))],
        out_shape=[jax.ShapeDtypeStruct((bsz, nh, w), BF16), jax.ShapeDtypeStruct((bsz, nh, w), BF16)],
        scratch_shapes=[pltpu.VMEM((2 * w // LANES, t, LANES), F32)],
        compiler_params=_cparams(("parallel",)),
        name="p_compress",
    )(rows_t, pe4, w1bd, w2bd, kgain2, _seg_mean_matrix(), c, s1, s2)


def _stack_heads(q, g):
    return jnp.concatenate([q[:, (g * NSA_HPG + hh) * NSA_HD:(g * NSA_HPG + hh + 1) * NSA_HD]
                            for hh in range(NSA_HPG)], axis=0)


def _stack_gate(gates, g, j):
    return jnp.concatenate([gates[:, (g * NSA_HPG + hh) * 3 + j:(g * NSA_HPG + hh) * 3 + j + 1]
                            for hh in range(NSA_HPG)], axis=0)


def _tile_rows(m, n):
    return jnp.concatenate([m] * n, axis=0)


def _topk_mask(imp, k):
    r, s = imp.shape
    idx = lax.broadcasted_iota(jnp.int32, (r, s), 1)
    rank = jnp.zeros((r, s), F32)
    for s2 in range(s):
        col = imp[:, s2:s2 + 1]
        beats = jnp.where(col > imp, 1.0, jnp.where((col == imp) & (idx > s2), 1.0, 0.0))
        rank = rank + beats
    return rank < k


def _p_attn_kernel(q_ref, gate_ref, kc_ref, vc_ref, kvs_ref, w0, w1, w2, w3, w4, mov_ref, exp_ref,
                   wo_ref, x_ref, y_ref, o_scr, *, tq, t_all):
    i = pl.program_id(1)
    q = q_ref[0]
    gates = gate_ref[0]
    nh = kc_ref.shape[1]
    n_slc = mov_ref.shape[1]
    rows = NSA_HPG * tq
    pos_q = i * tq + lax.broadcasted_iota(jnp.int32, (tq, 1), 0)

    pos_c = lax.broadcasted_iota(jnp.int32, (tq, nh), 1) * CMP_STRIDE + (CMP_BLOCK - 1)
    ok_c = _tile_rows(pos_c <= pos_q, NSA_HPG)
    key_pos = lax.broadcasted_iota(jnp.int32, (tq, t_all), 1)
    causal = key_pos <= pos_q
    blk = lax.broadcasted_iota(jnp.int32, (tq, n_slc), 1)
    cur = pos_q // SLC_BLOCK
    win_blocks = (w4, w3, w2, w1, w0)
    n_wb = len(win_blocks)
    wpos = (i - (n_wb - 1)) * tq + lax.broadcasted_iota(jnp.int32, (tq, n_wb * tq), 1)
    dpos = pos_q - wpos
    ok_w = _tile_rows((dpos >= 0) & (dpos < WINDOW) & (wpos >= 0), NSA_HPG)
    w = KV_W

    for g in range(NSA_KV):
        qg = _stack_heads(q, g).astype(BF16)
        gs = slice(g * NSA_HD, (g + 1) * NSA_HD)
        s = jnp.where(ok_c, _dot_nt(qg, kc_ref[0, :, gs]), NEG)
        e = jnp.exp(s - jnp.max(s, axis=-1, keepdims=True))
        p = jnp.where(ok_c, e / jnp.sum(e, axis=-1, keepdims=True), 0.0)
        o_c = _dot(p.astype(BF16), vc_ref[0, :, gs])
        psum = p[0:tq]
        for hh in range(1, NSA_HPG):
            psum = psum + p[hh * tq:(hh + 1) * tq]
        imp = jnp.dot(psum, mov_ref[...], precision=HI, preferred_element_type=F32)
        imp = jnp.where((blk == cur) | (blk == 0), BIG, imp)
        imp = jnp.where(blk > cur, NEG, imp)
        sel = _topk_mask(imp, SLC_TOP) & (imp > 0.5 * NEG)
        selk = _dot(jnp.where(sel, 1.0, 0.0).astype(BF16), exp_ref[...])
        ok_s = _tile_rows((selk > 0.5) & causal, NSA_HPG)
        s = jnp.where(ok_s, _dot(qg, kvs_ref[0, g * NSA_HD:(g + 1) * NSA_HD, :]), NEG)
        e = jnp.exp(s - jnp.max(s, axis=-1, keepdims=True))
        o_s = _dot_nt(e.astype(BF16), kvs_ref[0, w + g * NSA_HD:w + (g + 1) * NSA_HD, :])
        o_s = o_s / jnp.sum(e, axis=-1, keepdims=True)
        k_w = jnp.concatenate([r[0, g * NSA_HD:(g + 1) * NSA_HD, :] for r in win_blocks], axis=1)
        v_w = jnp.concatenate([r[0, w + g * NSA_HD:w + (g + 1) * NSA_HD, :] for r in win_blocks], axis=1)
        s = jnp.where(ok_w, _dot(qg, k_w), NEG)
        e = jnp.exp(s - jnp.max(s, axis=-1, keepdims=True))
        o_w = _dot_nt(e.astype(BF16), v_w) / jnp.sum(e, axis=-1, keepdims=True)
        o = (_stack_gate(gates, g, 0) * o_c + _stack_gate(gates, g, 1) * o_s
             + _stack_gate(gates, g, 2) * o_w)
        for hh in range(NSA_HPG):
            h = g * NSA_HPG + hh
            o_scr[:, h * NSA_HD:(h + 1) * NSA_HD] = o[hh * tq:(hh + 1) * tq].astype(BF16)

    y_ref[0] = x_ref[0] + _dot(o_scr[...], wo_ref[...])


def _p_attn(q, gates, kc, vc, kvbf, mov, expand, wo_bf, x, tq):
    bsz, t, hq = q.shape
    nh = kc.shape[1]
    w = KV_W
    n_slc = mov.shape[1]
    tok = lambda b, i: (b, i, 0)
    win_specs = [pl.BlockSpec((1, 2 * w, tq), functools.partial(
        lambda b, i, k: (b, 1, jnp.maximum(i - k, 0)), k=k)) for k in range(5)]
    return pl.pallas_call(
        functools.partial(_p_attn_kernel, tq=tq, t_all=t),
        grid=(bsz, t // tq),
        in_specs=[pl.BlockSpec((1, tq, hq), tok), pl.BlockSpec((1, tq, LANES), tok),
                  pl.BlockSpec((1, nh, w), lambda b, i: (b, 0, 0)),
                  pl.BlockSpec((1, nh, w), lambda b, i: (b, 0, 0)),
                  pl.BlockSpec((1, 2 * w, t), lambda b, i: (b, 0, 0))] + win_specs + [
                  _const_spec((nh, n_slc)), _const_spec((n_slc, t)), _const_spec((hq, D_MODEL)),
                  pl.BlockSpec((1, tq, D_MODEL), tok)],
        out_specs=pl.BlockSpec((1, tq, D_MODEL), tok),
        out_shape=jax.ShapeDtypeStruct((bsz, t, D_MODEL), F32),
        scratch_shapes=[pltpu.VMEM((tq, hq), BF16)],
        compiler_params=_cparams(("parallel", "parallel")),
        name="p_attn",
    )(q, gates, kc, vc, kvbf, kvbf, kvbf, kvbf, kvbf, kvbf, mov, expand, wo_bf, x)


PAGES_PER_STEP = 8


def _s_compress_kernel(pt_ref, *refs, nh, page):
    pages = refs[:PAGES_PER_STEP]
    pe_ref, w1_ref, a_ref, b_ref, xt = refs[PAGES_PER_STEP:]
    for k, pg in enumerate(pages):
        for c in range(2 * KV_W // LANES):
            xt[c, k * page:(k + 1) * page, :] = pg[0, c * LANES:(c + 1) * LANES, :].T
    a_k, b_k, a_v, b_v = _half_sums(xt, pe_ref, w1_ref, nh)
    a_ref[0] = jnp.concatenate([a_k, a_v], axis=1)
    b_ref[0] = jnp.concatenate([b_k, b_v], axis=1)


def _s_compress(cache_t, page_table, pe4, w1bd):
    bsz, n_pages = page_table.shape
    page = cache_t.shape[2]
    w = KV_W
    steps = n_pages // PAGES_PER_STEP
    nh = PAGES_PER_STEP * page // CMP_STRIDE
    page_specs = [pl.BlockSpec((1, 2 * w, page), functools.partial(
        lambda b, s, pt, k: (pt[b, s * PAGES_PER_STEP + k], 0, 0), k=k)) for k in range(PAGES_PER_STEP)]
    grid_spec = pltpu.PrefetchScalarGridSpec(
        num_scalar_prefetch=1,
        grid=(bsz, steps),
        in_specs=page_specs + [pl.BlockSpec((2, CMP_BLOCK, w), lambda b, s, pt: (0, 0, 0)),
                               pl.BlockSpec((2, CMP_BLOCK, w, w), lambda b, s, pt: (0, 0, 0, 0))],
        out_specs=[pl.BlockSpec((1, nh, 2 * w), lambda b, s, pt: (b, s, 0)),
                   pl.BlockSpec((1, nh, 2 * w), lambda b, s, pt: (b, s, 0))],
        scratch_shapes=[pltpu.VMEM((2 * w // LANES, PAGES_PER_STEP * page, LANES), F32)])
    out = jax.ShapeDtypeStruct((bsz, steps * nh, 2 * w), F32)
    return pl.pallas_call(
        functools.partial(_s_compress_kernel, nh=nh, page=page),
        grid_spec=grid_spec,
        out_shape=[out, out],
        compiler_params=_cparams(("parallel", "arbitrary")),
        name="s_compress",
    )(page_table, *([cache_t] * PAGES_PER_STEP), pe4, w1bd)


def _s_cmp_kernel(a_ref, b_ref, q_ref, w2_ref, kg_ref, bd_ref, c_ref, s1_ref, s2_ref, mov_ref,
                  oc_ref, sel_ref, *, ts, past):
    ab = a_ref[0]
    bb = b_ref[0]
    w = KV_W
    kc, vc = _finish_compress(ab[:, 0:w], bb[:, 0:w], ab[:, w:2 * w], bb[:, w:2 * w], w2_ref, kg_ref,
                              bd_ref[...], c_ref[...], s1_ref[...], s2_ref[...])
    kc = kc.astype(BF16)
    vc = vc.astype(BF16)
    nh = kc.shape[0]
    q = q_ref[0]
    pos_q = past + lax.broadcasted_iota(jnp.int32, (ts, 1), 0)
    pos_c = lax.broadcasted_iota(jnp.int32, (ts, nh), 1) * CMP_STRIDE + (CMP_BLOCK - 1)
    ok_c = _tile_rows(pos_c <= pos_q, NSA_HPG)
    n_blk = mov_ref.shape[1]
    blk = lax.broadcasted_iota(jnp.int32, (ts, n_blk), 1)
    for g in range(NSA_KV):
        qg = _stack_heads(q, g).astype(BF16)
        gs = slice(g * NSA_HD, (g + 1) * NSA_HD)
        s = jnp.where(ok_c, _dot_nt(qg, kc[:, gs]), NEG)
        e = jnp.exp(s - jnp.max(s, axis=-1, keepdims=True))
        p = jnp.where(ok_c, e / jnp.sum(e, axis=-1, keepdims=True), 0.0)
        o_c = _dot(p.astype(BF16), vc[:, gs])
        for hh in range(NSA_HPG):
            h = g * NSA_HPG + hh
            oc_ref[0, :, h * NSA_HD:(h + 1) * NSA_HD] = o_c[hh * ts:(hh + 1) * ts]
        psum = p[0:ts]
        for hh in range(1, NSA_HPG):
            psum = psum + p[hh * ts:(hh + 1) * ts]
        imp = jnp.dot(psum, mov_ref[...], precision=HI, preferred_element_type=F32)
        imp = jnp.where(blk == 0, BIG, imp)
        sel_ref[0, g * ts:(g + 1) * ts, :] = jnp.where(_topk_mask(imp, SLC_TOP - 1), 1.0, 0.0)


def _s_cmp(a, b, q, w2bd, kgain2, ctabs, mov, past):
    bsz, nh, _ = a.shape
    ts = q.shape[1]
    hq = q.shape[2]
    c, s1, s2 = ctabs
    w = KV_W
    n_blk = mov.shape[1]
    per_b = lambda i: (i, 0, 0)
    return pl.pallas_call(
        functools.partial(_s_cmp_kernel, ts=ts, past=past),
        grid=(bsz,),
        in_specs=[pl.BlockSpec((1, nh, 2 * w), per_b), pl.BlockSpec((1, nh, 2 * w), per_b),
                  pl.BlockSpec((1, ts, hq), per_b),
                  _const_spec((2, w, w)), _const_spec((1, LANES)), _const_spec((LANES, LANES)),
                  _const_spec((nh, LANES)), _const_spec((nh, LANES)), _const_spec((nh, LANES)),
                  _const_spec((nh, n_blk))],
        out_specs=[pl.BlockSpec((1, ts, hq), per_b), pl.BlockSpec((1, NSA_KV * ts, n_blk), per_b)],
        out_shape=[jax.ShapeDtypeStruct((bsz, ts, hq), F32),
                   jax.ShapeDtypeStruct((bsz, NSA_KV * ts, n_blk), F32)],
        compiler_params=_cparams(("parallel",)),
        name="s_cmp",
    )(a, b, q, w2bd, kgain2, _seg_mean_matrix(), c, s1, s2, mov)


def _s_slc_kernel(pt_ref, *refs, ts, page):
    pages = refs[:PAGES_PER_STEP]
    q_ref, sel_ref, new_ref, exp_ref, o_ref, m_scr, l_scr, acc = refs[PAGES_PER_STEP:]
    s_id = pl.program_id(1)
    w = KV_W
    rows = NSA_HPG * ts

    @pl.when(s_id == 0)
    def _():
        m_scr[...] = jnp.full(m_scr.shape, NEG, F32)
        l_scr[...] = jnp.zeros(l_scr.shape, F32)
        acc[...] = jnp.zeros(acc.shape, F32)

    q = q_ref[0]
    sel = sel_ref[0, 0]
    for g in range(NSA_KV):
        qg = _stack_heads(q, g).astype(BF16)
        k_t = jnp.concatenate([pg[0, g * NSA_HD:(g + 1) * NSA_HD, :] for pg in pages], axis=1).astype(BF16)
        v_t = jnp.concatenate([pg[0, w + g * NSA_HD:w + (g + 1) * NSA_HD, :] for pg in pages],
                              axis=1).astype(BF16)
        selk = _dot(sel[g * ts:(g + 1) * ts].astype(BF16), exp_ref[...])
        ok = _tile_rows(selk > 0.5, NSA_HPG)
        s = jnp.where(ok, _dot(qg, k_t), NEG)
        m_old = m_scr[g]
        m_new = jnp.maximum(m_old, jnp.max(s, axis=-1, keepdims=True))
        alpha = jnp.exp(m_old - m_new)
        e = jnp.where(ok, jnp.exp(s - m_new), 0.0)
        l_scr[g] = alpha * l_scr[g] + jnp.sum(e, axis=-1, keepdims=True)
        acc[g] = alpha * acc[g] + _dot_nt(e.astype(BF16), v_t)
        m_scr[g] = m_new

    @pl.when(s_id == pl.num_programs(1) - 1)
    def _():
        t_row = lax.broadcasted_iota(jnp.int32, (ts, ts), 0)
        t_col = lax.broadcasted_iota(jnp.int32, (ts, ts), 1)
        ok_n = _tile_rows(t_col <= t_row, NSA_HPG)
        for g in range(NSA_KV):
            qg = _stack_heads(q, g).astype(BF16)
            k_n = new_ref[0, g * NSA_HD:(g + 1) * NSA_HD, :].astype(BF16)
            v_n = new_ref[0, w + g * NSA_HD:w + (g + 1) * NSA_HD, :].astype(BF16)
            s = jnp.where(ok_n, _dot(qg, k_n), NEG)
            m_old = m_scr[g]
            m_new = jnp.maximum(m_old, jnp.max(s, axis=-1, keepdims=True))
            alpha = jnp.exp(m_old - m_new)
            e = jnp.where(ok_n, jnp.exp(s - m_new), 0.0)
            l_fin = alpha * l_scr[g] + jnp.sum(e, axis=-1, keepdims=True)
            o = (alpha * acc[g] + _dot_nt(e.astype(BF16), v_n)) / l_fin
            for hh in range(NSA_HPG):
                h = g * NSA_HPG + hh
                o_ref[0, :, h * NSA_HD:(h + 1) * NSA_HD] = o[hh * ts:(hh + 1) * ts]


def _s_slc(cache_t, page_table, q, sel_steps, new_t, expand):
    bsz, n_pages = page_table.shape
    page = cache_t.shape[2]
    ts, hq = q.shape[1], q.shape[2]
    w = KV_W
    steps = n_pages // PAGES_PER_STEP
    bps = sel_steps.shape[3]
    page_specs = [pl.BlockSpec((1, 2 * w, page), functools.partial(
        lambda b, s, pt, k: (pt[b, s * PAGES_PER_STEP + k], 1, 0), k=k)) for k in range(PAGES_PER_STEP)]
    rows = NSA_HPG * ts
    grid_spec = pltpu.PrefetchScalarGridSpec(
        num_scalar_prefetch=1,
        grid=(bsz, steps),
        in_specs=page_specs + [
            pl.BlockSpec((1, ts, hq), lambda b, s, pt: (b, 0, 0)),
            pl.BlockSpec((1, 1, NSA_KV * ts, bps), lambda b, s, pt: (b, s, 0, 0)),
            pl.BlockSpec((1, 2 * w, ts), lambda b, s, pt: (b, 0, 0)),
            pl.BlockSpec((bps, PAGES_PER_STEP * page), lambda b, s, pt: (0, 0))],
        out_specs=pl.BlockSpec((1, ts, hq), lambda b, s, pt: (b, 0, 0)),
        scratch_shapes=[pltpu.VMEM((NSA_KV, rows, 1), F32), pltpu.VMEM((NSA_KV, rows, 1), F32),
                        pltpu.VMEM((NSA_KV, rows, NSA_HD), F32)])
    return pl.pallas_call(
        functools.partial(_s_slc_kernel, ts=ts, page=page),
        grid_spec=grid_spec,
        out_shape=jax.ShapeDtypeStruct((bsz, ts, hq), F32),
        compiler_params=_cparams(("parallel", "arbitrary")),
        name="s_slc",
    )(page_table, *([cache_t] * PAGES_PER_STEP), q, sel_steps, new_t, expand)


def _s_win_kernel(q_ref, gate_ref, oc_ref, os_ref, cache_ref, new_ref, o_ref, wout_ref, *, ts):
    q = q_ref[0]
    gates = gate_ref[0]
    cache = cache_ref[0]
    new = new_ref[0]
    wb = cache.shape[1]
    w = KV_W
    t_row = lax.broadcasted_iota(jnp.int32, (ts, wb), 0)
    idx = lax.broadcasted_iota(jnp.int32, (ts, wb), 1)
    dpos = t_row + wb - idx
    ok_old = _tile_rows((dpos >= 0) & (dpos < WINDOW), NSA_HPG)
    n_row = lax.broadcasted_iota(jnp.int32, (ts, ts), 0)
    n_col = lax.broadcasted_iota(jnp.int32, (ts, ts), 1)
    ok_new = _tile_rows((n_col <= n_row) & (n_row - n_col < WINDOW), NSA_HPG)
    for g in range(NSA_KV):
        qg = _stack_heads(q, g).astype(BF16)
        gk = slice(g * NSA_HD, (g + 1) * NSA_HD)
        gv = slice(w + g * NSA_HD, w + (g + 1) * NSA_HD)
        s_old = jnp.where(ok_old, _dot(qg, cache[gk].astype(BF16)), NEG)
        s_new = jnp.where(ok_new, _dot(qg, new[gk].astype(BF16)), NEG)
        m = jnp.maximum(jnp.max(s_old, axis=-1, keepdims=True), jnp.max(s_new, axis=-1, keepdims=True))
        e_old = jnp.exp(s_old - m)
        e_new = jnp.exp(s_new - m)
        den = jnp.sum(e_old, axis=-1, keepdims=True) + jnp.sum(e_new, axis=-1, keepdims=True)
        o_w = (_dot_nt(e_old.astype(BF16), cache[gv].astype(BF16))
               + _dot_nt(e_new.astype(BF16), new[gv].astype(BF16))) / den
        for hh in range(NSA_HPG):
            h = g * NSA_HPG + hh
            hs = slice(h * NSA_HD, (h + 1) * NSA_HD)
            o_ref[0, :, hs] = (gates[:, 3 * h:3 * h + 1] * oc_ref[0, :, hs]
                               + gates[:, 3 * h + 1:3 * h + 2] * os_ref[0, :, hs]
                               + gates[:, 3 * h + 2:3 * h + 3] * o_w[hh * ts:(hh + 1) * ts])
    wout_ref[0] = pltpu.roll(cache, wb - ts, 1)
    wout_ref[0, :, wb - ts:wb] = new


def _s_win(q, gates, o_c, o_s, cache_win_t, new_win_t):
    bsz, ts, hq = q.shape
    wb = cache_win_t.shape[2]
    w = KV_W
    per_b = lambda b: (b, 0, 0)
    return pl.pallas_call(
        functools.partial(_s_win_kernel, ts=ts),
        grid=(bsz,),
        in_specs=[pl.BlockSpec((1, ts, hq), per_b), pl.BlockSpec((1, ts, LANES), per_b),
                  pl.BlockSpec((1, ts, hq), per_b), pl.BlockSpec((1, ts, hq), per_b),
                  pl.BlockSpec((1, 2 * w, wb), per_b), pl.BlockSpec((1, 2 * w, ts), per_b)],
        out_specs=[pl.BlockSpec((1, ts, hq), per_b), pl.BlockSpec((1, 2 * w, wb), per_b)],
        out_shape=[jax.ShapeDtypeStruct((bsz, ts, hq), F32),
                   jax.ShapeDtypeStruct((bsz, 2 * w, wb), F32)],
        compiler_params=_cparams(("parallel",)),
        name="s_win",
    )(q, gates, o_c, o_s, cache_win_t, new_win_t)


def _block_overlap(n_cmp_rows, n_cmp, n_slc):
    cs = jnp.arange(n_cmp_rows)[:, None] * CMP_STRIDE
    ss = jnp.arange(n_slc)[None, :] * SLC_BLOCK
    ov = jnp.minimum(cs + CMP_BLOCK, ss + SLC_BLOCK) - jnp.maximum(cs, ss)
    ov = jnp.clip(ov, 0, None).astype(F32) / CMP_BLOCK
    return jnp.where(jnp.arange(n_cmp_rows)[:, None] < n_cmp, ov, 0.0)


def _expand_matrix(n_blk, n_keys):
    return (jnp.arange(n_keys)[None, :] // SLC_BLOCK == jnp.arange(n_blk)[:, None]).astype(BF16)


def _pad_rows(a, rows):
    return jnp.pad(a, ((0, rows - a.shape[0]), (0, 0)))


def _pad_lanes(a, lanes=LANES):
    return jnp.pad(a, ((0, 0), (0, lanes - a.shape[1])))


def _blockdiag4(m):
    eye = jnp.eye(NSA_KV, dtype=m.dtype)
    out = jnp.einsum('gh,...de->...gdhe', eye, m)
    return out.reshape(m.shape[:-2] + (KV_W, KV_W))


def _layer_tail(x, layer, kv, tq_mem, tm_ffn, p):
    bsz, t, d = x.shape
    x = _memattn(x, tq_mem, p['norm_mem'][layer][None], p['mem_w_q'][layer].astype(BF16),
                 p['mem_q_norm'][layer][None], kv, p['mem_w_o'][layer].astype(BF16))
    x2 = _ffn(x.reshape(bsz * t, d), tm_ffn, p['norm_ffn'][layer][None],
              p['ffn_w_gate'][layer].astype(BF16), p['ffn_w_up'][layer].astype(BF16),
              p['ffn_w_down'][layer].astype(BF16))
    return x2.reshape(bsz, t, d)


def _run_mixer0(x, tq, sconv_st, ssdc_st, ssd_st, p):
    w_in = _pad_lanes(p['ab_w_in'][0], IN0_PAD).astype(BF16)
    front = lambda s: jnp.pad(s, ((0, 0), (CARRY - s.shape[1], 0), (0, 0)))
    y, sa, sb, hn = _mixer0(
        x, tq, p['norm_mix'][0][None], w_in,
        _pad_rows(p['ab_sconv_w'][0], SUBLANES), _pad_rows(p['ab_ssd_conv_w'][0], SUBLANES),
        p['ab_ssd_conv_b'], _pad_lanes(p['ab_dt_bias']), _pad_lanes(p['ab_a_log']),
        _pad_lanes(p['ab_d']), p['ab_ssd_norm'], p['ab_w_out'][0].astype(BF16),
        front(sconv_st), front(ssdc_st), ssd_st)
    return y, sa[:, CARRY - (SCONV_W - 1):], sb[:, CARRY - (SSD_CONV_W - 1):], hn


def _nsa_weights(p):
    w_in = p['nsa_w_in'][0]
    hq = NSA_HEADS * NSA_HD
    wq = w_in[:, :hq].astype(BF16)
    wkv_t = w_in[:, hq:hq + 6 * KV_W].T.astype(BF16)
    wg = _pad_lanes(w_in[:, hq + 6 * KV_W:]).astype(BF16)
    qgain2 = jnp.tile(p['nsa_q_norm'][0], 2)[None]
    kn = p['nsa_k_norm'][0]
    kgain = jnp.stack([kn[1], kn[2]])[:, :, None]
    kcgain2 = jnp.tile(kn[0], 2)[None]
    pe4 = jnp.tile(p['nsa_cmp_pe'][0], (1, 1, NSA_KV))
    w1bd = _blockdiag4(p['nsa_cmp_w1'][0]).astype(BF16)
    w2bd = _blockdiag4(p['nsa_cmp_w2'][0]).astype(BF16)
    return wq, wg, wkv_t, qgain2, kgain, kcgain2, pe4, w1bd, w2bd


def kernel(x_prompt, x_sample, mem_prompt, state_sconv, state_ssd_conv, state_ssd, cache_nsa_kv,
           cache_nsa_win, cache_mem_kv, page_table, norm_mix, norm_mem, norm_memsrc, norm_ffn,
           ab_w_in, ab_sconv_w, ab_ssd_conv_w, ab_ssd_conv_b, ab_dt_bias, ab_a_log, ab_d, ab_ssd_norm,
           ab_w_out, nsa_w_in, nsa_q_norm, nsa_k_norm, nsa_cmp_pe, nsa_cmp_w1, nsa_cmp_w2, nsa_w_out,
           mem_w_q, mem_q_norm, mem_w_kv, mem_k_norm, mem_w_o, ffn_w_gate, ffn_w_up, ffn_w_down):
    p = dict(norm_mix=norm_mix, norm_mem=norm_mem, norm_ffn=norm_ffn, ab_w_in=ab_w_in,
             ab_sconv_w=ab_sconv_w, ab_ssd_conv_w=ab_ssd_conv_w, ab_ssd_conv_b=ab_ssd_conv_b,
             ab_dt_bias=ab_dt_bias, ab_a_log=ab_a_log, ab_d=ab_d, ab_ssd_norm=ab_ssd_norm,
             ab_w_out=ab_w_out, nsa_w_in=nsa_w_in, nsa_q_norm=nsa_q_norm, nsa_k_norm=nsa_k_norm,
             nsa_cmp_pe=nsa_cmp_pe, nsa_cmp_w1=nsa_cmp_w1, nsa_cmp_w2=nsa_cmp_w2, nsa_w_out=nsa_w_out,
             mem_w_q=mem_w_q, mem_q_norm=mem_q_norm, mem_w_o=mem_w_o,
             ffn_w_gate=ffn_w_gate, ffn_w_up=ffn_w_up, ffn_w_down=ffn_w_down)
    bp, t, d = x_prompt.shape
    bs, ts, _ = x_sample.shape
    n_mem = mem_prompt.shape[1]
    depth = norm_mix.shape[0]
    assert depth == 2 and ab_w_in.shape[0] == 1 and nsa_w_in.shape[0] == 1
    w = KV_W
    hq = NSA_HEADS * NSA_HD
    wq, wg, wkv_t, qgain2, kgain, kcgain2, pe4, w1bd, w2bd = _nsa_weights(p)
    wo_nsa = nsa_w_out[0].astype(BF16)

    mem2d = mem_prompt.reshape(bp * n_mem, d)
    p_mem = [_memkv(mem2d, norm_memsrc[l][None], mem_w_kv[l].astype(BF16), mem_k_norm[l][None])
             .reshape(bp, n_mem, 2 * MEM_HEADS * MEM_HD) for l in range(depth)]
    p_mem_kv = jnp.stack(p_mem).reshape(depth, bp, n_mem, 2, MEM_HEADS, MEM_HD)
    s_mem = cache_mem_kv.reshape(depth, bs, n_mem, 2 * MEM_HEADS * MEM_HD)

    tq0 = min(256, t)
    tq_mem = min(512, t)
    tm_ffn = min(1024, bp * t)
    zeros = lambda *s: jnp.zeros(s, F32)
    x, p_sconv, p_ssdc, p_ssd = _run_mixer0(
        x_prompt, tq0, zeros(bp, SCONV_W - 1, d), zeros(bp, SSD_CONV_W - 1, SSD_CONV_DIM),
        zeros(bp, SSD_HEADS, SSD_HD, SSD_N), p)
    x = _layer_tail(x, 0, p_mem[0], tq_mem, tm_ffn, p)

    pos_p = jnp.arange(t)
    tabs_p = _rope_tables(pos_p)
    tm_proj = min(512, t)
    q, gates, rows_t, win_t, kvbf = _nsa_proj(x, tm_proj, norm_mix[1][None], wq, wg, wkv_t, qgain2,
                                              kgain, tabs_p)
    nh = t // CMP_STRIDE
    n_cmp = nh - 1
    pos_c = jnp.arange(nh) * CMP_STRIDE + CMP_BLOCK - 1
    kc, vc = _p_compress(rows_t, pe4, w1bd, w2bd, kcgain2, _rope_tables(pos_c)[:3])
    n_slc = -(-t // SLC_BLOCK)
    tq_a = min(128, t)
    x = _p_attn(q, gates, kc, vc, kvbf, _block_overlap(nh, n_cmp, n_slc), _expand_matrix(n_slc, t),
                wo_nsa, x, tq_a)
    y_prompt = _layer_tail(x, 1, p_mem[1], tq_mem, tm_ffn, p)
    p_rows = rows_t.reshape(bp, 1, 4, NSA_KV, NSA_HD, t).transpose(0, 5, 1, 2, 3, 4)
    wl = min(WINDOW, t)
    p_win = win_t[:, :, t - wl:].reshape(1, bp, 2, NSA_KV, NSA_HD, wl).transpose(0, 1, 5, 2, 3, 4)

    n_pool, page = cache_nsa_kv.shape[0], cache_nsa_kv.shape[1]
    n_pages = page_table.shape[1]
    past = n_pages * page
    assert ts <= CMP_STRIDE and past % SLC_BLOCK == 0 and n_pages % PAGES_PER_STEP == 0
    xs, s_sconv, s_ssdc, s_ssd = _run_mixer0(x_sample, ts, state_sconv[0], state_ssd_conv[0],
                                             state_ssd[0], p)
    xs = _layer_tail(xs, 0, s_mem[0], ts, bs * ts, p)

    pos_s = jnp.tile(past + jnp.arange(ts), bs)
    qs, gates_s, rows_s, win_s, _ = _nsa_proj(xs.reshape(1, bs * ts, d), bs * ts, norm_mix[1][None],
                                              wq, wg, wkv_t, qgain2, kgain, _rope_tables(pos_s))
    qs = qs.reshape(bs, ts, hq)
    gates_s = gates_s.reshape(bs, ts, LANES)
    rows_s = rows_s[0].reshape(4 * w, bs, ts).transpose(1, 0, 2)
    win_s = win_s[0].reshape(2 * w, bs, ts).transpose(1, 0, 2)
    cache_t = cache_nsa_kv.transpose(0, 2, 3, 4, 5, 1).reshape(n_pool, 4 * w, page)
    a_sum, b_sum = _s_compress(cache_t, page_table, pe4, w1bd)
    nh_s = past // CMP_STRIDE
    n_cmp_s = (past + ts) // CMP_STRIDE - 1
    pos_cs = jnp.arange(nh_s) * CMP_STRIDE + CMP_BLOCK - 1
    n_blk = past // SLC_BLOCK
    o_c, sel = _s_cmp(a_sum, b_sum, qs, w2bd, kcgain2, _rope_tables(pos_cs)[:3],
                      _block_overlap(nh_s, n_cmp_s, n_blk), past)
    steps = n_pages // PAGES_PER_STEP
    bps = n_blk // steps
    sel_steps = sel.reshape(bs, NSA_KV * ts, steps, bps).transpose(0, 2, 1, 3)
    o_s = _s_slc(cache_t, page_table, qs, sel_steps, rows_s[:, 2 * w:], _expand_matrix(bps, PAGES_PER_STEP * page))
    wb = cache_nsa_win.shape[2]
    cache_win_t = cache_nsa_win[0].transpose(0, 2, 3, 4, 1).reshape(bs, 2 * w, wb)
    o_att, new_win_t = _s_win(qs, gates_s, o_c, o_s, cache_win_t, win_s)
    xs = _proj_res(o_att.reshape(bs * ts, hq), wo_nsa, xs.reshape(bs * ts, d)).reshape(bs, ts, d)
    y_sample = _layer_tail(xs, 1, s_mem[1], ts, bs * ts, p)
    s_rows = rows_s.transpose(0, 2, 1).reshape(bs, ts, 1, 4, NSA_KV, NSA_HD)
    s_win = new_win_t.reshape(1, bs, 2, NSA_KV, NSA_HD, wb).transpose(0, 1, 5, 2, 3, 4)

    return (y_prompt, y_sample, p_sconv[None], p_ssdc[None], p_ssd[None], p_rows, p_win, p_mem_kv,
            s_sconv[None], s_ssdc[None], s_ssd[None], s_rows, s_win)
```

```python
import functools
import math

import jax
import jax.numpy as jnp
from jax import lax
from jax.experimental import pallas as pl
from jax.experimental.pallas import tpu as pltpu

F32 = jnp.float32
BF16 = jnp.bfloat16
HI = lax.Precision.HIGHEST

EPS = 1e-6
ROPE_THETA = 500000.0
NEG = -1e30
BIG = 1e9

LANES = 128
SUBLANES = 8
VMEM_LIMIT = 56 * 1024 * 1024

D_MODEL = 1024
SCONV_W = 3
SSD_HEADS = 16
SSD_HD = 64
SSD_DIM = SSD_HEADS * SSD_HD
SSD_GROUPS = 4
SSD_HPG = SSD_HEADS // SSD_GROUPS
SSD_N = 128
SSD_CONV_W = 4
SSD_CONV_DIM = SSD_DIM + 2 * SSD_GROUPS * SSD_N
IN0_DIM = 3 * D_MODEL + SSD_DIM + SSD_CONV_DIM + SSD_HEADS
IN0_PAD = 3 * D_MODEL + SSD_DIM + SSD_CONV_DIM + LANES
NSA_HEADS = 16
NSA_KV = 4
NSA_HD = 64
NSA_HPG = NSA_HEADS // NSA_KV
ROT_HALF = NSA_HD // 8
CMP_BLOCK = 32
CMP_STRIDE = 16
SLC_BLOCK = 64
SLC_TOP = 16
WINDOW = 512
KV_W = NSA_KV * NSA_HD
MEM_HEADS = 4
MEM_HD = 128
CARRY = SUBLANES


def _cparams(sem):
    return pltpu.CompilerParams(dimension_semantics=sem, vmem_limit_bytes=VMEM_LIMIT)


def _const_spec(shape):
    n = len(shape)
    return pl.BlockSpec(shape, lambda *_: (0,) * n)


def _rms(x, g):
    ms = jnp.mean(x * x, axis=-1, keepdims=True)
    return x * lax.rsqrt(ms + EPS) * g


def _dot(a, b):
    return jnp.dot(a, b, preferred_element_type=F32)


def _dot_nt(a, b, precision=None):
    return lax.dot_general(a, b, (((1,), (1,)), ((), ())), precision=precision,
                           preferred_element_type=F32)


def _dot_tn(a, b):
    return lax.dot_general(a, b, (((0,), (0,)), ((), ())), preferred_element_type=F32)


def _silu(x):
    return x * jax.nn.sigmoid(x)


def _softplus(x):
    return jnp.maximum(x, 0.0) + jnp.log1p(jnp.exp(-jnp.abs(x)))


def _seg_mean_matrix():
    r = lax.broadcasted_iota(jnp.int32, (LANES, LANES), 0) // NSA_HD
    c = lax.broadcasted_iota(jnp.int32, (LANES, LANES), 1) // NSA_HD
    return jnp.where(r == c, 1.0 / NSA_HD, 0.0).astype(F32)


def _headnorm_rope_tok(x, gain, bd, c, s1, s2):
    ms = jnp.dot(x * x, bd, precision=HI, preferred_element_type=F32)
    xn = x * lax.rsqrt(ms + EPS) * gain
    return xn * c + pltpu.roll(xn, LANES - ROT_HALF, 1) * s1 + pltpu.roll(xn, ROT_HALF, 1) * s2


def _headnorm_rope_ch(x, gain_col, cos_t, sin_t):
    ms = jnp.mean(x * x, axis=0, keepdims=True)
    xn = x * lax.rsqrt(ms + EPS) * gain_col
    x1 = xn[0:ROT_HALF]
    x2 = xn[ROT_HALF:2 * ROT_HALF]
    return x1 * cos_t - x2 * sin_t, x2 * cos_t + x1 * sin_t, xn[2 * ROT_HALF:]


def _rope_tables(pos):
    inv = ROPE_THETA ** (-jnp.arange(ROT_HALF, dtype=F32) / ROT_HALF)
    ang = pos.astype(F32)[:, None] * inv[None, :]
    cos, sin = jnp.cos(ang), jnp.sin(ang)
    n = pos.shape[0]
    ones = jnp.ones((n, NSA_HD - 2 * ROT_HALF), F32)
    zeros8 = jnp.zeros((n, ROT_HALF), F32)
    zeros = jnp.zeros((n, NSA_HD - 2 * ROT_HALF), F32)
    c = jnp.concatenate([cos, cos, ones], axis=1)
    s1 = jnp.concatenate([-sin, zeros8, zeros], axis=1)
    s2 = jnp.concatenate([zeros8, sin, zeros], axis=1)
    tile2 = lambda t: jnp.concatenate([t, t], axis=1)
    return tile2(c), tile2(s1), tile2(s2), cos.T, sin.T


MEM_SLOTS = 2 * MEM_HEADS


def _memkv_kernel(x_ref, g_ref, w_ref, kn_ref, o_ref, *, tm):
    hn = _rms(x_ref[...], g_ref[0]).astype(BF16)
    kv = _dot(hn, w_ref[0])
    for c in range(MEM_SLOTS):
        blk = kv[:, c * MEM_HD:(c + 1) * MEM_HD]
        if c < MEM_HEADS:
            blk = _rms(blk, kn_ref[0])
        o_ref[0, pl.ds(c, tm, stride=MEM_SLOTS), :] = blk


def _memkv(mem2d, g, w_bf, kn):
    m = mem2d.shape[0]
    depth = w_bf.shape[0]
    tm = min(512, m)
    n = w_bf.shape[2]
    return pl.pallas_call(
        functools.partial(_memkv_kernel, tm=tm),
        grid=(depth, m // tm),
        in_specs=[pl.BlockSpec((tm, D_MODEL), lambda l, i: (i, 0)),
                  pl.BlockSpec((1, 1, D_MODEL), lambda l, i: (l, 0, 0)),
                  pl.BlockSpec((1, D_MODEL, n), lambda l, i: (l, 0, 0)),
                  pl.BlockSpec((1, 1, MEM_HD), lambda l, i: (l, 0, 0))],
        out_specs=pl.BlockSpec((1, tm * MEM_SLOTS, MEM_HD), lambda l, i: (l, i, 0)),
        out_shape=jax.ShapeDtypeStruct((depth, m * MEM_SLOTS, MEM_HD), F32),
        compiler_params=_cparams(("parallel", "parallel")),
        name="memkv",
    )(mem2d, g, w_bf, kn)


def _mixer0_kernel(x_ref, g_ref, win_ref, scw_ref, cw_ref, cb_ref, dtb_ref, alog_ref, dsk_ref,
                   ssdn_ref, wout_ref, sa0_ref, sb0_ref, h0_ref,
                   y_ref, sa_ref, sb_ref, hout_ref,
                   proj, ca, cbuf, hst, ycat, *, tq, n_col_chunks):
    i = pl.program_id(1)
    last = pl.num_programs(1) - 1

    @pl.when(i == 0)
    def _():
        ca[0:CARRY, :] = sa0_ref[0]
        cbuf[0:CARRY, :] = sb0_ref[0]
        hst[...] = h0_ref[0]

    x = x_ref[0]
    hn = _rms(x, g_ref[...]).astype(BF16)
    cw_cols = IN0_PAD // n_col_chunks
    for c in range(n_col_chunks):
        sl = slice(c * cw_cols, (c + 1) * cw_cols)
        proj[:, sl] = _dot(hn, win_ref[:, sl])

    d = D_MODEL
    ca[CARRY:CARRY + tq, :] = proj[:, 2 * d:3 * d] * proj[:, 0:d]
    conv_u = ca[pl.ds(CARRY - 2, tq), :] * scw_ref[0:1, :]
    for k in range(1, SCONV_W):
        conv_u = conv_u + ca[pl.ds(CARRY - 2 + k, tq), :] * scw_ref[k:k + 1, :]
    ycat[:, 0:d] = (proj[:, d:2 * d] * conv_u).astype(BF16)

    x0 = 3 * d + SSD_DIM
    cbuf[CARRY:CARRY + tq, :] = proj[:, x0:x0 + SSD_CONV_DIM]
    xbc = cbuf[pl.ds(CARRY - 3, tq), :] * cw_ref[0:1, :]
    for k in range(1, SSD_CONV_W):
        xbc = xbc + cbuf[pl.ds(CARRY - 3 + k, tq), :] * cw_ref[k:k + 1, :]
    xbc = _silu(xbc + cb_ref[...])

    dt = _softplus(proj[:, x0 + SSD_CONV_DIM:IN0_PAD] + dtb_ref[...])
    a_neg = -jnp.exp(alog_ref[...])
    row = lax.broadcasted_iota(jnp.int32, (tq, tq), 0)
    col = lax.broadcasted_iota(jnp.int32, (tq, tq), 1)
    causal = row >= col
    tril = jnp.where(causal, 1.0, 0.0).astype(F32)
    cum = jnp.dot(tril, dt * a_neg, precision=HI, preferred_element_type=F32)
    eye = jnp.where(lax.broadcasted_iota(jnp.int32, (LANES, LANES), 0)
                    == lax.broadcasted_iota(jnp.int32, (LANES, LANES), 1), 1.0, 0.0).astype(F32)
    cum_t = _dot_nt(eye, cum, precision=HI)
    cum_last = cum[tq - 1:tq, :]
    e_cum = jnp.exp(cum)
    e_last = jnp.exp(cum_last)
    e_end = jnp.exp(cum_last - cum)
    z = proj[:, 3 * d:3 * d + SSD_DIM]
    dsk = dsk_ref[...]

    for g in range(SSD_GROUPS):
        b_g = xbc[:, SSD_DIM + g * SSD_N:SSD_DIM + (g + 1) * SSD_N].astype(BF16)
        c_g = xbc[:, SSD_DIM + (SSD_GROUPS + g) * SSD_N:
                  SSD_DIM + (SSD_GROUPS + g + 1) * SSD_N].astype(BF16)
        cb_g = _dot_nt(c_g, b_g)
        y_parts = []
        for hh in range(SSD_HPG):
            h = g * SSD_HPG + hh
            x_h = xbc[:, h * SSD_HD:(h + 1) * SSD_HD]
            xdt = x_h * dt[:, h:h + 1]
            lmat = jnp.where(causal, jnp.exp(cum[:, h:h + 1] - cum_t[h:h + 1, :]), 0.0)
            y_h = _dot((cb_g * lmat).astype(BF16), xdt.astype(BF16))
            st = hst[h]
            y_h = y_h + e_cum[:, h:h + 1] * _dot_nt(c_g, st.astype(BF16))
            hst[h] = e_last[:, h:h + 1] * st + _dot_tn((xdt * e_end[:, h:h + 1]).astype(BF16), b_g)
            y_parts.append(y_h + dsk[:, h:h + 1] * x_h)
        yg = jnp.concatenate(y_parts, axis=1)
        gs = slice(g * SSD_HPG * SSD_HD, (g + 1) * SSD_HPG * SSD_HD)
        yg = yg * _silu(z[:, gs])
        yg = yg * lax.rsqrt(jnp.mean(yg * yg, axis=-1, keepdims=True) + EPS)
        ycat[:, d + gs.start:d + gs.stop] = (yg * ssdn_ref[:, gs]).astype(BF16)

    y_ref[0] = x + _dot(ycat[...], wout_ref[...])

    ca[0:CARRY, :] = ca[tq:tq + CARRY, :]
    cbuf[0:CARRY, :] = cbuf[tq:tq + CARRY, :]

    @pl.when(i == last)
    def _():
        sa_ref[0] = ca[0:CARRY, :]
        sb_ref[0] = cbuf[0:CARRY, :]
        hout_ref[0] = hst[...]


def _mixer0(x, tq, g, w_in_bf, scw, cw, cb, dtb, alog, dsk, ssdn, w_out_bf, sa0, sb0, h0):
    bsz, t, d = x.shape
    n_col_chunks = 7
    kern = functools.partial(_mixer0_kernel, tq=tq, n_col_chunks=n_col_chunks)
    per_b = lambda b, i: (b, 0, 0)
    return pl.pallas_call(
        kern,
        grid=(bsz, t // tq),
        in_specs=[pl.BlockSpec((1, tq, d), lambda b, i: (b, i, 0)),
                  _const_spec((1, d)), _const_spec((d, IN0_PAD)),
                  _const_spec((SUBLANES, d)), _const_spec((SUBLANES, SSD_CONV_DIM)),
                  _const_spec((1, SSD_CONV_DIM)), _const_spec((1, LANES)), _const_spec((1, LANES)),
                  _const_spec((1, LANES)), _const_spec((1, SSD_DIM)),
                  _const_spec((d + SSD_DIM, d)),
                  pl.BlockSpec((1, CARRY, d), per_b),
                  pl.BlockSpec((1, CARRY, SSD_CONV_DIM), per_b),
                  pl.BlockSpec((1, SSD_HEADS, SSD_HD, SSD_N), lambda b, i: (b, 0, 0, 0))],
        out_specs=[pl.BlockSpec((1, tq, d), lambda b, i: (b, i, 0)),
                   pl.BlockSpec((1, CARRY, d), per_b),
                   pl.BlockSpec((1, CARRY, SSD_CONV_DIM), per_b),
                   pl.BlockSpec((1, SSD_HEADS, SSD_HD, SSD_N), lambda b, i: (b, 0, 0, 0))],
        out_shape=[jax.ShapeDtypeStruct((bsz, t, d), F32),
                   jax.ShapeDtypeStruct((bsz, CARRY, d), F32),
                   jax.ShapeDtypeStruct((bsz, CARRY, SSD_CONV_DIM), F32),
                   jax.ShapeDtypeStruct((bsz, SSD_HEADS, SSD_HD, SSD_N), F32)],
        scratch_shapes=[pltpu.VMEM((tq, IN0_PAD), F32),
                        pltpu.VMEM((CARRY + tq, d), F32),
                        pltpu.VMEM((CARRY + tq, SSD_CONV_DIM), F32),
                        pltpu.VMEM((SSD_HEADS, SSD_HD, SSD_N), F32),
                        pltpu.VMEM((tq, d + SSD_DIM), BF16)],
        compiler_params=_cparams(("parallel", "arbitrary")),
        name="mixer0",
    )(x, g, w_in_bf, scw, cw, cb, dtb, alog, dsk, ssdn, w_out_bf, sa0, sb0, h0)


def _memattn_kernel(x_ref, g_ref, wq_ref, qn_ref, kv_ref, wo_ref, o_ref, *, n_mem):
    x = x_ref[0]
    hn = _rms(x, g_ref[...]).astype(BF16)
    q = _dot(hn, wq_ref[...])
    outs = []
    for h in range(MEM_HEADS):
        sl = slice(h * MEM_HD, (h + 1) * MEM_HD)
        k_h = kv_ref[0, 0, pl.ds(h, n_mem, stride=MEM_SLOTS), :].astype(BF16)
        v_h = kv_ref[0, 0, pl.ds(MEM_HEADS + h, n_mem, stride=MEM_SLOTS), :].astype(BF16)
        qh = _rms(q[:, sl], qn_ref[...]).astype(BF16)
        s = _dot_nt(qh, k_h) * (MEM_HD ** -0.5)
        m = jnp.max(s, axis=-1, keepdims=True)
        e = jnp.exp(s - m)
        p = e / jnp.sum(e, axis=-1, keepdims=True)
        outs.append(_dot(p.astype(BF16), v_h))
    o = jnp.concatenate(outs, axis=1).astype(BF16)
    o_ref[0] = x + _dot(o, wo_ref[...])


def _memattn(x, tq, g, wq_bf, qn, kv4, layer, wo_bf):
    bsz, t, d = x.shape
    rows = kv4.shape[2]
    hw = MEM_HEADS * MEM_HD
    return pl.pallas_call(
        functools.partial(_memattn_kernel, n_mem=rows // MEM_SLOTS),
        grid=(bsz, t // tq),
        in_specs=[pl.BlockSpec((1, tq, d), lambda b, i: (b, i, 0)),
                  _const_spec((1, d)), _const_spec((d, hw)), _const_spec((1, MEM_HD)),
                  pl.BlockSpec((1, 1, rows, MEM_HD), lambda b, i: (layer, b, 0, 0)),
                  _const_spec((hw, d))],
        out_specs=pl.BlockSpec((1, tq, d), lambda b, i: (b, i, 0)),
        out_shape=jax.ShapeDtypeStruct((bsz, t, d), F32),
        compiler_params=_cparams(("parallel", "parallel")),
        name="memattn",
    )(x, g, wq_bf, qn, kv4, wo_bf)


def _ffn_kernel(x_ref, g_ref, wg_ref, wu_ref, wd_ref, o_ref, hn_scr, acc):
    f = pl.program_id(1)

    @pl.when(f == 0)
    def _():
        hn_scr[...] = _rms(x_ref[...], g_ref[...]).astype(BF16)
        acc[...] = jnp.zeros_like(acc)

    hn = hn_scr[...]
    a = _silu(_dot(hn, wg_ref[...])) * _dot(hn, wu_ref[...])
    acc[...] += _dot(a.astype(BF16), wd_ref[...])

    @pl.when(f == pl.num_programs(1) - 1)
    def _():
        o_ref[...] = x_ref[...] + acc[...]


def _ffn(x2d, tm, g, wg_bf, wu_bf, wd_bf):
    m, d = x2d.shape
    dff = wg_bf.shape[1]
    tf = 256
    return pl.pallas_call(
        _ffn_kernel,
        grid=(m // tm, dff // tf),
        in_specs=[pl.BlockSpec((tm, d), lambda i, f: (i, 0)), _const_spec((1, d)),
                  pl.BlockSpec((d, tf), lambda i, f: (0, f)),
                  pl.BlockSpec((d, tf), lambda i, f: (0, f)),
                  pl.BlockSpec((tf, d), lambda i, f: (f, 0))],
        out_specs=pl.BlockSpec((tm, d), lambda i, f: (i, 0)),
        out_shape=jax.ShapeDtypeStruct((m, d), F32),
        scratch_shapes=[pltpu.VMEM((tm, d), BF16), pltpu.VMEM((tm, d), F32)],
        compiler_params=_cparams(("parallel", "arbitrary")),
        name="ffn",
    )(x2d, g, wg_bf, wu_bf, wd_bf)


def _proj_res_kernel(a_ref, w_ref, x_ref, o_ref):
    o_ref[...] = x_ref[...] + _dot(a_ref[...].astype(BF16), w_ref[...])


def _proj_res(a2d, w_bf, x2d):
    m, k = a2d.shape
    n = w_bf.shape[1]
    tm = min(256, m)
    return pl.pallas_call(
        _proj_res_kernel,
        grid=(m // tm,),
        in_specs=[pl.BlockSpec((tm, k), lambda i: (i, 0)), _const_spec((k, n)),
                  pl.BlockSpec((tm, n), lambda i: (i, 0))],
        out_specs=pl.BlockSpec((tm, n), lambda i: (i, 0)),
        out_shape=jax.ShapeDtypeStruct((m, n), F32),
        compiler_params=_cparams(("parallel",)),
        name="proj_res",
    )(a2d, w_bf, x2d)


def _nsa_proj_kernel(x_ref, g_ref, wq_ref, wg_ref, wkv_ref, bd_ref, qg_ref, c_ref, s1_ref, s2_ref,
                     kg_ref, cos_ref, sin_ref,
                     q_ref, gate_ref, rows_ref, win_ref, kvbf_ref):
    hn = _rms(x_ref[0], g_ref[...]).astype(BF16)
    q = _dot(hn, wq_ref[...])
    bd = bd_ref[...]
    scale = NSA_HD ** -0.5
    for c in range(NSA_HEADS * NSA_HD // LANES):
        sl = slice(c * LANES, (c + 1) * LANES)
        q_ref[0, :, sl] = scale * _headnorm_rope_tok(q[:, sl], qg_ref[...], bd, c_ref[...],
                                                     s1_ref[...], s2_ref[...])
    gate_ref[0] = jax.nn.sigmoid(_dot(hn, wg_ref[...]))
    kv_t = _dot_nt(wkv_ref[...], hn)
    cos_t, sin_t = cos_ref[...], sin_ref[...]

    def put_normed(src_row, gain_idx, dst_ref, dst_row, bf_row):
        for g in range(NSA_KV):
            r1, r2, rest = _headnorm_rope_ch(kv_t[src_row + g * NSA_HD:src_row + (g + 1) * NSA_HD],
                                             kg_ref[gain_idx], cos_t, sin_t)
            full = jnp.concatenate([r1, r2, rest], axis=0)
            dst_ref[0, dst_row + g * NSA_HD:dst_row + (g + 1) * NSA_HD, :] = full
            kvbf_ref[0, bf_row + g * NSA_HD:bf_row + (g + 1) * NSA_HD, :] = full.astype(BF16)

    w = KV_W
    rows_ref[0, 0:2 * w, :] = kv_t[0:2 * w]
    put_normed(2 * w, 0, rows_ref, 2 * w, 0)
    rows_ref[0, 3 * w:4 * w, :] = kv_t[3 * w:4 * w]
    kvbf_ref[0, w:2 * w, :] = kv_t[3 * w:4 * w].astype(BF16)
    put_normed(4 * w, 1, win_ref, 0, 2 * w)
    win_ref[0, w:2 * w, :] = kv_t[5 * w:6 * w]
    kvbf_ref[0, 3 * w:4 * w, :] = kv_t[5 * w:6 * w].astype(BF16)


def _nsa_proj(x, tm, g, wq_bf, wg_bf, wkv_t_bf, qgain2, kgain, tabs):
    bsz, t, d = x.shape
    c, s1, s2, cos_t, sin_t = tabs
    hq = NSA_HEADS * NSA_HD
    w = KV_W
    tok = lambda b, i: (b, i, 0)
    chan = lambda b, i: (b, 0, i)
    return pl.pallas_call(
        _nsa_proj_kernel,
        grid=(bsz, t // tm),
        in_specs=[pl.BlockSpec((1, tm, d), tok), _const_spec((1, d)),
                  _const_spec((d, hq)), _const_spec((d, LANES)), _const_spec((6 * w, d)),
                  _const_spec((LANES, LANES)), _const_spec((1, LANES)),
                  pl.BlockSpec((tm, LANES), lambda b, i: (i, 0)),
                  pl.BlockSpec((tm, LANES), lambda b, i: (i, 0)),
                  pl.BlockSpec((tm, LANES), lambda b, i: (i, 0)),
                  _const_spec((2, NSA_HD, 1)),
                  pl.BlockSpec((ROT_HALF, tm), lambda b, i: (0, i)),
                  pl.BlockSpec((ROT_HALF, tm), lambda b, i: (0, i))],
        out_specs=[pl.BlockSpec((1, tm, hq), tok), pl.BlockSpec((1, tm, LANES), tok),
                   pl.BlockSpec((1, 4 * w, tm), chan), pl.BlockSpec((1, 2 * w, tm), chan),
                   pl.BlockSpec((1, 4 * w, tm), chan)],
        out_shape=[jax.ShapeDtypeStruct((bsz, t, hq), F32),
                   jax.ShapeDtypeStruct((bsz, t, LANES), F32),
                   jax.ShapeDtypeStruct((bsz, 4 * w, t), F32),
                   jax.ShapeDtypeStruct((bsz, 2 * w, t), F32),
                   jax.ShapeDtypeStruct((bsz, 4 * w, t), BF16)],
        compiler_params=_cparams(("parallel", "parallel")),
        name="nsa_proj",
    )(x, g, wq_bf, wg_bf, wkv_t_bf, _seg_mean_matrix(), qgain2, c, s1, s2, kgain, cos_t, sin_t)


def _half_sums(xt, pe_ref, w1_ref, nh):
    out = []
    tiles = KV_W // LANES
    for kind in range(2):
        a = jnp.zeros((nh, KV_W), F32)
        b = jnp.zeros((nh, KV_W), F32)
        for j in range(CMP_STRIDE):
            r = jnp.concatenate([xt[kind * tiles + c, pl.ds(j, nh, stride=CMP_STRIDE), :]
                                 for c in range(tiles)], axis=1)
            a = a + _dot((r + pe_ref[kind, j:j + 1, :]).astype(BF16), w1_ref[kind, j])
            b = b + _dot((r + pe_ref[kind, CMP_STRIDE + j:CMP_STRIDE + j + 1, :]).astype(BF16),
                         w1_ref[kind, CMP_STRIDE + j])
        out += [a, b]
    return out


def _finish_compress(a_k, b_k, a_v, b_v, w2_ref, kg_ref, bd, c, s1, s2):
    nh = a_k.shape[0]
    kc = _dot(_silu(a_k + pltpu.roll(b_k, nh - 1, 0)).astype(BF16), w2_ref[0])
    vc = _dot(_silu(a_v + pltpu.roll(b_v, nh - 1, 0)).astype(BF16), w2_ref[1])
    kc = jnp.concatenate([_headnorm_rope_tok(kc[:, h * LANES:(h + 1) * LANES], kg_ref[...], bd, c, s1, s2)
                          for h in range(KV_W // LANES)], axis=1)
    return kc, vc


def _p_compress_kernel(rows_ref, pe_ref, w1_ref, w2_ref, kg_ref, bd_ref, c_ref, s1_ref, s2_ref,
                       kc_ref, vc_ref, xt, *, nh):
    for c in range(2 * KV_W // LANES):
        xt[c] = rows_ref[0, c * LANES:(c + 1) * LANES, :].T
    a_k, b_k, a_v, b_v = _half_sums(xt, pe_ref, w1_ref, nh)
    kc, vc = _finish_compress(a_k, b_k, a_v, b_v, w2_ref, kg_ref, bd_ref[...], c_ref[...],
                              s1_ref[...], s2_ref[...])
    kc_ref[0] = kc.astype(BF16)
    vc_ref[0] = vc.astype(BF16)


def _p_compress(rows_t, pe4, w1bd, w2bd, kgain2, ctabs):
    bsz, _, t = rows_t.shape
    nh = t // CMP_STRIDE
    c, s1, s2 = ctabs
    w = KV_W
    return pl.pallas_call(
        functools.partial(_p_compress_kernel, nh=nh),
        grid=(bsz,),
        in_specs=[pl.BlockSpec((1, 2 * w, t), lambda b: (b, 0, 0)),
                  _const_spec((2, CMP_BLOCK, w)), _const_spec((2, CMP_BLOCK, w, w)),
                  _const_spec((2, w, w)), _const_spec((1, LANES)), _const_spec((LANES, LANES)),
                  _const_spec((nh, LANES)), _const_spec((nh, LANES)), _const_spec((nh, LANES))],
        out_specs=[pl.BlockSpec((1, nh, w), lambda b: (b, 0, 0)),
                   pl.BlockSpec((1, nh, w), lambda b: (b, 0, 0))],
        out_shape=[jax.ShapeDtypeStruct((bsz, nh, w), BF16), jax.ShapeDtypeStruct((bsz, nh, w), BF16)],
        scratch_shapes=[pltpu.VMEM((2 * w // LANES, t, LANES), F32)],
        compiler_params=_cparams(("parallel",)),
        name="p_compress",
    )(rows_t, pe4, w1bd, w2bd, kgain2, _seg_mean_matrix(), c, s1, s2)


def _stack_heads(q, g):
    return jnp.concatenate([q[:, (g * NSA_HPG + hh) * NSA_HD:(g * NSA_HPG + hh + 1) * NSA_HD]
                            for hh in range(NSA_HPG)], axis=0)


def _stack_gate(gates, g, j):
    return jnp.concatenate([gates[:, (g * NSA_HPG + hh) * 3 + j:(g * NSA_HPG + hh) * 3 + j + 1]
                            for hh in range(NSA_HPG)], axis=0)


def _tile_rows(m, n):
    return jnp.concatenate([m] * n, axis=0)


def _topk_mask(imp, k):
    r, s = imp.shape
    idx = lax.broadcasted_iota(jnp.int32, (r, s), 1)
    rank = jnp.zeros((r, s), F32)
    for s2 in range(s):
        col = imp[:, s2:s2 + 1]
        beats = jnp.where(col > imp, 1.0, jnp.where((col == imp) & (idx > s2), 1.0, 0.0))
        rank = rank + beats
    return rank < k


def _topk_bias_t(imp_t, k):
    s, r = imp_t.shape
    idx = lax.broadcasted_iota(jnp.int32, (s, r), 0)
    rank = jnp.zeros((s, r), F32)
    for s2 in range(s):
        row = imp_t[s2:s2 + 1, :]
        rank = rank + jnp.where(row > imp_t, 1.0, jnp.where((row == imp_t) & (idx > s2), 1.0, 0.0))
    return jnp.where((rank < k) & (imp_t > 0.5 * NEG), 0.0, NEG)


def _p_attn_kernel(q_ref, gate_ref, kc_ref, vc_ref, kvs_ref, w0, w1, w2, w3, w4, movt_ref, epad_ref,
                   eye4_ref, wo_ref, x_ref, y_ref, o_scr, qa_scr, oc_scr, os_scr, *, tq, t_all, kch):
    i = pl.program_id(1)
    q = q_ref[0]
    gates = gate_ref[0]
    nh = kc_ref.shape[1]
    n_slc = movt_ref.shape[0]
    pos_q = i * tq + lax.broadcasted_iota(jnp.int32, (tq, 1), 0)
    pos_q_row = i * tq + lax.broadcasted_iota(jnp.int32, (1, tq), 1)
    pos_c = lax.broadcasted_iota(jnp.int32, (tq, nh), 1) * CMP_STRIDE + (CMP_BLOCK - 1)
    ok_c = _tile_rows(pos_c <= pos_q, NSA_HPG)
    blk_t = lax.broadcasted_iota(jnp.int32, (n_slc, tq), 0)
    cur_t = pos_q_row // SLC_BLOCK
    w = KV_W

    for g in range(NSA_KV):
        qg = _stack_heads(q, g)
        gs = slice(g * NSA_HD, (g + 1) * NSA_HD)
        s = jnp.where(ok_c, _dot_nt(qg.astype(BF16), kc_ref[0, :, gs]), NEG)
        e = jnp.exp(s - jnp.max(s, axis=-1, keepdims=True))
        p = jnp.where(ok_c, e / jnp.sum(e, axis=-1, keepdims=True), 0.0)
        oc_scr[g] = _dot(p.astype(BF16), vc_ref[0, :, gs])
        psum = p[0:tq]
        for hh in range(1, NSA_HPG):
            psum = psum + p[hh * tq:(hh + 1) * tq]
        imp_t = _dot_nt(movt_ref[...], psum, precision=HI)
        imp_t = jnp.where((blk_t == cur_t) | (blk_t == 0), BIG, imp_t)
        imp_t = jnp.where(blk_t > cur_t, NEG, imp_t)
        selb_t = _topk_bias_t(imp_t, SLC_TOP).astype(BF16)
        selb4 = _dot_nt(eye4_ref[...], selb_t)
        pad = jnp.zeros((NSA_HPG * tq, LANES - NSA_HD - n_slc), F32)
        qa_scr[g] = jnp.concatenate([qg, selb4, pad], axis=1).astype(BF16)

    n_var = t_all // kch
    need = (i * tq) // kch + 1
    for nv in range(1, n_var + 1):
        @pl.when(need == nv)
        def _(nv=nv):
            klen = nv * kch
            head = klen - kch
            tail_pos = head + lax.broadcasted_iota(jnp.int32, (tq, kch), 1)
            cb4 = _tile_rows(jnp.where(tail_pos <= pos_q, 0.0, NEG), NSA_HPG)
            for g in range(NSA_KV):
                qa = qa_scr[g]
                ks = slice(g * NSA_HD, (g + 1) * NSA_HD)
                vs = slice(w + g * NSA_HD, w + (g + 1) * NSA_HD)
                k_tail = jnp.concatenate([kvs_ref[0, ks, head:klen], epad_ref[:, head:klen]], axis=0)
                s_t = _dot(qa, k_tail) + cb4
                m = jnp.max(s_t, axis=-1, keepdims=True)
                if head:
                    k_head = jnp.concatenate([kvs_ref[0, ks, 0:head], epad_ref[:, 0:head]], axis=0)
                    s_h = _dot(qa, k_head)
                    m = jnp.maximum(m, jnp.max(s_h, axis=-1, keepdims=True))
                e_t = jnp.exp(s_t - m)
                den = jnp.sum(e_t, axis=-1, keepdims=True)
                o = _dot_nt(e_t.astype(BF16), kvs_ref[0, vs, head:klen])
                if head:
                    e_h = jnp.exp(s_h - m)
                    den = den + jnp.sum(e_h, axis=-1, keepdims=True)
                    o = o + _dot_nt(e_h.astype(BF16), kvs_ref[0, vs, 0:head])
                os_scr[g] = o / den

    win_blocks = (w4, w3, w2, w1, w0)
    n_wb = len(win_blocks)
    wpos = (i - (n_wb - 1)) * tq + lax.broadcasted_iota(jnp.int32, (tq, n_wb * tq), 1)
    dpos = pos_q - wpos
    wb4 = _tile_rows(jnp.where((dpos >= 0) & (dpos < WINDOW) & (wpos >= 0), 0.0, NEG), NSA_HPG)
    for g in range(NSA_KV):
        qg = qa_scr[g, :, 0:NSA_HD]
        k_w = jnp.concatenate([r[0, g * NSA_HD:(g + 1) * NSA_HD, :] for r in win_blocks], axis=1)
        v_w = jnp.concatenate([r[0, w + g * NSA_HD:w + (g + 1) * NSA_HD, :] for r in win_blocks], axis=1)
        s = _dot(qg, k_w) + wb4
        e = jnp.exp(s - jnp.max(s, axis=-1, keepdims=True))
        o_w = _dot_nt(e.astype(BF16), v_w) / jnp.sum(e, axis=-1, keepdims=True)
        o = (_stack_gate(gates, g, 0) * oc_scr[g] + _stack_gate(gates, g, 1) * os_scr[g]
             + _stack_gate(gates, g, 2) * o_w)
        for hh in range(NSA_HPG):
            h = g * NSA_HPG + hh
            o_scr[:, h * NSA_HD:(h + 1) * NSA_HD] = o[hh * tq:(hh + 1) * tq].astype(BF16)

    y_ref[0] = x_ref[0] + _dot(o_scr[...], wo_ref[...])


def _p_attn(q, gates, kc, vc, kvbf, mov_t, epad, wo_bf, x, tq):
    bsz, t, hq = q.shape
    nh = kc.shape[1]
    w = KV_W
    n_slc = mov_t.shape[0]
    kch = min(512, t)
    rows = NSA_HPG * tq
    eye4 = jnp.tile(jnp.eye(tq, dtype=BF16), (NSA_HPG, 1))
    tok = lambda b, i: (b, i, 0)
    win_specs = [pl.BlockSpec((1, 2 * w, tq), functools.partial(
        lambda b, i, k: (b, 1, jnp.maximum(i - k, 0)), k=k)) for k in range(5)]
    return pl.pallas_call(
        functools.partial(_p_attn_kernel, tq=tq, t_all=t, kch=kch),
        grid=(bsz, t // tq),
        in_specs=[pl.BlockSpec((1, tq, hq), tok), pl.BlockSpec((1, tq, LANES), tok),
                  pl.BlockSpec((1, nh, w), lambda b, i: (b, 0, 0)),
                  pl.BlockSpec((1, nh, w), lambda b, i: (b, 0, 0)),
                  pl.BlockSpec((1, 2 * w, t), lambda b, i: (b, 0, 0))] + win_specs + [
                  _const_spec((n_slc, nh)), _const_spec((LANES - NSA_HD, t)),
                  _const_spec((rows, tq)), _const_spec((hq, D_MODEL)),
                  pl.BlockSpec((1, tq, D_MODEL), tok)],
        out_specs=pl.BlockSpec((1, tq, D_MODEL), tok),
        out_shape=jax.ShapeDtypeStruct((bsz, t, D_MODEL), F32),
        scratch_shapes=[pltpu.VMEM((tq, hq), BF16), pltpu.VMEM((NSA_KV, rows, LANES), BF16),
                        pltpu.VMEM((NSA_KV, rows, NSA_HD), F32), pltpu.VMEM((NSA_KV, rows, NSA_HD), F32)],
        compiler_params=_cparams(("parallel", "parallel")),
        name="p_attn",
    )(q, gates, kc, vc, kvbf, kvbf, kvbf, kvbf, kvbf, kvbf, mov_t, epad, eye4, wo_bf, x)


MAX_PAGES_COMPRESS = 32
MAX_PAGES_SLC = 16


def _s_compress_kernel(pt_ref, *refs, nh, page, pps):
    pages = refs[:pps]
    pe_ref, w1_ref, a_ref, b_ref, xt = refs[pps:]
    for k, pg in enumerate(pages):
        for c in range(2 * KV_W // LANES):
            xt[c, k * page:(k + 1) * page, :] = pg[0, c * LANES:(c + 1) * LANES, :].T
    a_k, b_k, a_v, b_v = _half_sums(xt, pe_ref, w1_ref, nh)
    a_ref[0] = jnp.concatenate([a_k, a_v], axis=1)
    b_ref[0] = jnp.concatenate([b_k, b_v], axis=1)


def _s_compress(cache_t, page_table, pe4, w1bd):
    bsz, n_pages = page_table.shape
    page = cache_t.shape[2]
    w = KV_W
    pps = min(MAX_PAGES_COMPRESS, n_pages)
    steps = n_pages // pps
    nh = pps * page // CMP_STRIDE
    page_specs = [pl.BlockSpec((1, 2 * w, page), functools.partial(
        lambda b, s, pt, k: (pt[b, s * pps + k], 0, 0), k=k)) for k in range(pps)]
    grid_spec = pltpu.PrefetchScalarGridSpec(
        num_scalar_prefetch=1,
        grid=(bsz, steps),
        in_specs=page_specs + [pl.BlockSpec((2, CMP_BLOCK, w), lambda b, s, pt: (0, 0, 0)),
                               pl.BlockSpec((2, CMP_BLOCK, w, w), lambda b, s, pt: (0, 0, 0, 0))],
        out_specs=[pl.BlockSpec((1, nh, 2 * w), lambda b, s, pt: (b, s, 0)),
                   pl.BlockSpec((1, nh, 2 * w), lambda b, s, pt: (b, s, 0))],
        scratch_shapes=[pltpu.VMEM((2 * w // LANES, pps * page, LANES), F32)])
    out = jax.ShapeDtypeStruct((bsz, steps * nh, 2 * w), F32)
    return pl.pallas_call(
        functools.partial(_s_compress_kernel, nh=nh, page=page, pps=pps),
        grid_spec=grid_spec,
        out_shape=[out, out],
        compiler_params=_cparams(("parallel", "arbitrary")),
        name="s_compress",
    )(page_table, *([cache_t] * pps), pe4, w1bd)


def _s_cmp_kernel(a_ref, b_ref, q_ref, w2_ref, kg_ref, bd_ref, c_ref, s1_ref, s2_ref, mov_ref,
                  oc_ref, sel_ref, *, ts, past):
    ab = a_ref[0]
    bb = b_ref[0]
    w = KV_W
    kc, vc = _finish_compress(ab[:, 0:w], bb[:, 0:w], ab[:, w:2 * w], bb[:, w:2 * w], w2_ref, kg_ref,
                              bd_ref[...], c_ref[...], s1_ref[...], s2_ref[...])
    kc = kc.astype(BF16)
    vc = vc.astype(BF16)
    nh = kc.shape[0]
    q = q_ref[0]
    pos_q = past + lax.broadcasted_iota(jnp.int32, (ts, 1), 0)
    pos_c = lax.broadcasted_iota(jnp.int32, (ts, nh), 1) * CMP_STRIDE + (CMP_BLOCK - 1)
    ok_c = _tile_rows(pos_c <= pos_q, NSA_HPG)
    n_blk = mov_ref.shape[1]
    blk = lax.broadcasted_iota(jnp.int32, (ts, n_blk), 1)
    for g in range(NSA_KV):
        qg = _stack_heads(q, g).astype(BF16)
        gs = slice(g * NSA_HD, (g + 1) * NSA_HD)
        s = jnp.where(ok_c, _dot_nt(qg, kc[:, gs]), NEG)
        e = jnp.exp(s - jnp.max(s, axis=-1, keepdims=True))
        p = jnp.where(ok_c, e / jnp.sum(e, axis=-1, keepdims=True), 0.0)
        o_c = _dot(p.astype(BF16), vc[:, gs])
        for hh in range(NSA_HPG):
            h = g * NSA_HPG + hh
            oc_ref[0, :, h * NSA_HD:(h + 1) * NSA_HD] = o_c[hh * ts:(hh + 1) * ts]
        psum = p[0:ts]
        for hh in range(1, NSA_HPG):
            psum = psum + p[hh * ts:(hh + 1) * ts]
        imp = jnp.dot(psum, mov_ref[...], precision=HI, preferred_element_type=F32)
        imp = jnp.where(blk == 0, BIG, imp)
        sel_ref[0, g * ts:(g + 1) * ts, :] = jnp.where(_topk_mask(imp, SLC_TOP - 1), 1.0, 0.0)


def _s_cmp(a, b, q, w2bd, kgain2, ctabs, mov, past):
    bsz, nh, _ = a.shape
    ts = q.shape[1]
    hq = q.shape[2]
    c, s1, s2 = ctabs
    w = KV_W
    n_blk = mov.shape[1]
    per_b = lambda i: (i, 0, 0)
    return pl.pallas_call(
        functools.partial(_s_cmp_kernel, ts=ts, past=past),
        grid=(bsz,),
        in_specs=[pl.BlockSpec((1, nh, 2 * w), per_b), pl.BlockSpec((1, nh, 2 * w), per_b),
                  pl.BlockSpec((1, ts, hq), per_b),
                  _const_spec((2, w, w)), _const_spec((1, LANES)), _const_spec((LANES, LANES)),
                  _const_spec((nh, LANES)), _const_spec((nh, LANES)), _const_spec((nh, LANES)),
                  _const_spec((nh, n_blk))],
        out_specs=[pl.BlockSpec((1, ts, hq), per_b), pl.BlockSpec((1, NSA_KV * ts, n_blk), per_b)],
        out_shape=[jax.ShapeDtypeStruct((bsz, ts, hq), F32),
                   jax.ShapeDtypeStruct((bsz, NSA_KV * ts, n_blk), F32)],
        compiler_params=_cparams(("parallel",)),
        name="s_cmp",
    )(a, b, q, w2bd, kgain2, _seg_mean_matrix(), c, s1, s2, mov)


def _s_slc_kernel(pt_ref, *refs, ts, pps):
    pages = refs[:pps]
    q_ref, sel_ref, new_ref, exp_ref, o_ref, m_scr, l_scr, acc = refs[pps:]
    s_id = pl.program_id(1)
    w = KV_W
    rows = NSA_HPG * ts

    @pl.when(s_id == 0)
    def _():
        m_scr[...] = jnp.full(m_scr.shape, NEG, F32)
        l_scr[...] = jnp.zeros(l_scr.shape, F32)
        acc[...] = jnp.zeros(acc.shape, F32)

    q = q_ref[0]
    sel = sel_ref[0, 0]
    for g in range(NSA_KV):
        qg = _stack_heads(q, g).astype(BF16)
        k_t = jnp.concatenate([pg[0, g * NSA_HD:(g + 1) * NSA_HD, :] for pg in pages], axis=1).astype(BF16)
        v_t = jnp.concatenate([pg[0, w + g * NSA_HD:w + (g + 1) * NSA_HD, :] for pg in pages],
                              axis=1).astype(BF16)
        selk = _dot(sel[g * ts:(g + 1) * ts].astype(BF16), exp_ref[...])
        ok = _tile_rows(selk > 0.5, NSA_HPG)
        s = jnp.where(ok, _dot(qg, k_t), NEG)
        m_old = m_scr[g]
        m_new = jnp.maximum(m_old, jnp.max(s, axis=-1, keepdims=True))
        alpha = jnp.exp(m_old - m_new)
        e = jnp.where(ok, jnp.exp(s - m_new), 0.0)
        l_scr[g] = alpha * l_scr[g] + jnp.sum(e, axis=-1, keepdims=True)
        acc[g] = alpha * acc[g] + _dot_nt(e.astype(BF16), v_t)
        m_scr[g] = m_new

    @pl.when(s_id == pl.num_programs(1) - 1)
    def _():
        t_row = lax.broadcasted_iota(jnp.int32, (ts, ts), 0)
        t_col = lax.broadcasted_iota(jnp.int32, (ts, ts), 1)
        ok_n = _tile_rows(t_col <= t_row, NSA_HPG)
        for g in range(NSA_KV):
            qg = _stack_heads(q, g).astype(BF16)
            k_n = new_ref[0, g * NSA_HD:(g + 1) * NSA_HD, :].astype(BF16)
            v_n = new_ref[0, w + g * NSA_HD:w + (g + 1) * NSA_HD, :].astype(BF16)
            s = jnp.where(ok_n, _dot(qg, k_n), NEG)
            m_old = m_scr[g]
            m_new = jnp.maximum(m_old, jnp.max(s, axis=-1, keepdims=True))
            alpha = jnp.exp(m_old - m_new)
            e = jnp.where(ok_n, jnp.exp(s - m_new), 0.0)
            l_fin = alpha * l_scr[g] + jnp.sum(e, axis=-1, keepdims=True)
            o = (alpha * acc[g] + _dot_nt(e.astype(BF16), v_n)) / l_fin
            for hh in range(NSA_HPG):
                h = g * NSA_HPG + hh
                o_ref[0, :, h * NSA_HD:(h + 1) * NSA_HD] = o[hh * ts:(hh + 1) * ts]


def _s_slc(cache_t, page_table, q, sel_steps, new_t, expand, pps):
    bsz, n_pages = page_table.shape
    page = cache_t.shape[2]
    ts, hq = q.shape[1], q.shape[2]
    w = KV_W
    steps = n_pages // pps
    bps = sel_steps.shape[3]
    page_specs = [pl.BlockSpec((1, 2 * w, page), functools.partial(
        lambda b, s, pt, k: (pt[b, s * pps + k], 1, 0), k=k)) for k in range(pps)]
    rows = NSA_HPG * ts
    grid_spec = pltpu.PrefetchScalarGridSpec(
        num_scalar_prefetch=1,
        grid=(bsz, steps),
        in_specs=page_specs + [
            pl.BlockSpec((1, ts, hq), lambda b, s, pt: (b, 0, 0)),
            pl.BlockSpec((1, 1, NSA_KV * ts, bps), lambda b, s, pt: (b, s, 0, 0)),
            pl.BlockSpec((1, 2 * w, ts), lambda b, s, pt: (b, 0, 0)),
            pl.BlockSpec((bps, pps * page), lambda b, s, pt: (0, 0))],
        out_specs=pl.BlockSpec((1, ts, hq), lambda b, s, pt: (b, 0, 0)),
        scratch_shapes=[pltpu.VMEM((NSA_KV, rows, 1), F32), pltpu.VMEM((NSA_KV, rows, 1), F32),
                        pltpu.VMEM((NSA_KV, rows, NSA_HD), F32)])
    return pl.pallas_call(
        functools.partial(_s_slc_kernel, ts=ts, pps=pps),
        grid_spec=grid_spec,
        out_shape=jax.ShapeDtypeStruct((bsz, ts, hq), F32),
        compiler_params=_cparams(("parallel", "arbitrary")),
        name="s_slc",
    )(page_table, *([cache_t] * pps), q, sel_steps, new_t, expand)


def _s_win_kernel(q_ref, gate_ref, oc_ref, os_ref, cache_ref, new_ref, o_ref, wout_ref, *, ts):
    q = q_ref[0]
    gates = gate_ref[0]
    cache = cache_ref[0]
    new = new_ref[0]
    wb = cache.shape[1]
    w = KV_W
    t_row = lax.broadcasted_iota(jnp.int32, (ts, wb), 0)
    idx = lax.broadcasted_iota(jnp.int32, (ts, wb), 1)
    dpos = t_row + wb - idx
    ok_old = _tile_rows((dpos >= 0) & (dpos < WINDOW), NSA_HPG)
    n_row = lax.broadcasted_iota(jnp.int32, (ts, ts), 0)
    n_col = lax.broadcasted_iota(jnp.int32, (ts, ts), 1)
    ok_new = _tile_rows((n_col <= n_row) & (n_row - n_col < WINDOW), NSA_HPG)
    for g in range(NSA_KV):
        qg = _stack_heads(q, g).astype(BF16)
        gk = slice(g * NSA_HD, (g + 1) * NSA_HD)
        gv = slice(w + g * NSA_HD, w + (g + 1) * NSA_HD)
        s_old = jnp.where(ok_old, _dot(qg, cache[gk].astype(BF16)), NEG)
        s_new = jnp.where(ok_new, _dot(qg, new[gk].astype(BF16)), NEG)
        m = jnp.maximum(jnp.max(s_old, axis=-1, keepdims=True), jnp.max(s_new, axis=-1, keepdims=True))
        e_old = jnp.exp(s_old - m)
        e_new = jnp.exp(s_new - m)
        den = jnp.sum(e_old, axis=-1, keepdims=True) + jnp.sum(e_new, axis=-1, keepdims=True)
        o_w = (_dot_nt(e_old.astype(BF16), cache[gv].astype(BF16))
               + _dot_nt(e_new.astype(BF16), new[gv].astype(BF16))) / den
        for hh in range(NSA_HPG):
            h = g * NSA_HPG + hh
            hs = slice(h * NSA_HD, (h + 1) * NSA_HD)
            o_ref[0, :, hs] = (gates[:, 3 * h:3 * h + 1] * oc_ref[0, :, hs]
                               + gates[:, 3 * h + 1:3 * h + 2] * os_ref[0, :, hs]
                               + gates[:, 3 * h + 2:3 * h + 3] * o_w[hh * ts:(hh + 1) * ts])
    wout_ref[0] = pltpu.roll(cache, wb - ts, 1)
    wout_ref[0, :, wb - ts:wb] = new


def _s_win(q, gates, o_c, o_s, cache_win_t, new_win_t):
    bsz, ts, hq = q.shape
    wb = cache_win_t.shape[2]
    w = KV_W
    per_b = lambda b: (b, 0, 0)
    return pl.pallas_call(
        functools.partial(_s_win_kernel, ts=ts),
        grid=(bsz,),
        in_specs=[pl.BlockSpec((1, ts, hq), per_b), pl.BlockSpec((1, ts, LANES), per_b),
                  pl.BlockSpec((1, ts, hq), per_b), pl.BlockSpec((1, ts, hq), per_b),
                  pl.BlockSpec((1, 2 * w, wb), per_b), pl.BlockSpec((1, 2 * w, ts), per_b)],
        out_specs=[pl.BlockSpec((1, ts, hq), per_b), pl.BlockSpec((1, 2 * w, wb), per_b)],
        out_shape=[jax.ShapeDtypeStruct((bsz, ts, hq), F32),
                   jax.ShapeDtypeStruct((bsz, 2 * w, wb), F32)],
        compiler_params=_cparams(("parallel",)),
        name="s_win",
    )(q, gates, o_c, o_s, cache_win_t, new_win_t)


def _block_overlap(n_cmp_rows, n_cmp, n_slc):
    cs = jnp.arange(n_cmp_rows)[:, None] * CMP_STRIDE
    ss = jnp.arange(n_slc)[None, :] * SLC_BLOCK
    ov = jnp.minimum(cs + CMP_BLOCK, ss + SLC_BLOCK) - jnp.maximum(cs, ss)
    ov = jnp.clip(ov, 0, None).astype(F32) / CMP_BLOCK
    return jnp.where(jnp.arange(n_cmp_rows)[:, None] < n_cmp, ov, 0.0)


def _expand_matrix(n_blk, n_keys):
    return (jnp.arange(n_keys)[None, :] // SLC_BLOCK == jnp.arange(n_blk)[:, None]).astype(BF16)


def _pad_rows(a, rows):
    return jnp.pad(a, ((0, rows - a.shape[0]), (0, 0)))


def _pad_lanes(a, lanes=LANES):
    return jnp.pad(a, ((0, 0), (0, lanes - a.shape[1])))


def _blockdiag4(m):
    eye = jnp.eye(NSA_KV, dtype=m.dtype)
    out = jnp.einsum('gh,...de->...gdhe', eye, m)
    return out.reshape(m.shape[:-2] + (KV_W, KV_W))


def _layer_tail(x, layer, kv4, tq_mem, tm_ffn, p):
    bsz, t, d = x.shape
    x = _memattn(x, tq_mem, p['norm_mem'][layer][None], p['mem_w_q'][layer].astype(BF16),
                 p['mem_q_norm'][layer][None], kv4, layer, p['mem_w_o'][layer].astype(BF16))
    x2 = _ffn(x.reshape(bsz * t, d), tm_ffn, p['norm_ffn'][layer][None],
              p['ffn_w_gate'][layer].astype(BF16), p['ffn_w_up'][layer].astype(BF16),
              p['ffn_w_down'][layer].astype(BF16))
    return x2.reshape(bsz, t, d)


def _run_mixer0(x, tq, sconv_st, ssdc_st, ssd_st, p):
    w_in = _pad_lanes(p['ab_w_in'][0], IN0_PAD).astype(BF16)
    front = lambda s: jnp.pad(s, ((0, 0), (CARRY - s.shape[1], 0), (0, 0)))
    y, sa, sb, hn = _mixer0(
        x, tq, p['norm_mix'][0][None], w_in,
        _pad_rows(p['ab_sconv_w'][0], SUBLANES), _pad_rows(p['ab_ssd_conv_w'][0], SUBLANES),
        p['ab_ssd_conv_b'], _pad_lanes(p['ab_dt_bias']), _pad_lanes(p['ab_a_log']),
        _pad_lanes(p['ab_d']), p['ab_ssd_norm'], p['ab_w_out'][0].astype(BF16),
        front(sconv_st), front(ssdc_st), ssd_st)
    return y, sa[:, CARRY - (SCONV_W - 1):], sb[:, CARRY - (SSD_CONV_W - 1):], hn


def _nsa_weights(p):
    w_in = p['nsa_w_in'][0]
    hq = NSA_HEADS * NSA_HD
    wq = w_in[:, :hq].astype(BF16)
    wkv_t = w_in[:, hq:hq + 6 * KV_W].T.astype(BF16)
    wg = _pad_lanes(w_in[:, hq + 6 * KV_W:]).astype(BF16)
    qgain2 = jnp.tile(p['nsa_q_norm'][0], 2)[None]
    kn = p['nsa_k_norm'][0]
    kgain = jnp.stack([kn[1], kn[2]])[:, :, None]
    kcgain2 = jnp.tile(kn[0], 2)[None]
    pe4 = jnp.tile(p['nsa_cmp_pe'][0], (1, 1, NSA_KV))
    w1bd = _blockdiag4(p['nsa_cmp_w1'][0]).astype(BF16)
    w2bd = _blockdiag4(p['nsa_cmp_w2'][0]).astype(BF16)
    return wq, wg, wkv_t, qgain2, kgain, kcgain2, pe4, w1bd, w2bd


def kernel(x_prompt, x_sample, mem_prompt, state_sconv, state_ssd_conv, state_ssd, cache_nsa_kv,
           cache_nsa_win, cache_mem_kv, page_table, norm_mix, norm_mem, norm_memsrc, norm_ffn,
           ab_w_in, ab_sconv_w, ab_ssd_conv_w, ab_ssd_conv_b, ab_dt_bias, ab_a_log, ab_d, ab_ssd_norm,
           ab_w_out, nsa_w_in, nsa_q_norm, nsa_k_norm, nsa_cmp_pe, nsa_cmp_w1, nsa_cmp_w2, nsa_w_out,
           mem_w_q, mem_q_norm, mem_w_kv, mem_k_norm, mem_w_o, ffn_w_gate, ffn_w_up, ffn_w_down):
    p = dict(norm_mix=norm_mix, norm_mem=norm_mem, norm_ffn=norm_ffn, ab_w_in=ab_w_in,
             ab_sconv_w=ab_sconv_w, ab_ssd_conv_w=ab_ssd_conv_w, ab_ssd_conv_b=ab_ssd_conv_b,
             ab_dt_bias=ab_dt_bias, ab_a_log=ab_a_log, ab_d=ab_d, ab_ssd_norm=ab_ssd_norm,
             ab_w_out=ab_w_out, nsa_w_in=nsa_w_in, nsa_q_norm=nsa_q_norm, nsa_k_norm=nsa_k_norm,
             nsa_cmp_pe=nsa_cmp_pe, nsa_cmp_w1=nsa_cmp_w1, nsa_cmp_w2=nsa_cmp_w2, nsa_w_out=nsa_w_out,
             mem_w_q=mem_w_q, mem_q_norm=mem_q_norm, mem_w_o=mem_w_o,
             ffn_w_gate=ffn_w_gate, ffn_w_up=ffn_w_up, ffn_w_down=ffn_w_down)
    bp, t, d = x_prompt.shape
    bs, ts, _ = x_sample.shape
    n_mem = mem_prompt.shape[1]
    depth = norm_mix.shape[0]
    assert depth == 2 and ab_w_in.shape[0] == 1 and nsa_w_in.shape[0] == 1
    w = KV_W
    hq = NSA_HEADS * NSA_HD
    wq, wg, wkv_t, qgain2, kgain, kcgain2, pe4, w1bd, w2bd = _nsa_weights(p)
    wo_nsa = nsa_w_out[0].astype(BF16)

    mem2d = mem_prompt.reshape(bp * n_mem, d)
    p_mem = _memkv(mem2d, norm_memsrc[:, None], mem_w_kv.astype(BF16), mem_k_norm[:, None])
    p_mem = p_mem.reshape(depth, bp, n_mem * MEM_SLOTS, MEM_HD)
    p_mem_kv = p_mem.reshape(depth, bp, n_mem, 2, MEM_HEADS, MEM_HD)
    s_mem = cache_mem_kv.reshape(depth, bs, n_mem * MEM_SLOTS, MEM_HD)

    tq0 = min(256, t)
    tq_mem = min(512, t)
    tm_ffn = min(1024, bp * t)
    zeros = lambda *s: jnp.zeros(s, F32)
    x, p_sconv, p_ssdc, p_ssd = _run_mixer0(
        x_prompt, tq0, zeros(bp, SCONV_W - 1, d), zeros(bp, SSD_CONV_W - 1, SSD_CONV_DIM),
        zeros(bp, SSD_HEADS, SSD_HD, SSD_N), p)
    x = _layer_tail(x, 0, p_mem, tq_mem, tm_ffn, p)

    pos_p = jnp.arange(t)
    tabs_p = _rope_tables(pos_p)
    tm_proj = min(512, t)
    q, gates, rows_t, win_t, kvbf = _nsa_proj(x, tm_proj, norm_mix[1][None], wq, wg, wkv_t, qgain2,
                                              kgain, tabs_p)
    nh = t // CMP_STRIDE
    n_cmp = nh - 1
    pos_c = jnp.arange(nh) * CMP_STRIDE + CMP_BLOCK - 1
    kc, vc = _p_compress(rows_t, pe4, w1bd, w2bd, kcgain2, _rope_tables(pos_c)[:3])
    n_slc = -(-t // SLC_BLOCK)
    tq_a = min(128, t)
    assert n_slc <= LANES - NSA_HD
    x = _p_attn(q, gates, kc, vc, kvbf, _block_overlap(nh, n_cmp, n_slc).T,
                _pad_rows(_expand_matrix(n_slc, t), LANES - NSA_HD), wo_nsa, x, tq_a)
    y_prompt = _layer_tail(x, 1, p_mem, tq_mem, tm_ffn, p)
    p_rows = rows_t.reshape(bp, 1, 4, NSA_KV, NSA_HD, t).transpose(0, 5, 1, 2, 3, 4)
    wl = min(WINDOW, t)
    p_win = win_t[:, :, t - wl:].reshape(1, bp, 2, NSA_KV, NSA_HD, wl).transpose(0, 1, 5, 2, 3, 4)

    n_pool, page = cache_nsa_kv.shape[0], cache_nsa_kv.shape[1]
    n_pages = page_table.shape[1]
    past = n_pages * page
    pps_slc = min(MAX_PAGES_SLC, n_pages)
    assert ts <= CMP_STRIDE and past % SLC_BLOCK == 0
    assert n_pages % pps_slc == 0 and n_pages % min(MAX_PAGES_COMPRESS, n_pages) == 0
    xs, s_sconv, s_ssdc, s_ssd = _run_mixer0(x_sample, ts, state_sconv[0], state_ssd_conv[0],
                                             state_ssd[0], p)
    xs = _layer_tail(xs, 0, s_mem, ts, bs * ts, p)

    pos_s = jnp.tile(past + jnp.arange(ts), bs)
    qs, gates_s, rows_s, win_s, _ = _nsa_proj(xs.reshape(1, bs * ts, d), bs * ts, norm_mix[1][None],
                                              wq, wg, wkv_t, qgain2, kgain, _rope_tables(pos_s))
    qs = qs.reshape(bs, ts, hq)
    gates_s = gates_s.reshape(bs, ts, LANES)
    rows_s = rows_s[0].reshape(4 * w, bs, ts).transpose(1, 0, 2)
    win_s = win_s[0].reshape(2 * w, bs, ts).transpose(1, 0, 2)
    cache_t = cache_nsa_kv.transpose(0, 2, 3, 4, 5, 1).reshape(n_pool, 4 * w, page)
    a_sum, b_sum = _s_compress(cache_t, page_table, pe4, w1bd)
    nh_s = past // CMP_STRIDE
    n_cmp_s = (past + ts) // CMP_STRIDE - 1
    pos_cs = jnp.arange(nh_s) * CMP_STRIDE + CMP_BLOCK - 1
    n_blk = past // SLC_BLOCK
    o_c, sel = _s_cmp(a_sum, b_sum, qs, w2bd, kcgain2, _rope_tables(pos_cs)[:3],
                      _block_overlap(nh_s, n_cmp_s, n_blk), past)
    steps = n_pages // pps_slc
    bps = n_blk // steps
    sel_steps = sel.reshape(bs, NSA_KV * ts, steps, bps).transpose(0, 2, 1, 3)
    o_s = _s_slc(cache_t, page_table, qs, sel_steps, rows_s[:, 2 * w:],
                 _expand_matrix(bps, pps_slc * page), pps_slc)
    wb = cache_nsa_win.shape[2]
    cache_win_t = cache_nsa_win[0].transpose(0, 2, 3, 4, 1).reshape(bs, 2 * w, wb)
    o_att, new_win_t = _s_win(qs, gates_s, o_c, o_s, cache_win_t, win_s)
    xs = _proj_res(o_att.reshape(bs * ts, hq), wo_nsa, xs.reshape(bs * ts, d)).reshape(bs, ts, d)
    y_sample = _layer_tail(xs, 1, s_mem, ts, bs * ts, p)
    s_rows = rows_s.transpose(0, 2, 1).reshape(bs, ts, 1, 4, NSA_KV, NSA_HD)
    s_win = new_win_t.reshape(1, bs, 2, NSA_KV, NSA_HD, wb).transpose(0, 1, 5, 2, 3, 4)

    return (y_prompt, y_sample, p_sconv[None], p_ssdc[None], p_ssd[None], p_rows, p_win, p_mem_kv,
            s_sconv[None], s_ssdc[None], s_ssd[None], s_rows, s_win)
```

```python
import functools
import math

import jax
import jax.numpy as jnp
from jax import lax
from jax.experimental import pallas as pl
from jax.experimental.pallas import tpu as pltpu

F32 = jnp.float32
BF16 = jnp.bfloat16
HI = lax.Precision.HIGHEST

EPS = 1e-6
ROPE_THETA = 500000.0
NEG = -1e30
BIG = 1e9

LANES = 128
SUBLANES = 8
VMEM_LIMIT = 56 * 1024 * 1024

D_MODEL = 1024
SCONV_W = 3
SSD_HEADS = 16
SSD_HD = 64
SSD_DIM = SSD_HEADS * SSD_HD
SSD_GROUPS = 4
SSD_HPG = SSD_HEADS // SSD_GROUPS
SSD_N = 128
SSD_CONV_W = 4
SSD_CONV_DIM = SSD_DIM + 2 * SSD_GROUPS * SSD_N
IN0_DIM = 3 * D_MODEL + SSD_DIM + SSD_CONV_DIM + SSD_HEADS
IN0_PAD = 3 * D_MODEL + SSD_DIM + SSD_CONV_DIM + LANES
NSA_HEADS = 16
NSA_KV = 4
NSA_HD = 64
NSA_HPG = NSA_HEADS // NSA_KV
ROT_HALF = NSA_HD // 8
CMP_BLOCK = 32
CMP_STRIDE = 16
SLC_BLOCK = 64
SLC_TOP = 16
WINDOW = 512
KV_W = NSA_KV * NSA_HD
MEM_HEADS = 4
MEM_HD = 128
CARRY = SUBLANES


def _cparams(sem):
    return pltpu.CompilerParams(dimension_semantics=sem, vmem_limit_bytes=VMEM_LIMIT)


def _const_spec(shape):
    n = len(shape)
    return pl.BlockSpec(shape, lambda *_: (0,) * n)


def _rms(x, g):
    ms = jnp.mean(x * x, axis=-1, keepdims=True)
    return x * lax.rsqrt(ms + EPS) * g


def _dot(a, b):
    return jnp.dot(a, b, preferred_element_type=F32)


def _dot_nt(a, b, precision=None):
    return lax.dot_general(a, b, (((1,), (1,)), ((), ())), precision=precision,
                           preferred_element_type=F32)


def _dot_tn(a, b):
    return lax.dot_general(a, b, (((0,), (0,)), ((), ())), preferred_element_type=F32)


def _silu(x):
    return x * jax.nn.sigmoid(x)


def _softplus(x):
    return jnp.maximum(x, 0.0) + jnp.log1p(jnp.exp(-jnp.abs(x)))


def _seg_mean_matrix():
    r = lax.broadcasted_iota(jnp.int32, (LANES, LANES), 0) // NSA_HD
    c = lax.broadcasted_iota(jnp.int32, (LANES, LANES), 1) // NSA_HD
    return jnp.where(r == c, 1.0 / NSA_HD, 0.0).astype(F32)


def _headnorm_rope_tok(x, gain, bd, c, s1, s2):
    ms = jnp.dot(x * x, bd, precision=HI, preferred_element_type=F32)
    xn = x * lax.rsqrt(ms + EPS) * gain
    return xn * c + pltpu.roll(xn, LANES - ROT_HALF, 1) * s1 + pltpu.roll(xn, ROT_HALF, 1) * s2


def _headnorm_rope_ch(x, gain_col, cos_t, sin_t):
    ms = jnp.mean(x * x, axis=0, keepdims=True)
    xn = x * lax.rsqrt(ms + EPS) * gain_col
    x1 = xn[0:ROT_HALF]
    x2 = xn[ROT_HALF:2 * ROT_HALF]
    return x1 * cos_t - x2 * sin_t, x2 * cos_t + x1 * sin_t, xn[2 * ROT_HALF:]


def _rope_tables(pos):
    inv = ROPE_THETA ** (-jnp.arange(ROT_HALF, dtype=F32) / ROT_HALF)
    ang = pos.astype(F32)[:, None] * inv[None, :]
    cos, sin = jnp.cos(ang), jnp.sin(ang)
    n = pos.shape[0]
    ones = jnp.ones((n, NSA_HD - 2 * ROT_HALF), F32)
    zeros8 = jnp.zeros((n, ROT_HALF), F32)
    zeros = jnp.zeros((n, NSA_HD - 2 * ROT_HALF), F32)
    c = jnp.concatenate([cos, cos, ones], axis=1)
    s1 = jnp.concatenate([-sin, zeros8, zeros], axis=1)
    s2 = jnp.concatenate([zeros8, sin, zeros], axis=1)
    tile2 = lambda t: jnp.concatenate([t, t], axis=1)
    return tile2(c), tile2(s1), tile2(s2), cos.T, sin.T


MEM_SLOTS = 2 * MEM_HEADS


def _memkv_kernel(x_ref, g_ref, w_ref, kn_ref, o_ref, *, tm):
    hn = _rms(x_ref[...], g_ref[0]).astype(BF16)
    kv = _dot(hn, w_ref[0])
    for c in range(MEM_SLOTS):
        blk = kv[:, c * MEM_HD:(c + 1) * MEM_HD]
        if c < MEM_HEADS:
            blk = _rms(blk, kn_ref[0])
        o_ref[0, pl.ds(c, tm, stride=MEM_SLOTS), :] = blk


def _memkv(mem2d, g, w_bf, kn):
    m = mem2d.shape[0]
    depth = w_bf.shape[0]
    tm = min(512, m)
    n = w_bf.shape[2]
    return pl.pallas_call(
        functools.partial(_memkv_kernel, tm=tm),
        grid=(depth, m // tm),
        in_specs=[pl.BlockSpec((tm, D_MODEL), lambda l, i: (i, 0)),
                  pl.BlockSpec((1, 1, D_MODEL), lambda l, i: (l, 0, 0)),
                  pl.BlockSpec((1, D_MODEL, n), lambda l, i: (l, 0, 0)),
                  pl.BlockSpec((1, 1, MEM_HD), lambda l, i: (l, 0, 0))],
        out_specs=pl.BlockSpec((1, tm * MEM_SLOTS, MEM_HD), lambda l, i: (l, i, 0)),
        out_shape=jax.ShapeDtypeStruct((depth, m * MEM_SLOTS, MEM_HD), F32),
        compiler_params=_cparams(("parallel", "parallel")),
        name="memkv",
    )(mem2d, g, w_bf, kn)


def _mixer0_kernel(x_ref, g_ref, win_ref, scw_ref, cw_ref, cb_ref, dtb_ref, alog_ref, dsk_ref,
                   ssdn_ref, wout_ref, sa0_ref, sb0_ref, h0_ref,
                   y_ref, sa_ref, sb_ref, hout_ref,
                   proj, ca, cbuf, hst, ycat, *, tq, n_col_chunks):
    i = pl.program_id(1)
    last = pl.num_programs(1) - 1

    @pl.when(i == 0)
    def _():
        ca[0:CARRY, :] = sa0_ref[0]
        cbuf[0:CARRY, :] = sb0_ref[0]
        hst[...] = h0_ref[0]

    x = x_ref[0]
    hn = _rms(x, g_ref[...]).astype(BF16)
    cw_cols = IN0_PAD // n_col_chunks
    for c in range(n_col_chunks):
        sl = slice(c * cw_cols, (c + 1) * cw_cols)
        proj[:, sl] = _dot(hn, win_ref[:, sl])

    d = D_MODEL
    ca[CARRY:CARRY + tq, :] = proj[:, 2 * d:3 * d] * proj[:, 0:d]
    conv_u = ca[pl.ds(CARRY - 2, tq), :] * scw_ref[0:1, :]
    for k in range(1, SCONV_W):
        conv_u = conv_u + ca[pl.ds(CARRY - 2 + k, tq), :] * scw_ref[k:k + 1, :]
    ycat[:, 0:d] = (proj[:, d:2 * d] * conv_u).astype(BF16)

    x0 = 3 * d + SSD_DIM
    cbuf[CARRY:CARRY + tq, :] = proj[:, x0:x0 + SSD_CONV_DIM]
    xbc = cbuf[pl.ds(CARRY - 3, tq), :] * cw_ref[0:1, :]
    for k in range(1, SSD_CONV_W):
        xbc = xbc + cbuf[pl.ds(CARRY - 3 + k, tq), :] * cw_ref[k:k + 1, :]
    xbc = _silu(xbc + cb_ref[...])

    dt = _softplus(proj[:, x0 + SSD_CONV_DIM:IN0_PAD] + dtb_ref[...])
    a_neg = -jnp.exp(alog_ref[...])
    row = lax.broadcasted_iota(jnp.int32, (tq, tq), 0)
    col = lax.broadcasted_iota(jnp.int32, (tq, tq), 1)
    causal = row >= col
    tril = jnp.where(causal, 1.0, 0.0).astype(F32)
    cum = jnp.dot(tril, dt * a_neg, precision=HI, preferred_element_type=F32)
    eye = jnp.where(lax.broadcasted_iota(jnp.int32, (LANES, LANES), 0)
                    == lax.broadcasted_iota(jnp.int32, (LANES, LANES), 1), 1.0, 0.0).astype(F32)
    cum_t = _dot_nt(eye, cum, precision=HI)
    cum_last = cum[tq - 1:tq, :]
    e_cum = jnp.exp(cum)
    e_last = jnp.exp(cum_last)
    e_end = jnp.exp(cum_last - cum)
    z = proj[:, 3 * d:3 * d + SSD_DIM]
    dsk = dsk_ref[...]

    for g in range(SSD_GROUPS):
        b_g = xbc[:, SSD_DIM + g * SSD_N:SSD_DIM + (g + 1) * SSD_N].astype(BF16)
        c_g = xbc[:, SSD_DIM + (SSD_GROUPS + g) * SSD_N:
                  SSD_DIM + (SSD_GROUPS + g + 1) * SSD_N].astype(BF16)
        cb_g = _dot_nt(c_g, b_g)
        y_parts = []
        for hh in range(SSD_HPG):
            h = g * SSD_HPG + hh
            x_h = xbc[:, h * SSD_HD:(h + 1) * SSD_HD]
            xdt = x_h * dt[:, h:h + 1]
            lmat = jnp.where(causal, jnp.exp(cum[:, h:h + 1] - cum_t[h:h + 1, :]), 0.0)
            y_h = _dot((cb_g * lmat).astype(BF16), xdt.astype(BF16))
            st = hst[h]
            y_h = y_h + e_cum[:, h:h + 1] * _dot_nt(c_g, st.astype(BF16))
            hst[h] = e_last[:, h:h + 1] * st + _dot_tn((xdt * e_end[:, h:h + 1]).astype(BF16), b_g)
            y_parts.append(y_h + dsk[:, h:h + 1] * x_h)
        yg = jnp.concatenate(y_parts, axis=1)
        gs = slice(g * SSD_HPG * SSD_HD, (g + 1) * SSD_HPG * SSD_HD)
        yg = yg * _silu(z[:, gs])
        yg = yg * lax.rsqrt(jnp.mean(yg * yg, axis=-1, keepdims=True) + EPS)
        ycat[:, d + gs.start:d + gs.stop] = (yg * ssdn_ref[:, gs]).astype(BF16)

    y_ref[0] = x + _dot(ycat[...], wout_ref[...])

    ca[0:CARRY, :] = ca[tq:tq + CARRY, :]
    cbuf[0:CARRY, :] = cbuf[tq:tq + CARRY, :]

    @pl.when(i == last)
    def _():
        sa_ref[0] = ca[0:CARRY, :]
        sb_ref[0] = cbuf[0:CARRY, :]
        hout_ref[0] = hst[...]


def _mixer0(x, tq, g, w_in_bf, scw, cw, cb, dtb, alog, dsk, ssdn, w_out_bf, sa0, sb0, h0):
    bsz, t, d = x.shape
    n_col_chunks = 7
    kern = functools.partial(_mixer0_kernel, tq=tq, n_col_chunks=n_col_chunks)
    per_b = lambda b, i: (b, 0, 0)
    return pl.pallas_call(
        kern,
        grid=(bsz, t // tq),
        in_specs=[pl.BlockSpec((1, tq, d), lambda b, i: (b, i, 0)),
                  _const_spec((1, d)), _const_spec((d, IN0_PAD)),
                  _const_spec((SUBLANES, d)), _const_spec((SUBLANES, SSD_CONV_DIM)),
                  _const_spec((1, SSD_CONV_DIM)), _const_spec((1, LANES)), _const_spec((1, LANES)),
                  _const_spec((1, LANES)), _const_spec((1, SSD_DIM)),
                  _const_spec((d + SSD_DIM, d)),
                  pl.BlockSpec((1, CARRY, d), per_b),
                  pl.BlockSpec((1, CARRY, SSD_CONV_DIM), per_b),
                  pl.BlockSpec((1, SSD_HEADS, SSD_HD, SSD_N), lambda b, i: (b, 0, 0, 0))],
        out_specs=[pl.BlockSpec((1, tq, d), lambda b, i: (b, i, 0)),
                   pl.BlockSpec((1, CARRY, d), per_b),
                   pl.BlockSpec((1, CARRY, SSD_CONV_DIM), per_b),
                   pl.BlockSpec((1, SSD_HEADS, SSD_HD, SSD_N), lambda b, i: (b, 0, 0, 0))],
        out_shape=[jax.ShapeDtypeStruct((bsz, t, d), F32),
                   jax.ShapeDtypeStruct((bsz, CARRY, d), F32),
                   jax.ShapeDtypeStruct((bsz, CARRY, SSD_CONV_DIM), F32),
                   jax.ShapeDtypeStruct((bsz, SSD_HEADS, SSD_HD, SSD_N), F32)],
        scratch_shapes=[pltpu.VMEM((tq, IN0_PAD), F32),
                        pltpu.VMEM((CARRY + tq, d), F32),
                        pltpu.VMEM((CARRY + tq, SSD_CONV_DIM), F32),
                        pltpu.VMEM((SSD_HEADS, SSD_HD, SSD_N), F32),
                        pltpu.VMEM((tq, d + SSD_DIM), BF16)],
        compiler_params=_cparams(("parallel", "arbitrary")),
        name="mixer0",
    )(x, g, w_in_bf, scw, cw, cb, dtb, alog, dsk, ssdn, w_out_bf, sa0, sb0, h0)


def _memattn_kernel(x_ref, g_ref, wq_ref, qn_ref, kv_ref, wo_ref, o_ref, *, n_mem):
    x = x_ref[0]
    hn = _rms(x, g_ref[...]).astype(BF16)
    q = _dot(hn, wq_ref[...])
    outs = []
    for h in range(MEM_HEADS):
        sl = slice(h * MEM_HD, (h + 1) * MEM_HD)
        k_h = kv_ref[0, 0, pl.ds(h, n_mem, stride=MEM_SLOTS), :].astype(BF16)
        v_h = kv_ref[0, 0, pl.ds(MEM_HEADS + h, n_mem, stride=MEM_SLOTS), :].astype(BF16)
        qh = _rms(q[:, sl], qn_ref[...]).astype(BF16)
        s = _dot_nt(qh, k_h) * (MEM_HD ** -0.5)
        m = jnp.max(s, axis=-1, keepdims=True)
        e = jnp.exp(s - m)
        p = e / jnp.sum(e, axis=-1, keepdims=True)
        outs.append(_dot(p.astype(BF16), v_h))
    o = jnp.concatenate(outs, axis=1).astype(BF16)
    o_ref[0] = x + _dot(o, wo_ref[...])


def _memattn(x, tq, g, wq_bf, qn, kv4, layer, wo_bf):
    bsz, t, d = x.shape
    rows = kv4.shape[2]
    hw = MEM_HEADS * MEM_HD
    return pl.pallas_call(
        functools.partial(_memattn_kernel, n_mem=rows // MEM_SLOTS),
        grid=(bsz, t // tq),
        in_specs=[pl.BlockSpec((1, tq, d), lambda b, i: (b, i, 0)),
                  _const_spec((1, d)), _const_spec((d, hw)), _const_spec((1, MEM_HD)),
                  pl.BlockSpec((1, 1, rows, MEM_HD), lambda b, i: (layer, b, 0, 0)),
                  _const_spec((hw, d))],
        out_specs=pl.BlockSpec((1, tq, d), lambda b, i: (b, i, 0)),
        out_shape=jax.ShapeDtypeStruct((bsz, t, d), F32),
        compiler_params=_cparams(("parallel", "parallel")),
        name="memattn",
    )(x, g, wq_bf, qn, kv4, wo_bf)


def _ffn_kernel(x_ref, g_ref, wg_ref, wu_ref, wd_ref, o_ref, hn_scr, acc):
    f = pl.program_id(1)

    @pl.when(f == 0)
    def _():
        hn_scr[...] = _rms(x_ref[...], g_ref[...]).astype(BF16)
        acc[...] = jnp.zeros_like(acc)

    hn = hn_scr[...]
    a = _silu(_dot(hn, wg_ref[...])) * _dot(hn, wu_ref[...])
    acc[...] += _dot(a.astype(BF16), wd_ref[...])

    @pl.when(f == pl.num_programs(1) - 1)
    def _():
        o_ref[...] = x_ref[...] + acc[...]


def _ffn(x2d, tm, g, wg_bf, wu_bf, wd_bf):
    m, d = x2d.shape
    dff = wg_bf.shape[1]
    tf = 256
    return pl.pallas_call(
        _ffn_kernel,
        grid=(m // tm, dff // tf),
        in_specs=[pl.BlockSpec((tm, d), lambda i, f: (i, 0)), _const_spec((1, d)),
                  pl.BlockSpec((d, tf), lambda i, f: (0, f)),
                  pl.BlockSpec((d, tf), lambda i, f: (0, f)),
                  pl.BlockSpec((tf, d), lambda i, f: (f, 0))],
        out_specs=pl.BlockSpec((tm, d), lambda i, f: (i, 0)),
        out_shape=jax.ShapeDtypeStruct((m, d), F32),
        scratch_shapes=[pltpu.VMEM((tm, d), BF16), pltpu.VMEM((tm, d), F32)],
        compiler_params=_cparams(("parallel", "arbitrary")),
        name="ffn",
    )(x2d, g, wg_bf, wu_bf, wd_bf)


def _proj_res_kernel(a_ref, w_ref, x_ref, o_ref):
    o_ref[...] = x_ref[...] + _dot(a_ref[...].astype(BF16), w_ref[...])


def _proj_res(a2d, w_bf, x2d):
    m, k = a2d.shape
    n = w_bf.shape[1]
    tm = min(256, m)
    return pl.pallas_call(
        _proj_res_kernel,
        grid=(m // tm,),
        in_specs=[pl.BlockSpec((tm, k), lambda i: (i, 0)), _const_spec((k, n)),
                  pl.BlockSpec((tm, n), lambda i: (i, 0))],
        out_specs=pl.BlockSpec((tm, n), lambda i: (i, 0)),
        out_shape=jax.ShapeDtypeStruct((m, n), F32),
        compiler_params=_cparams(("parallel",)),
        name="proj_res",
    )(a2d, w_bf, x2d)


def _nsa_proj_kernel(x_ref, g_ref, wq_ref, wg_ref, wkv_ref, bd_ref, qg_ref, c_ref, s1_ref, s2_ref,
                     kg_ref, cos_ref, sin_ref,
                     q_ref, gate_ref, rows_ref, win_ref, kvbf_ref):
    hn = _rms(x_ref[0], g_ref[...]).astype(BF16)
    q = _dot(hn, wq_ref[...])
    bd = bd_ref[...]
    scale = NSA_HD ** -0.5
    for c in range(NSA_HEADS * NSA_HD // LANES):
        sl = slice(c * LANES, (c + 1) * LANES)
        q_ref[0, :, sl] = scale * _headnorm_rope_tok(q[:, sl], qg_ref[...], bd, c_ref[...],
                                                     s1_ref[...], s2_ref[...])
    gate_ref[0] = jax.nn.sigmoid(_dot(hn, wg_ref[...]))
    kv_t = _dot_nt(wkv_ref[...], hn)
    cos_t, sin_t = cos_ref[...], sin_ref[...]

    def put_normed(src_row, gain_idx, dst_ref, dst_row, bf_row):
        for g in range(NSA_KV):
            r1, r2, rest = _headnorm_rope_ch(kv_t[src_row + g * NSA_HD:src_row + (g + 1) * NSA_HD],
                                             kg_ref[gain_idx], cos_t, sin_t)
            full = jnp.concatenate([r1, r2, rest], axis=0)
            dst_ref[0, dst_row + g * NSA_HD:dst_row + (g + 1) * NSA_HD, :] = full
            kvbf_ref[0, bf_row + g * NSA_HD:bf_row + (g + 1) * NSA_HD, :] = full.astype(BF16)

    w = KV_W
    rows_ref[0, 0:2 * w, :] = kv_t[0:2 * w]
    put_normed(2 * w, 0, rows_ref, 2 * w, 0)
    rows_ref[0, 3 * w:4 * w, :] = kv_t[3 * w:4 * w]
    kvbf_ref[0, w:2 * w, :] = kv_t[3 * w:4 * w].astype(BF16)
    put_normed(4 * w, 1, win_ref, 0, 2 * w)
    win_ref[0, w:2 * w, :] = kv_t[5 * w:6 * w]
    kvbf_ref[0, 3 * w:4 * w, :] = kv_t[5 * w:6 * w].astype(BF16)


def _nsa_proj(x, tm, g, wq_bf, wg_bf, wkv_t_bf, qgain2, kgain, tabs):
    bsz, t, d = x.shape
    c, s1, s2, cos_t, sin_t = tabs
    hq = NSA_HEADS * NSA_HD
    w = KV_W
    tok = lambda b, i: (b, i, 0)
    chan = lambda b, i: (b, 0, i)
    return pl.pallas_call(
        _nsa_proj_kernel,
        grid=(bsz, t // tm),
        in_specs=[pl.BlockSpec((1, tm, d), tok), _const_spec((1, d)),
                  _const_spec((d, hq)), _const_spec((d, LANES)), _const_spec((6 * w, d)),
                  _const_spec((LANES, LANES)), _const_spec((1, LANES)),
                  pl.BlockSpec((tm, LANES), lambda b, i: (i, 0)),
                  pl.BlockSpec((tm, LANES), lambda b, i: (i, 0)),
                  pl.BlockSpec((tm, LANES), lambda b, i: (i, 0)),
                  _const_spec((2, NSA_HD, 1)),
                  pl.BlockSpec((ROT_HALF, tm), lambda b, i: (0, i)),
                  pl.BlockSpec((ROT_HALF, tm), lambda b, i: (0, i))],
        out_specs=[pl.BlockSpec((1, tm, hq), tok), pl.BlockSpec((1, tm, LANES), tok),
                   pl.BlockSpec((1, 4 * w, tm), chan), pl.BlockSpec((1, 2 * w, tm), chan),
                   pl.BlockSpec((1, 4 * w, tm), chan)],
        out_shape=[jax.ShapeDtypeStruct((bsz, t, hq), F32),
                   jax.ShapeDtypeStruct((bsz, t, LANES), F32),
                   jax.ShapeDtypeStruct((bsz, 4 * w, t), F32),
                   jax.ShapeDtypeStruct((bsz, 2 * w, t), F32),
                   jax.ShapeDtypeStruct((bsz, 4 * w, t), BF16)],
        compiler_params=_cparams(("parallel", "parallel")),
        name="nsa_proj",
    )(x, g, wq_bf, wg_bf, wkv_t_bf, _seg_mean_matrix(), qgain2, c, s1, s2, kgain, cos_t, sin_t)


def _half_sums(xt, pe_ref, w1_ref, nh):
    out = []
    tiles = KV_W // LANES
    for kind in range(2):
        rows = [jnp.concatenate([xt[kind * tiles + c, pl.ds(j, nh, stride=CMP_STRIDE), :]
                                 for c in range(tiles)], axis=1) for j in range(CMP_STRIDE)]
        for half in range(2):
            lhs = jnp.concatenate(
                [(rows[j] + pe_ref[kind, half * CMP_STRIDE + j:half * CMP_STRIDE + j + 1, :]).astype(BF16)
                 for j in range(CMP_STRIDE)], axis=1)
            out.append(_dot(lhs, w1_ref[kind, half]))
    return out


def _finish_compress(a_k, b_k, a_v, b_v, w2_ref, kg_ref, bd, c, s1, s2):
    nh = a_k.shape[0]
    kc = _dot(_silu(a_k + pltpu.roll(b_k, nh - 1, 0)).astype(BF16), w2_ref[0])
    vc = _dot(_silu(a_v + pltpu.roll(b_v, nh - 1, 0)).astype(BF16), w2_ref[1])
    kc = jnp.concatenate([_headnorm_rope_tok(kc[:, h * LANES:(h + 1) * LANES], kg_ref[...], bd, c, s1, s2)
                          for h in range(KV_W // LANES)], axis=1)
    return kc, vc


def _p_compress_kernel(rows_ref, pe_ref, w1_ref, w2_ref, kg_ref, bd_ref, c_ref, s1_ref, s2_ref,
                       kc_ref, vc_ref, xt, *, nh):
    for c in range(2 * KV_W // LANES):
        xt[c] = rows_ref[0, c * LANES:(c + 1) * LANES, :].T
    a_k, b_k, a_v, b_v = _half_sums(xt, pe_ref, w1_ref, nh)
    kc, vc = _finish_compress(a_k, b_k, a_v, b_v, w2_ref, kg_ref, bd_ref[...], c_ref[...],
                              s1_ref[...], s2_ref[...])
    kc_ref[0] = kc.astype(BF16)
    vc_ref[0] = vc.astype(BF16)


def _p_compress(rows_t, pe4, w1bd, w2bd, kgain2, ctabs):
    bsz, _, t = rows_t.shape
    nh = t // CMP_STRIDE
    c, s1, s2 = ctabs
    w = KV_W
    return pl.pallas_call(
        functools.partial(_p_compress_kernel, nh=nh),
        grid=(bsz,),
        in_specs=[pl.BlockSpec((1, 2 * w, t), lambda b: (b, 0, 0)),
                  _const_spec((2, CMP_BLOCK, w)), _const_spec((2, 2, CMP_STRIDE * w, w)),
                  _const_spec((2, w, w)), _const_spec((1, LANES)), _const_spec((LANES, LANES)),
                  _const_spec((nh, LANES)), _const_spec((nh, LANES)), _const_spec((nh, LANES))],
        out_specs=[pl.BlockSpec((1, nh, w), lambda b: (b, 0, 0)),
                   pl.BlockSpec((1, nh, w), lambda b: (b, 0, 0))],
        out_shape=[jax.ShapeDtypeStruct((bsz, nh, w), BF16), jax.ShapeDtypeStruct((bsz, nh, w), BF16)],
        scratch_shapes=[pltpu.VMEM((2 * w // LANES, t, LANES), F32)],
        compiler_params=_cparams(("parallel",)),
        name="p_compress",
    )(rows_t, pe4, w1bd, w2bd, kgain2, _seg_mean_matrix(), c, s1, s2)


def _stack_heads(q, g):
    return jnp.concatenate([q[:, (g * NSA_HPG + hh) * NSA_HD:(g * NSA_HPG + hh + 1) * NSA_HD]
                            for hh in range(NSA_HPG)], axis=0)


def _stack_gate(gates, g, j):
    return jnp.concatenate([gates[:, (g * NSA_HPG + hh) * 3 + j:(g * NSA_HPG + hh) * 3 + j + 1]
                            for hh in range(NSA_HPG)], axis=0)


def _tile_rows(m, n):
    return jnp.concatenate([m] * n, axis=0)


def _topk_mask(imp, k):
    r, s = imp.shape
    idx = lax.broadcasted_iota(jnp.int32, (r, s), 1)
    rank = jnp.zeros((r, s), F32)
    for s2 in range(s):
        col = imp[:, s2:s2 + 1]
        beats = jnp.where(col > imp, 1.0, jnp.where((col == imp) & (idx > s2), 1.0, 0.0))
        rank = rank + beats
    return rank < k


def _topk_bias_t(imp_t, k):
    s, r = imp_t.shape
    idx = lax.broadcasted_iota(jnp.int32, (s, r), 0)
    rank = jnp.zeros((s, r), F32)
    for s2 in range(s):
        row = imp_t[s2:s2 + 1, :]
        rank = rank + jnp.where(row > imp_t, 1.0, jnp.where((row == imp_t) & (idx > s2), 1.0, 0.0))
    return jnp.where((rank < k) & (imp_t > 0.5 * NEG), 0.0, NEG)


SEL_COLS = 32
DEN_COL = NSA_HD + SEL_COLS


def _p_attn_kernel(q_ref, gate_ref, kc_ref, vc_ref, kvs_ref, *refs, tq, t_all, kch, n_wb):
    win_refs = refs[:n_wb]
    movt_ref, epad_ref, eye4_ref, wo_ref, x_ref, y_ref, o_scr, qa_scr, oc_scr, os_scr = refs[n_wb:]
    i = pl.program_id(1)
    q = q_ref[0]
    gates = gate_ref[0]
    nh = kc_ref.shape[1]
    n_slc = movt_ref.shape[0]
    pos_q = i * tq + lax.broadcasted_iota(jnp.int32, (tq, 1), 0)
    pos_q_row = i * tq + lax.broadcasted_iota(jnp.int32, (1, tq), 1)
    pos_c = lax.broadcasted_iota(jnp.int32, (tq, nh), 1) * CMP_STRIDE + (CMP_BLOCK - 1)
    ok_c = _tile_rows(pos_c <= pos_q, NSA_HPG)
    blk_t = lax.broadcasted_iota(jnp.int32, (n_slc, tq), 0)
    cur_t = pos_q_row // SLC_BLOCK
    w = KV_W

    for g in range(NSA_KV):
        qg = _stack_heads(q, g)
        gs = slice(g * NSA_HD, (g + 1) * NSA_HD)
        s = jnp.where(ok_c, _dot_nt(qg.astype(BF16), kc_ref[0, :, gs]), NEG)
        e = jnp.exp(s - jnp.max(s, axis=-1, keepdims=True))
        p = jnp.where(ok_c, e / jnp.sum(e, axis=-1, keepdims=True), 0.0)
        oc_scr[g] = _dot(p.astype(BF16), vc_ref[0, :, gs])
        psum = p[0:tq]
        for hh in range(1, NSA_HPG):
            psum = psum + p[hh * tq:(hh + 1) * tq]
        imp_t = _dot_nt(movt_ref[...], psum, precision=HI)
        imp_t = jnp.where((blk_t == cur_t) | (blk_t == 0), BIG, imp_t)
        imp_t = jnp.where(blk_t > cur_t, NEG, imp_t)
        selb_t = _topk_bias_t(imp_t, SLC_TOP).astype(BF16)
        selb4 = _dot_nt(eye4_ref[...], selb_t)
        pad = jnp.zeros((NSA_HPG * tq, LANES - NSA_HD - n_slc), F32)
        qa_scr[g] = jnp.concatenate([qg, selb4, pad], axis=1).astype(BF16)

    n_var = t_all // kch
    need = (i * tq) // kch + 1
    for nv in range(1, n_var + 1):
        @pl.when(need == nv)
        def _(nv=nv):
            klen = nv * kch
            head = klen - kch
            tail_pos = head + lax.broadcasted_iota(jnp.int32, (tq, kch), 1)
            cb4 = _tile_rows(jnp.where(tail_pos <= pos_q, 0.0, NEG), NSA_HPG)
            for g in range(NSA_KV):
                qa = qa_scr[g]
                ks = slice(g * NSA_HD, (g + 1) * NSA_HD)
                vs = slice(w + g * NSA_HD, w + (g + 1) * NSA_HD)
                k_tail = jnp.concatenate([kvs_ref[0, ks, head:klen], epad_ref[:, head:klen]], axis=0)
                s_t = _dot(qa, k_tail) + cb4
                m = jnp.max(s_t, axis=-1, keepdims=True)
                if head:
                    k_head = jnp.concatenate([kvs_ref[0, ks, 0:head], epad_ref[:, 0:head]], axis=0)
                    s_h = _dot(qa, k_head)
                    m = jnp.maximum(m, jnp.max(s_h, axis=-1, keepdims=True))
                v_tail = jnp.concatenate([kvs_ref[0, vs, head:klen], epad_ref[:, head:klen]], axis=0)
                o = _dot_nt(jnp.exp(s_t - m).astype(BF16), v_tail)
                if head:
                    v_head = jnp.concatenate([kvs_ref[0, vs, 0:head], epad_ref[:, 0:head]], axis=0)
                    o = o + _dot_nt(jnp.exp(s_h - m).astype(BF16), v_head)
                os_scr[g] = o[:, 0:NSA_HD] / o[:, DEN_COL:DEN_COL + 1]

    n_wb = len(win_refs)
    win_blocks = win_refs[::-1]
    wpos = (i - (n_wb - 1)) * tq + lax.broadcasted_iota(jnp.int32, (tq, n_wb * tq), 1)
    dpos = pos_q - wpos
    wb4 = _tile_rows(jnp.where((dpos >= 0) & (dpos < WINDOW) & (wpos >= 0), 0.0, NEG), NSA_HPG)
    for g in range(NSA_KV):
        qg = qa_scr[g, :, 0:NSA_HD]
        k_w = jnp.concatenate([r[0, g * NSA_HD:(g + 1) * NSA_HD, :] for r in win_blocks], axis=1)
        v_w = jnp.concatenate([r[0, w + g * NSA_HD:w + (g + 1) * NSA_HD, :] for r in win_blocks], axis=1)
        v_w = jnp.concatenate([v_w, epad_ref[:, 0:n_wb * tq]], axis=0)
        s = _dot(qg, k_w) + wb4
        e = jnp.exp(s - jnp.max(s, axis=-1, keepdims=True))
        o_w = _dot_nt(e.astype(BF16), v_w)
        o_w = o_w[:, 0:NSA_HD] / o_w[:, DEN_COL:DEN_COL + 1]
        o = (_stack_gate(gates, g, 0) * oc_scr[g] + _stack_gate(gates, g, 1) * os_scr[g]
             + _stack_gate(gates, g, 2) * o_w)
        for hh in range(NSA_HPG):
            h = g * NSA_HPG + hh
            o_scr[:, h * NSA_HD:(h + 1) * NSA_HD] = o[hh * tq:(hh + 1) * tq].astype(BF16)

    y_ref[0] = x_ref[0] + _dot(o_scr[...], wo_ref[...])


def _p_attn(q, gates, kc, vc, kvbf, mov_t, epad, wo_bf, x, tq):
    bsz, t, hq = q.shape
    nh = kc.shape[1]
    w = KV_W
    n_slc = mov_t.shape[0]
    kch = min(512, t)
    rows = NSA_HPG * tq
    n_wb = -(-WINDOW // tq) + 1
    assert kch % tq == 0 and n_wb * tq <= t
    eye4 = jnp.tile(jnp.eye(tq, dtype=BF16), (NSA_HPG, 1))
    tok = lambda b, i: (b, i, 0)
    win_specs = [pl.BlockSpec((1, 2 * w, tq), functools.partial(
        lambda b, i, k: (b, 1, jnp.maximum(i - k, 0)), k=k)) for k in range(n_wb)]
    return pl.pallas_call(
        functools.partial(_p_attn_kernel, tq=tq, t_all=t, kch=kch, n_wb=n_wb),
        grid=(bsz, t // tq),
        in_specs=[pl.BlockSpec((1, tq, hq), tok), pl.BlockSpec((1, tq, LANES), tok),
                  pl.BlockSpec((1, nh, w), lambda b, i: (b, 0, 0)),
                  pl.BlockSpec((1, nh, w), lambda b, i: (b, 0, 0)),
                  pl.BlockSpec((1, 2 * w, t), lambda b, i: (b, 0, 0))] + win_specs + [
                  _const_spec((n_slc, nh)), _const_spec((LANES - NSA_HD, t)),
                  _const_spec((rows, tq)), _const_spec((hq, D_MODEL)),
                  pl.BlockSpec((1, tq, D_MODEL), tok)],
        out_specs=pl.BlockSpec((1, tq, D_MODEL), tok),
        out_shape=jax.ShapeDtypeStruct((bsz, t, D_MODEL), F32),
        scratch_shapes=[pltpu.VMEM((tq, hq), BF16), pltpu.VMEM((NSA_KV, rows, LANES), BF16),
                        pltpu.VMEM((NSA_KV, rows, NSA_HD), F32), pltpu.VMEM((NSA_KV, rows, NSA_HD), F32)],
        compiler_params=_cparams(("parallel", "parallel")),
        name="p_attn",
    )(q, gates, kc, vc, kvbf, *([kvbf] * n_wb), mov_t, epad, eye4, wo_bf, x)


MAX_PAGES_COMPRESS = 32
MAX_PAGES_SLC = 32


def _s_compress_kernel(pt_ref, *refs, nh, page, pps):
    pages = refs[:pps]
    pe_ref, w1_ref, a_ref, b_ref, xt = refs[pps:]
    for k, pg in enumerate(pages):
        for c in range(2 * KV_W // LANES):
            xt[c, k * page:(k + 1) * page, :] = pg[0, c * LANES:(c + 1) * LANES, :].T
    a_k, b_k, a_v, b_v = _half_sums(xt, pe_ref, w1_ref, nh)
    a_ref[0] = jnp.concatenate([a_k, a_v], axis=1)
    b_ref[0] = jnp.concatenate([b_k, b_v], axis=1)


def _s_compress(cache_t, page_table, pe4, w1bd):
    bsz, n_pages = page_table.shape
    page = cache_t.shape[2]
    w = KV_W
    pps = min(MAX_PAGES_COMPRESS, n_pages)
    steps = n_pages // pps
    nh = pps * page // CMP_STRIDE
    page_specs = [pl.BlockSpec((1, 2 * w, page), functools.partial(
        lambda b, s, pt, k: (pt[b, s * pps + k], 0, 0), k=k)) for k in range(pps)]
    grid_spec = pltpu.PrefetchScalarGridSpec(
        num_scalar_prefetch=1,
        grid=(bsz, steps),
        in_specs=page_specs + [pl.BlockSpec((2, CMP_BLOCK, w), lambda b, s, pt: (0, 0, 0)),
                               pl.BlockSpec((2, 2, CMP_STRIDE * w, w), lambda b, s, pt: (0, 0, 0, 0))],
        out_specs=[pl.BlockSpec((1, nh, 2 * w), lambda b, s, pt: (b, s, 0)),
                   pl.BlockSpec((1, nh, 2 * w), lambda b, s, pt: (b, s, 0))],
        scratch_shapes=[pltpu.VMEM((2 * w // LANES, pps * page, LANES), F32)])
    out = jax.ShapeDtypeStruct((bsz, steps * nh, 2 * w), F32)
    return pl.pallas_call(
        functools.partial(_s_compress_kernel, nh=nh, page=page, pps=pps),
        grid_spec=grid_spec,
        out_shape=[out, out],
        compiler_params=_cparams(("parallel", "arbitrary")),
        name="s_compress",
    )(page_table, *([cache_t] * pps), pe4, w1bd)


def _s_cmp_kernel(a_ref, b_ref, q_ref, w2_ref, kg_ref, bd_ref, c_ref, s1_ref, s2_ref, mov_ref,
                  oc_ref, sel_ref, *, ts, past):
    ab = a_ref[0]
    bb = b_ref[0]
    w = KV_W
    kc, vc = _finish_compress(ab[:, 0:w], bb[:, 0:w], ab[:, w:2 * w], bb[:, w:2 * w], w2_ref, kg_ref,
                              bd_ref[...], c_ref[...], s1_ref[...], s2_ref[...])
    kc = kc.astype(BF16)
    vc = vc.astype(BF16)
    nh = kc.shape[0]
    q = q_ref[0]
    pos_q = past + lax.broadcasted_iota(jnp.int32, (ts, 1), 0)
    pos_c = lax.broadcasted_iota(jnp.int32, (ts, nh), 1) * CMP_STRIDE + (CMP_BLOCK - 1)
    ok_c = _tile_rows(pos_c <= pos_q, NSA_HPG)
    n_blk = mov_ref.shape[1]
    blk = lax.broadcasted_iota(jnp.int32, (ts, n_blk), 1)
    for g in range(NSA_KV):
        qg = _stack_heads(q, g).astype(BF16)
        gs = slice(g * NSA_HD, (g + 1) * NSA_HD)
        s = jnp.where(ok_c, _dot_nt(qg, kc[:, gs]), NEG)
        e = jnp.exp(s - jnp.max(s, axis=-1, keepdims=True))
        p = jnp.where(ok_c, e / jnp.sum(e, axis=-1, keepdims=True), 0.0)
        o_c = _dot(p.astype(BF16), vc[:, gs])
        for hh in range(NSA_HPG):
            h = g * NSA_HPG + hh
            oc_ref[0, :, h * NSA_HD:(h + 1) * NSA_HD] = o_c[hh * ts:(hh + 1) * ts]
        psum = p[0:ts]
        for hh in range(1, NSA_HPG):
            psum = psum + p[hh * ts:(hh + 1) * ts]
        imp = jnp.dot(psum, mov_ref[...], precision=HI, preferred_element_type=F32)
        imp = jnp.where(blk == 0, BIG, imp)
        sel_ref[0, g * ts:(g + 1) * ts, :] = jnp.where(_topk_mask(imp, SLC_TOP - 1), 1.0, 0.0)


def _s_cmp(a, b, q, w2bd, kgain2, ctabs, mov, past):
    bsz, nh, _ = a.shape
    ts = q.shape[1]
    hq = q.shape[2]
    c, s1, s2 = ctabs
    w = KV_W
    n_blk = mov.shape[1]
    per_b = lambda i: (i, 0, 0)
    return pl.pallas_call(
        functools.partial(_s_cmp_kernel, ts=ts, past=past),
        grid=(bsz,),
        in_specs=[pl.BlockSpec((1, nh, 2 * w), per_b), pl.BlockSpec((1, nh, 2 * w), per_b),
                  pl.BlockSpec((1, ts, hq), per_b),
                  _const_spec((2, w, w)), _const_spec((1, LANES)), _const_spec((LANES, LANES)),
                  _const_spec((nh, LANES)), _const_spec((nh, LANES)), _const_spec((nh, LANES)),
                  _const_spec((nh, n_blk))],
        out_specs=[pl.BlockSpec((1, ts, hq), per_b), pl.BlockSpec((1, NSA_KV * ts, n_blk), per_b)],
        out_shape=[jax.ShapeDtypeStruct((bsz, ts, hq), F32),
                   jax.ShapeDtypeStruct((bsz, NSA_KV * ts, n_blk), F32)],
        compiler_params=_cparams(("parallel",)),
        name="s_cmp",
    )(a, b, q, w2bd, kgain2, _seg_mean_matrix(), c, s1, s2, mov)


def _s_slc_kernel(pt_ref, *refs, ts, pps):
    pages = refs[:pps]
    q_ref, sel_ref, new_ref, exp_ref, o_ref, m_scr, l_scr, acc = refs[pps:]
    s_id = pl.program_id(1)
    w = KV_W
    rows = NSA_HPG * ts

    @pl.when(s_id == 0)
    def _():
        m_scr[...] = jnp.full(m_scr.shape, NEG, F32)
        l_scr[...] = jnp.zeros(l_scr.shape, F32)
        acc[...] = jnp.zeros(acc.shape, F32)

    q = q_ref[0]
    sel = sel_ref[0, 0]
    for g in range(NSA_KV):
        qg = _stack_heads(q, g).astype(BF16)
        k_t = jnp.concatenate([pg[0, g * NSA_HD:(g + 1) * NSA_HD, :] for pg in pages], axis=1).astype(BF16)
        v_t = jnp.concatenate([pg[0, w + g * NSA_HD:w + (g + 1) * NSA_HD, :] for pg in pages],
                              axis=1).astype(BF16)
        selk = _dot(sel[g * ts:(g + 1) * ts].astype(BF16), exp_ref[...])
        ok = _tile_rows(selk > 0.5, NSA_HPG)
        s = jnp.where(ok, _dot(qg, k_t), NEG)
        m_old = m_scr[g]
        m_new = jnp.maximum(m_old, jnp.max(s, axis=-1, keepdims=True))
        alpha = jnp.exp(m_old - m_new)
        e = jnp.where(ok, jnp.exp(s - m_new), 0.0)
        l_scr[g] = alpha * l_scr[g] + jnp.sum(e, axis=-1, keepdims=True)
        acc[g] = alpha * acc[g] + _dot_nt(e.astype(BF16), v_t)
        m_scr[g] = m_new

    @pl.when(s_id == pl.num_programs(1) - 1)
    def _():
        t_row = lax.broadcasted_iota(jnp.int32, (ts, ts), 0)
        t_col = lax.broadcasted_iota(jnp.int32, (ts, ts), 1)
        ok_n = _tile_rows(t_col <= t_row, NSA_HPG)
        for g in range(NSA_KV):
            qg = _stack_heads(q, g).astype(BF16)
            k_n = new_ref[0, g * NSA_HD:(g + 1) * NSA_HD, :].astype(BF16)
            v_n = new_ref[0, w + g * NSA_HD:w + (g + 1) * NSA_HD, :].astype(BF16)
            s = jnp.where(ok_n, _dot(qg, k_n), NEG)
            m_old = m_scr[g]
            m_new = jnp.maximum(m_old, jnp.max(s, axis=-1, keepdims=True))
            alpha = jnp.exp(m_old - m_new)
            e = jnp.where(ok_n, jnp.exp(s - m_new), 0.0)
            l_fin = alpha * l_scr[g] + jnp.sum(e, axis=-1, keepdims=True)
            o = (alpha * acc[g] + _dot_nt(e.astype(BF16), v_n)) / l_fin
            for hh in range(NSA_HPG):
                h = g * NSA_HPG + hh
                o_ref[0, :, h * NSA_HD:(h + 1) * NSA_HD] = o[hh * ts:(hh + 1) * ts]


def _s_slc(cache_t, page_table, q, sel_steps, new_t, expand, pps):
    bsz, n_pages = page_table.shape
    page = cache_t.shape[2]
    ts, hq = q.shape[1], q.shape[2]
    w = KV_W
    steps = n_pages // pps
    bps = sel_steps.shape[3]
    page_specs = [pl.BlockSpec((1, 2 * w, page), functools.partial(
        lambda b, s, pt, k: (pt[b, s * pps + k], 1, 0), k=k)) for k in range(pps)]
    rows = NSA_HPG * ts
    grid_spec = pltpu.PrefetchScalarGridSpec(
        num_scalar_prefetch=1,
        grid=(bsz, steps),
        in_specs=page_specs + [
            pl.BlockSpec((1, ts, hq), lambda b, s, pt: (b, 0, 0)),
            pl.BlockSpec((1, 1, NSA_KV * ts, bps), lambda b, s, pt: (b, s, 0, 0)),
            pl.BlockSpec((1, 2 * w, ts), lambda b, s, pt: (b, 0, 0)),
            pl.BlockSpec((bps, pps * page), lambda b, s, pt: (0, 0))],
        out_specs=pl.BlockSpec((1, ts, hq), lambda b, s, pt: (b, 0, 0)),
        scratch_shapes=[pltpu.VMEM((NSA_KV, rows, 1), F32), pltpu.VMEM((NSA_KV, rows, 1), F32),
                        pltpu.VMEM((NSA_KV, rows, NSA_HD), F32)])
    return pl.pallas_call(
        functools.partial(_s_slc_kernel, ts=ts, pps=pps),
        grid_spec=grid_spec,
        out_shape=jax.ShapeDtypeStruct((bsz, ts, hq), F32),
        compiler_params=_cparams(("parallel", "arbitrary")),
        name="s_slc",
    )(page_table, *([cache_t] * pps), q, sel_steps, new_t, expand)


def _s_win_kernel(q_ref, gate_ref, oc_ref, os_ref, cache_ref, new_ref, o_ref, wout_ref, *, ts):
    q = q_ref[0]
    gates = gate_ref[0]
    cache = cache_ref[0]
    new = new_ref[0]
    wb = cache.shape[1]
    w = KV_W
    t_row = lax.broadcasted_iota(jnp.int32, (ts, wb), 0)
    idx = lax.broadcasted_iota(jnp.int32, (ts, wb), 1)
    dpos = t_row + wb - idx
    ok_old = _tile_rows((dpos >= 0) & (dpos < WINDOW), NSA_HPG)
    n_row = lax.broadcasted_iota(jnp.int32, (ts, ts), 0)
    n_col = lax.broadcasted_iota(jnp.int32, (ts, ts), 1)
    ok_new = _tile_rows((n_col <= n_row) & (n_row - n_col < WINDOW), NSA_HPG)
    for g in range(NSA_KV):
        qg = _stack_heads(q, g).astype(BF16)
        gk = slice(g * NSA_HD, (g + 1) * NSA_HD)
        gv = slice(w + g * NSA_HD, w + (g + 1) * NSA_HD)
        s_old = jnp.where(ok_old, _dot(qg, cache[gk].astype(BF16)), NEG)
        s_new = jnp.where(ok_new, _dot(qg, new[gk].astype(BF16)), NEG)
        m = jnp.maximum(jnp.max(s_old, axis=-1, keepdims=True), jnp.max(s_new, axis=-1, keepdims=True))
        e_old = jnp.exp(s_old - m)
        e_new = jnp.exp(s_new - m)
        den = jnp.sum(e_old, axis=-1, keepdims=True) + jnp.sum(e_new, axis=-1, keepdims=True)
        o_w = (_dot_nt(e_old.astype(BF16), cache[gv].astype(BF16))
               + _dot_nt(e_new.astype(BF16), new[gv].astype(BF16))) / den
        for hh in range(NSA_HPG):
            h = g * NSA_HPG + hh
            hs = slice(h * NSA_HD, (h + 1) * NSA_HD)
            o_ref[0, :, hs] = (gates[:, 3 * h:3 * h + 1] * oc_ref[0, :, hs]
                               + gates[:, 3 * h + 1:3 * h + 2] * os_ref[0, :, hs]
                               + gates[:, 3 * h + 2:3 * h + 3] * o_w[hh * ts:(hh + 1) * ts])
    wout_ref[0] = pltpu.roll(cache, wb - ts, 1)
    wout_ref[0, :, wb - ts:wb] = new


def _s_win(q, gates, o_c, o_s, cache_win_t, new_win_t):
    bsz, ts, hq = q.shape
    wb = cache_win_t.shape[2]
    w = KV_W
    per_b = lambda b: (b, 0, 0)
    return pl.pallas_call(
        functools.partial(_s_win_kernel, ts=ts),
        grid=(bsz,),
        in_specs=[pl.BlockSpec((1, ts, hq), per_b), pl.BlockSpec((1, ts, LANES), per_b),
                  pl.BlockSpec((1, ts, hq), per_b), pl.BlockSpec((1, ts, hq), per_b),
                  pl.BlockSpec((1, 2 * w, wb), per_b), pl.BlockSpec((1, 2 * w, ts), per_b)],
        out_specs=[pl.BlockSpec((1, ts, hq), per_b), pl.BlockSpec((1, 2 * w, wb), per_b)],
        out_shape=[jax.ShapeDtypeStruct((bsz, ts, hq), F32),
                   jax.ShapeDtypeStruct((bsz, 2 * w, wb), F32)],
        compiler_params=_cparams(("parallel",)),
        name="s_win",
    )(q, gates, o_c, o_s, cache_win_t, new_win_t)


def _block_overlap(n_cmp_rows, n_cmp, n_slc):
    cs = jnp.arange(n_cmp_rows)[:, None] * CMP_STRIDE
    ss = jnp.arange(n_slc)[None, :] * SLC_BLOCK
    ov = jnp.minimum(cs + CMP_BLOCK, ss + SLC_BLOCK) - jnp.maximum(cs, ss)
    ov = jnp.clip(ov, 0, None).astype(F32) / CMP_BLOCK
    return jnp.where(jnp.arange(n_cmp_rows)[:, None] < n_cmp, ov, 0.0)


def _expand_matrix(n_blk, n_keys):
    return (jnp.arange(n_keys)[None, :] // SLC_BLOCK == jnp.arange(n_blk)[:, None]).astype(BF16)


def _pad_rows(a, rows):
    return jnp.pad(a, ((0, rows - a.shape[0]), (0, 0)))


def _pad_lanes(a, lanes=LANES):
    return jnp.pad(a, ((0, 0), (0, lanes - a.shape[1])))


def _blockdiag4(m):
    eye = jnp.eye(NSA_KV, dtype=m.dtype)
    out = jnp.einsum('gh,...de->...gdhe', eye, m)
    return out.reshape(m.shape[:-2] + (KV_W, KV_W))


def _layer_tail(x, layer, kv4, tq_mem, tm_ffn, p):
    bsz, t, d = x.shape
    x = _memattn(x, tq_mem, p['norm_mem'][layer][None], p['mem_w_q'][layer].astype(BF16),
                 p['mem_q_norm'][layer][None], kv4, layer, p['mem_w_o'][layer].astype(BF16))
    x2 = _ffn(x.reshape(bsz * t, d), tm_ffn, p['norm_ffn'][layer][None],
              p['ffn_w_gate'][layer].astype(BF16), p['ffn_w_up'][layer].astype(BF16),
              p['ffn_w_down'][layer].astype(BF16))
    return x2.reshape(bsz, t, d)


def _run_mixer0(x, tq, sconv_st, ssdc_st, ssd_st, p):
    w_in = _pad_lanes(p['ab_w_in'][0], IN0_PAD).astype(BF16)
    front = lambda s: jnp.pad(s, ((0, 0), (CARRY - s.shape[1], 0), (0, 0)))
    y, sa, sb, hn = _mixer0(
        x, tq, p['norm_mix'][0][None], w_in,
        _pad_rows(p['ab_sconv_w'][0], SUBLANES), _pad_rows(p['ab_ssd_conv_w'][0], SUBLANES),
        p['ab_ssd_conv_b'], _pad_lanes(p['ab_dt_bias']), _pad_lanes(p['ab_a_log']),
        _pad_lanes(p['ab_d']), p['ab_ssd_norm'], p['ab_w_out'][0].astype(BF16),
        front(sconv_st), front(ssdc_st), ssd_st)
    return y, sa[:, CARRY - (SCONV_W - 1):], sb[:, CARRY - (SSD_CONV_W - 1):], hn


def _nsa_weights(p):
    w_in = p['nsa_w_in'][0]
    hq = NSA_HEADS * NSA_HD
    wq = w_in[:, :hq].astype(BF16)
    wkv_t = w_in[:, hq:hq + 6 * KV_W].T.astype(BF16)
    wg = _pad_lanes(w_in[:, hq + 6 * KV_W:]).astype(BF16)
    qgain2 = jnp.tile(p['nsa_q_norm'][0], 2)[None]
    kn = p['nsa_k_norm'][0]
    kgain = jnp.stack([kn[1], kn[2]])[:, :, None]
    kcgain2 = jnp.tile(kn[0], 2)[None]
    pe4 = jnp.tile(p['nsa_cmp_pe'][0], (1, 1, NSA_KV))
    w1bd = _blockdiag4(p['nsa_cmp_w1'][0]).astype(BF16)
    w1bd = w1bd.reshape(2, 2, CMP_STRIDE * KV_W, KV_W)
    w2bd = _blockdiag4(p['nsa_cmp_w2'][0]).astype(BF16)
    return wq, wg, wkv_t, qgain2, kgain, kcgain2, pe4, w1bd, w2bd


def kernel(x_prompt, x_sample, mem_prompt, state_sconv, state_ssd_conv, state_ssd, cache_nsa_kv,
           cache_nsa_win, cache_mem_kv, page_table, norm_mix, norm_mem, norm_memsrc, norm_ffn,
           ab_w_in, ab_sconv_w, ab_ssd_conv_w, ab_ssd_conv_b, ab_dt_bias, ab_a_log, ab_d, ab_ssd_norm,
           ab_w_out, nsa_w_in, nsa_q_norm, nsa_k_norm, nsa_cmp_pe, nsa_cmp_w1, nsa_cmp_w2, nsa_w_out,
           mem_w_q, mem_q_norm, mem_w_kv, mem_k_norm, mem_w_o, ffn_w_gate, ffn_w_up, ffn_w_down):
    p = dict(norm_mix=norm_mix, norm_mem=norm_mem, norm_ffn=norm_ffn, ab_w_in=ab_w_in,
             ab_sconv_w=ab_sconv_w, ab_ssd_conv_w=ab_ssd_conv_w, ab_ssd_conv_b=ab_ssd_conv_b,
             ab_dt_bias=ab_dt_bias, ab_a_log=ab_a_log, ab_d=ab_d, ab_ssd_norm=ab_ssd_norm,
             ab_w_out=ab_w_out, nsa_w_in=nsa_w_in, nsa_q_norm=nsa_q_norm, nsa_k_norm=nsa_k_norm,
             nsa_cmp_pe=nsa_cmp_pe, nsa_cmp_w1=nsa_cmp_w1, nsa_cmp_w2=nsa_cmp_w2, nsa_w_out=nsa_w_out,
             mem_w_q=mem_w_q, mem_q_norm=mem_q_norm, mem_w_o=mem_w_o,
             ffn_w_gate=ffn_w_gate, ffn_w_up=ffn_w_up, ffn_w_down=ffn_w_down)
    bp, t, d = x_prompt.shape
    bs, ts, _ = x_sample.shape
    n_mem = mem_prompt.shape[1]
    depth = norm_mix.shape[0]
    assert depth == 2 and ab_w_in.shape[0] == 1 and nsa_w_in.shape[0] == 1
    w = KV_W
    hq = NSA_HEADS * NSA_HD
    wq, wg, wkv_t, qgain2, kgain, kcgain2, pe4, w1bd, w2bd = _nsa_weights(p)
    wo_nsa = nsa_w_out[0].astype(BF16)

    mem2d = mem_prompt.reshape(bp * n_mem, d)
    p_mem = _memkv(mem2d, norm_memsrc[:, None], mem_w_kv.astype(BF16), mem_k_norm[:, None])
    p_mem = p_mem.reshape(depth, bp, n_mem * MEM_SLOTS, MEM_HD)
    p_mem_kv = p_mem.reshape(depth, bp, n_mem, 2, MEM_HEADS, MEM_HD)
    s_mem = cache_mem_kv.reshape(depth, bs, n_mem * MEM_SLOTS, MEM_HD)

    tq0 = min(256, t)
    tq_mem = min(512, t)
    tm_ffn = min(1024, bp * t)
    zeros = lambda *s: jnp.zeros(s, F32)
    x, p_sconv, p_ssdc, p_ssd = _run_mixer0(
        x_prompt, tq0, zeros(bp, SCONV_W - 1, d), zeros(bp, SSD_CONV_W - 1, SSD_CONV_DIM),
        zeros(bp, SSD_HEADS, SSD_HD, SSD_N), p)
    x = _layer_tail(x, 0, p_mem, tq_mem, tm_ffn, p)

    pos_p = jnp.arange(t)
    tabs_p = _rope_tables(pos_p)
    tm_proj = min(512, t)
    q, gates, rows_t, win_t, kvbf = _nsa_proj(x, tm_proj, norm_mix[1][None], wq, wg, wkv_t, qgain2,
                                              kgain, tabs_p)
    nh = t // CMP_STRIDE
    n_cmp = nh - 1
    pos_c = jnp.arange(nh) * CMP_STRIDE + CMP_BLOCK - 1
    kc, vc = _p_compress(rows_t, pe4, w1bd, w2bd, kcgain2, _rope_tables(pos_c)[:3])
    n_slc = -(-t // SLC_BLOCK)
    tq_a = min(256, t)
    assert n_slc <= SEL_COLS
    epad = jnp.concatenate([_pad_rows(_expand_matrix(n_slc, t), SEL_COLS), jnp.ones((1, t), BF16),
                            jnp.zeros((LANES - DEN_COL - 1, t), BF16)], axis=0)
    x = _p_attn(q, gates, kc, vc, kvbf, _block_overlap(nh, n_cmp, n_slc).T, epad, wo_nsa, x, tq_a)
    y_prompt = _layer_tail(x, 1, p_mem, tq_mem, tm_ffn, p)
    p_rows = rows_t.reshape(bp, 1, 4, NSA_KV, NSA_HD, t).transpose(0, 5, 1, 2, 3, 4)
    wl = min(WINDOW, t)
    p_win = win_t[:, :, t - wl:].reshape(1, bp, 2, NSA_KV, NSA_HD, wl).transpose(0, 1, 5, 2, 3, 4)

    n_pool, page = cache_nsa_kv.shape[0], cache_nsa_kv.shape[1]
    n_pages = page_table.shape[1]
    past = n_pages * page
    pps_slc = min(MAX_PAGES_SLC, n_pages)
    assert ts <= CMP_STRIDE and past % SLC_BLOCK == 0
    assert n_pages % pps_slc == 0 and n_pages % min(MAX_PAGES_COMPRESS, n_pages) == 0
    xs, s_sconv, s_ssdc, s_ssd = _run_mixer0(x_sample, ts, state_sconv[0], state_ssd_conv[0],
                                             state_ssd[0], p)
    xs = _layer_tail(xs, 0, s_mem, ts, bs * ts, p)

    pos_s = jnp.tile(past + jnp.arange(ts), bs)
    qs, gates_s, rows_s, win_s, _ = _nsa_proj(xs.reshape(1, bs * ts, d), bs * ts, norm_mix[1][None],
                                              wq, wg, wkv_t, qgain2, kgain, _rope_tables(pos_s))
    qs = qs.reshape(bs, ts, hq)
    gates_s = gates_s.reshape(bs, ts, LANES)
    rows_s = rows_s[0].reshape(4 * w, bs, ts).transpose(1, 0, 2)
    win_s = win_s[0].reshape(2 * w, bs, ts).transpose(1, 0, 2)
    cache_t = cache_nsa_kv.transpose(0, 2, 3, 4, 5, 1).reshape(n_pool, 4 * w, page)
    a_sum, b_sum = _s_compress(cache_t, page_table, pe4, w1bd)
    nh_s = past // CMP_STRIDE
    n_cmp_s = (past + ts) // CMP_STRIDE - 1
    pos_cs = jnp.arange(nh_s) * CMP_STRIDE + CMP_BLOCK - 1
    n_blk = past // SLC_BLOCK
    o_c, sel = _s_cmp(a_sum, b_sum, qs, w2bd, kcgain2, _rope_tables(pos_cs)[:3],
                      _block_overlap(nh_s, n_cmp_s, n_blk), past)
    steps = n_pages // pps_slc
    bps = n_blk // steps
    sel_steps = sel.reshape(bs, NSA_KV * ts, steps, bps).transpose(0, 2, 1, 3)
    o_s = _s_slc(cache_t, page_table, qs, sel_steps, rows_s[:, 2 * w:],
                 _expand_matrix(bps, pps_slc * page), pps_slc)
    wb = cache_nsa_win.shape[2]
    cache_win_t = cache_nsa_win[0].transpose(0, 2, 3, 4, 1).reshape(bs, 2 * w, wb)
    o_att, new_win_t = _s_win(qs, gates_s, o_c, o_s, cache_win_t, win_s)
    xs = _proj_res(o_att.reshape(bs * ts, hq), wo_nsa, xs.reshape(bs * ts, d)).reshape(bs, ts, d)
    y_sample = _layer_tail(xs, 1, s_mem, ts, bs * ts, p)
    s_rows = rows_s.transpose(0, 2, 1).reshape(bs, ts, 1, 4, NSA_KV, NSA_HD)
    s_win = new_win_t.reshape(1, bs, 2, NSA_KV, NSA_HD, wb).transpose(0, 1, 5, 2, 3, 4)

    return (y_prompt, y_sample, p_sconv[None], p_ssdc[None], p_ssd[None], p_rows, p_win, p_mem_kv,
            s_sconv[None], s_ssdc[None], s_ssd[None], s_rows, s_win)
```

```python
import functools
import math

import jax
import jax.numpy as jnp
from jax import lax
from jax.experimental import pallas as pl
from jax.experimental.pallas import tpu as pltpu

F32 = jnp.float32
BF16 = jnp.bfloat16
HI = lax.Precision.HIGHEST

EPS = 1e-6
ROPE_THETA = 500000.0
NEG = -1e30
BIG = 1e9

LANES = 128
SUBLANES = 8
VMEM_LIMIT = 56 * 1024 * 1024

D_MODEL = 1024
SCONV_W = 3
SSD_HEADS = 16
SSD_HD = 64
SSD_DIM = SSD_HEADS * SSD_HD
SSD_GROUPS = 4
SSD_HPG = SSD_HEADS // SSD_GROUPS
SSD_N = 128
SSD_CONV_W = 4
SSD_CONV_DIM = SSD_DIM + 2 * SSD_GROUPS * SSD_N
IN0_DIM = 3 * D_MODEL + SSD_DIM + SSD_CONV_DIM + SSD_HEADS
IN0_PAD = 3 * D_MODEL + SSD_DIM + SSD_CONV_DIM + LANES
NSA_HEADS = 16
NSA_KV = 4
NSA_HD = 64
NSA_HPG = NSA_HEADS // NSA_KV
ROT_HALF = NSA_HD // 8
CMP_BLOCK = 32
CMP_STRIDE = 16
SLC_BLOCK = 64
SLC_TOP = 16
WINDOW = 512
KV_W = NSA_KV * NSA_HD
MEM_HEADS = 4
MEM_HD = 128
CARRY = SUBLANES


def _cparams(sem):
    return pltpu.CompilerParams(dimension_semantics=sem, vmem_limit_bytes=VMEM_LIMIT)


def _const_spec(shape, single_buffer=False):
    n = len(shape)
    if single_buffer:
        return pl.BlockSpec(shape, lambda *_: (0,) * n, pipeline_mode=pl.Buffered(1))
    return pl.BlockSpec(shape, lambda *_: (0,) * n)


def _rms(x, g):
    ms = jnp.mean(x * x, axis=-1, keepdims=True)
    return x * lax.rsqrt(ms + EPS) * g


def _dot(a, b):
    return jnp.dot(a, b, preferred_element_type=F32)


def _dot_nt(a, b, precision=None):
    return lax.dot_general(a, b, (((1,), (1,)), ((), ())), precision=precision,
                           preferred_element_type=F32)


def _dot_tn(a, b):
    return lax.dot_general(a, b, (((0,), (0,)), ((), ())), preferred_element_type=F32)


def _silu(x):
    return x * jax.nn.sigmoid(x)


def _softplus(x):
    return jnp.maximum(x, 0.0) + jnp.log1p(jnp.exp(-jnp.abs(x)))


def _seg_mean_matrix():
    r = lax.broadcasted_iota(jnp.int32, (LANES, LANES), 0) // NSA_HD
    c = lax.broadcasted_iota(jnp.int32, (LANES, LANES), 1) // NSA_HD
    return jnp.where(r == c, 1.0 / NSA_HD, 0.0).astype(F32)


def _headnorm_rope_tok(x, gain, bd, c, s1, s2):
    ms = jnp.dot(x * x, bd, precision=HI, preferred_element_type=F32)
    xn = x * lax.rsqrt(ms + EPS) * gain
    return xn * c + pltpu.roll(xn, LANES - ROT_HALF, 1) * s1 + pltpu.roll(xn, ROT_HALF, 1) * s2


def _headnorm_rope_ch(x, gain_col, cos_t, sin_t):
    ms = jnp.mean(x * x, axis=0, keepdims=True)
    xn = x * lax.rsqrt(ms + EPS) * gain_col
    x1 = xn[0:ROT_HALF]
    x2 = xn[ROT_HALF:2 * ROT_HALF]
    return x1 * cos_t - x2 * sin_t, x2 * cos_t + x1 * sin_t, xn[2 * ROT_HALF:]


def _rope_tables(pos):
    inv = ROPE_THETA ** (-jnp.arange(ROT_HALF, dtype=F32) / ROT_HALF)
    ang = pos.astype(F32)[:, None] * inv[None, :]
    cos, sin = jnp.cos(ang), jnp.sin(ang)
    n = pos.shape[0]
    ones = jnp.ones((n, NSA_HD - 2 * ROT_HALF), F32)
    zeros8 = jnp.zeros((n, ROT_HALF), F32)
    zeros = jnp.zeros((n, NSA_HD - 2 * ROT_HALF), F32)
    c = jnp.concatenate([cos, cos, ones], axis=1)
    s1 = jnp.concatenate([-sin, zeros8, zeros], axis=1)
    s2 = jnp.concatenate([zeros8, sin, zeros], axis=1)
    tile2 = lambda t: jnp.concatenate([t, t], axis=1)
    return tile2(c), tile2(s1), tile2(s2), cos.T, sin.T


MEM_SLOTS = 2 * MEM_HEADS


def _memkv_kernel(x_ref, g_ref, w_ref, kn_ref, o_ref, *, tm):
    hn = _rms(x_ref[...], g_ref[0]).astype(BF16)
    kv = _dot(hn, w_ref[0])
    for c in range(MEM_SLOTS):
        blk = kv[:, c * MEM_HD:(c + 1) * MEM_HD]
        if c < MEM_HEADS:
            blk = _rms(blk, kn_ref[0])
        o_ref[0, pl.ds(c, tm, stride=MEM_SLOTS), :] = blk


def _memkv(mem2d, g, w_bf, kn):
    m = mem2d.shape[0]
    depth = w_bf.shape[0]
    tm = min(512, m)
    n = w_bf.shape[2]
    return pl.pallas_call(
        functools.partial(_memkv_kernel, tm=tm),
        grid=(depth, m // tm),
        in_specs=[pl.BlockSpec((tm, D_MODEL), lambda l, i: (i, 0)),
                  pl.BlockSpec((1, 1, D_MODEL), lambda l, i: (l, 0, 0)),
                  pl.BlockSpec((1, D_MODEL, n), lambda l, i: (l, 0, 0)),
                  pl.BlockSpec((1, 1, MEM_HD), lambda l, i: (l, 0, 0))],
        out_specs=pl.BlockSpec((1, tm * MEM_SLOTS, MEM_HD), lambda l, i: (l, i, 0)),
        out_shape=jax.ShapeDtypeStruct((depth, m * MEM_SLOTS, MEM_HD), F32),
        compiler_params=_cparams(("parallel", "parallel")),
        name="memkv",
    )(mem2d, g, w_bf, kn)


def _mixer0_project(x_ref, g_ref, win_ref, proj, n_col_chunks):
    hn = _rms(x_ref[0], g_ref[...]).astype(BF16)
    cw_cols = -(-IN0_PAD // n_col_chunks // LANES) * LANES

    def chunk(c):
        sl = slice(c * cw_cols, min((c + 1) * cw_cols, IN0_PAD))
        proj[:, sl] = _dot(hn, win_ref[:, sl])

    return [functools.partial(chunk, c) for c in range(-(-IN0_PAD // cw_cols))]


def _mixer0_mix(proj, x_ref, scw_ref, cw_ref, cb_ref, dtb_ref, alog_ref, dsk_ref, ssdn_ref, wout_ref,
                y_ref, ca, cbuf, hst, ycat, tq, side_work):
    side_work = list(side_work)

    def side(n=1):
        for _ in range(min(n, len(side_work))):
            side_work.pop(0)()

    x = x_ref[0]
    d = D_MODEL
    side(2)
    ca[CARRY:CARRY + tq, :] = proj[:, 2 * d:3 * d] * proj[:, 0:d]
    conv_u = ca[pl.ds(CARRY - 2, tq), :] * scw_ref[0:1, :]
    for k in range(1, SCONV_W):
        conv_u = conv_u + ca[pl.ds(CARRY - 2 + k, tq), :] * scw_ref[k:k + 1, :]
    ycat[:, 0:d] = (proj[:, d:2 * d] * conv_u).astype(BF16)

    side(4)
    x0 = 3 * d + SSD_DIM
    cbuf[CARRY:CARRY + tq, :] = proj[:, x0:x0 + SSD_CONV_DIM]
    xbc = cbuf[pl.ds(CARRY - 3, tq), :] * cw_ref[0:1, :]
    for k in range(1, SSD_CONV_W):
        xbc = xbc + cbuf[pl.ds(CARRY - 3 + k, tq), :] * cw_ref[k:k + 1, :]
    xbc = _silu(xbc + cb_ref[...])

    dt = _softplus(proj[:, x0 + SSD_CONV_DIM:IN0_PAD] + dtb_ref[...])
    a_neg = -jnp.exp(alog_ref[...])
    row = lax.broadcasted_iota(jnp.int32, (tq, tq), 0)
    col = lax.broadcasted_iota(jnp.int32, (tq, tq), 1)
    causal = row >= col
    tril = jnp.where(causal, 1.0, 0.0).astype(F32)
    cum = jnp.dot(tril, dt * a_neg, precision=HI, preferred_element_type=F32)
    eye = jnp.where(lax.broadcasted_iota(jnp.int32, (LANES, LANES), 0)
                    == lax.broadcasted_iota(jnp.int32, (LANES, LANES), 1), 1.0, 0.0).astype(F32)
    cum_t = _dot_nt(eye, cum, precision=HI)
    cum_last = cum[tq - 1:tq, :]
    e_cum = jnp.exp(cum)
    e_last = jnp.exp(cum_last)
    e_end = jnp.exp(cum_last - cum)
    z = proj[:, 3 * d:3 * d + SSD_DIM]
    dsk = dsk_ref[...]

    for g in range(SSD_GROUPS):
        b_g = xbc[:, SSD_DIM + g * SSD_N:SSD_DIM + (g + 1) * SSD_N].astype(BF16)
        c_g = xbc[:, SSD_DIM + (SSD_GROUPS + g) * SSD_N:
                  SSD_DIM + (SSD_GROUPS + g + 1) * SSD_N].astype(BF16)
        cb_g = _dot_nt(c_g, b_g)
        y_parts = []
        for hh in range(SSD_HPG):
            h = g * SSD_HPG + hh
            side()
            x_h = xbc[:, h * SSD_HD:(h + 1) * SSD_HD]
            xdt = x_h * dt[:, h:h + 1]
            lmat = jnp.where(causal, jnp.exp(cum[:, h:h + 1] - cum_t[h:h + 1, :]), 0.0)
            y_h = _dot((cb_g * lmat).astype(BF16), xdt.astype(BF16))
            st = hst[h]
            y_h = y_h + e_cum[:, h:h + 1] * _dot_nt(c_g, st.astype(BF16))
            hst[h] = e_last[:, h:h + 1] * st + _dot_tn((xdt * e_end[:, h:h + 1]).astype(BF16), b_g)
            y_parts.append(y_h + dsk[:, h:h + 1] * x_h)
        yg = jnp.concatenate(y_parts, axis=1)
        gs = slice(g * SSD_HPG * SSD_HD, (g + 1) * SSD_HPG * SSD_HD)
        yg = yg * _silu(z[:, gs])
        yg = yg * lax.rsqrt(jnp.mean(yg * yg, axis=-1, keepdims=True) + EPS)
        ycat[:, d + gs.start:d + gs.stop] = (yg * ssdn_ref[:, gs]).astype(BF16)

    side()
    y_ref[0] = x + _dot(ycat[...], wout_ref[...])
    while side_work:
        side()

    ca[0:CARRY, :] = ca[tq:tq + CARRY, :]
    cbuf[0:CARRY, :] = cbuf[tq:tq + CARRY, :]


def _mixer0_kernel(x_ref, g_ref, win_ref, scw_ref, cw_ref, cb_ref, dtb_ref, alog_ref,
                   dsk_ref, ssdn_ref, wout_ref, sa0_ref, sb0_ref, h0_ref,
                   y_ref, sa_ref, sb_ref, hout_ref,
                   proj, ca, cbuf, hst, ycat, *, tq, n_col_chunks):
    i = pl.program_id(1)

    @pl.when(i == 0)
    def _():
        ca[0:CARRY, :] = sa0_ref[0]
        cbuf[0:CARRY, :] = sb0_ref[0]
        hst[...] = h0_ref[0]

    for chunk in _mixer0_project(x_ref, g_ref, win_ref, proj, n_col_chunks):
        chunk()
    _mixer0_mix(proj, x_ref, scw_ref, cw_ref, cb_ref, dtb_ref, alog_ref, dsk_ref, ssdn_ref,
                wout_ref, y_ref, ca, cbuf, hst, ycat, tq, ())

    @pl.when(i == pl.num_programs(1) - 1)
    def _():
        sa_ref[0] = ca[0:CARRY, :]
        sb_ref[0] = cbuf[0:CARRY, :]
        hout_ref[0] = hst[...]


def _mixer0(x, tq, g, w_in_bf, scw, cw, cb, dtb, alog, dsk, ssdn, w_out_bf, sa0, sb0, h0):
    bsz, t, d = x.shape
    n_col_chunks = 7
    kern = functools.partial(_mixer0_kernel, tq=tq, n_col_chunks=n_col_chunks)
    per_b = lambda b, i: (b, 0, 0)
    return pl.pallas_call(
        kern,
        grid=(bsz, t // tq),
        in_specs=[pl.BlockSpec((1, tq, d), lambda b, i: (b, i, 0)),
                  _const_spec((1, d)), _const_spec((d, IN0_PAD)),
                  _const_spec((SUBLANES, d)), _const_spec((SUBLANES, SSD_CONV_DIM)),
                  _const_spec((1, SSD_CONV_DIM)), _const_spec((1, LANES)), _const_spec((1, LANES)),
                  _const_spec((1, LANES)), _const_spec((1, SSD_DIM)),
                  _const_spec((d + SSD_DIM, d)),
                  pl.BlockSpec((1, CARRY, d), per_b),
                  pl.BlockSpec((1, CARRY, SSD_CONV_DIM), per_b),
                  pl.BlockSpec((1, SSD_HEADS, SSD_HD, SSD_N), lambda b, i: (b, 0, 0, 0))],
        out_specs=[pl.BlockSpec((1, tq, d), lambda b, i: (b, i, 0)),
                   pl.BlockSpec((1, CARRY, d), per_b),
                   pl.BlockSpec((1, CARRY, SSD_CONV_DIM), per_b),
                   pl.BlockSpec((1, SSD_HEADS, SSD_HD, SSD_N), lambda b, i: (b, 0, 0, 0))],
        out_shape=[jax.ShapeDtypeStruct((bsz, t, d), F32),
                   jax.ShapeDtypeStruct((bsz, CARRY, d), F32),
                   jax.ShapeDtypeStruct((bsz, CARRY, SSD_CONV_DIM), F32),
                   jax.ShapeDtypeStruct((bsz, SSD_HEADS, SSD_HD, SSD_N), F32)],
        scratch_shapes=[pltpu.VMEM((tq, IN0_PAD), F32),
                        pltpu.VMEM((CARRY + tq, d), F32),
                        pltpu.VMEM((CARRY + tq, SSD_CONV_DIM), F32),
                        pltpu.VMEM((SSD_HEADS, SSD_HD, SSD_N), F32),
                        pltpu.VMEM((tq, d + SSD_DIM), BF16)],
        compiler_params=_cparams(("parallel", "arbitrary")),
        name="mixer0",
    )(x, g, w_in_bf, scw, cw, cb, dtb, alog, dsk, ssdn, w_out_bf, sa0, sb0, h0)


def _memattn_kernel(x_ref, g_ref, wq_ref, qn_ref, kv_ref, wo_ref, o_ref, *, n_mem):
    x = x_ref[0]
    hn = _rms(x, g_ref[...]).astype(BF16)
    q = _dot(hn, wq_ref[...])
    outs = []
    for h in range(MEM_HEADS):
        sl = slice(h * MEM_HD, (h + 1) * MEM_HD)
        k_h = kv_ref[0, 0, pl.ds(h, n_mem, stride=MEM_SLOTS), :].astype(BF16)
        v_h = kv_ref[0, 0, pl.ds(MEM_HEADS + h, n_mem, stride=MEM_SLOTS), :].astype(BF16)
        qh = _rms(q[:, sl], qn_ref[...]).astype(BF16)
        s = _dot_nt(qh, k_h) * (MEM_HD ** -0.5)
        m = jnp.max(s, axis=-1, keepdims=True)
        e = jnp.exp(s - m)
        p = e / jnp.sum(e, axis=-1, keepdims=True)
        outs.append(_dot(p.astype(BF16), v_h))
    o = jnp.concatenate(outs, axis=1).astype(BF16)
    o_ref[0] = x + _dot(o, wo_ref[...])


def _memattn(x, tq, g, wq_bf, qn, kv4, layer, wo_bf):
    bsz, t, d = x.shape
    rows = kv4.shape[2]
    hw = MEM_HEADS * MEM_HD
    return pl.pallas_call(
        functools.partial(_memattn_kernel, n_mem=rows // MEM_SLOTS),
        grid=(bsz, t // tq),
        in_specs=[pl.BlockSpec((1, tq, d), lambda b, i: (b, i, 0)),
                  _const_spec((1, d)), _const_spec((d, hw)), _const_spec((1, MEM_HD)),
                  pl.BlockSpec((1, 1, rows, MEM_HD), lambda b, i: (layer, b, 0, 0)),
                  _const_spec((hw, d))],
        out_specs=pl.BlockSpec((1, tq, d), lambda b, i: (b, i, 0)),
        out_shape=jax.ShapeDtypeStruct((bsz, t, d), F32),
        compiler_params=_cparams(("parallel", "parallel")),
        name="memattn",
    )(x, g, wq_bf, qn, kv4, wo_bf)


FFN_CHUNKS = 11


def _ffn_kernel(x_ref, g_ref, wg_ref, wu_ref, wd_ref, o_ref):
    x = x_ref[...]
    hn = _rms(x, g_ref[...]).astype(BF16)
    cf = wg_ref.shape[1] // FFN_CHUNKS
    acc = x
    for c in range(FFN_CHUNKS):
        sl = slice(c * cf, (c + 1) * cf)
        a = _silu(_dot(hn, wg_ref[:, sl])) * _dot(hn, wu_ref[:, sl])
        acc = acc + _dot(a.astype(BF16), wd_ref[sl, :])
    o_ref[...] = acc


def _ffn(x2d, tm, g, wg_bf, wu_bf, wd_bf):
    m, d = x2d.shape
    dff = wg_bf.shape[1]
    assert dff % (FFN_CHUNKS * LANES) == 0
    return pl.pallas_call(
        _ffn_kernel,
        grid=(m // tm,),
        in_specs=[pl.BlockSpec((tm, d), lambda i: (i, 0)), _const_spec((1, d)),
                  _const_spec((d, dff), single_buffer=True),
                  _const_spec((d, dff), single_buffer=True),
                  _const_spec((dff, d), single_buffer=True)],
        out_specs=pl.BlockSpec((tm, d), lambda i: (i, 0)),
        out_shape=jax.ShapeDtypeStruct((m, d), F32),
        compiler_params=_cparams(("parallel",)),
        name="ffn",
    )(x2d, g, wg_bf, wu_bf, wd_bf)


def _proj_res_kernel(a_ref, w_ref, x_ref, o_ref):
    o_ref[...] = x_ref[...] + _dot(a_ref[...].astype(BF16), w_ref[...])


def _proj_res(a2d, w_bf, x2d):
    m, k = a2d.shape
    n = w_bf.shape[1]
    tm = min(256, m)
    return pl.pallas_call(
        _proj_res_kernel,
        grid=(m // tm,),
        in_specs=[pl.BlockSpec((tm, k), lambda i: (i, 0)), _const_spec((k, n)),
                  pl.BlockSpec((tm, n), lambda i: (i, 0))],
        out_specs=pl.BlockSpec((tm, n), lambda i: (i, 0)),
        out_shape=jax.ShapeDtypeStruct((m, n), F32),
        compiler_params=_cparams(("parallel",)),
        name="proj_res",
    )(a2d, w_bf, x2d)


def _nsa_proj_kernel(x_ref, g_ref, wq_ref, wg_ref, wkv_ref, bd_ref, qg_ref, c_ref, s1_ref, s2_ref,
                     kg_ref, cos_ref, sin_ref,
                     q_ref, gate_ref, rows_ref, win_ref, kvbf_ref):
    hn = _rms(x_ref[0], g_ref[...]).astype(BF16)
    q = _dot(hn, wq_ref[...])
    bd = bd_ref[...]
    scale = NSA_HD ** -0.5
    for c in range(NSA_HEADS * NSA_HD // LANES):
        sl = slice(c * LANES, (c + 1) * LANES)
        q_ref[0, :, sl] = scale * _headnorm_rope_tok(q[:, sl], qg_ref[...], bd, c_ref[...],
                                                     s1_ref[...], s2_ref[...])
    gate_ref[0] = jax.nn.sigmoid(_dot(hn, wg_ref[...]))
    kv_t = _dot_nt(wkv_ref[...], hn)
    cos_t, sin_t = cos_ref[...], sin_ref[...]

    def put_normed(src_row, gain_idx, dst_ref, dst_row, bf_row):
        for g in range(NSA_KV):
            r1, r2, rest = _headnorm_rope_ch(kv_t[src_row + g * NSA_HD:src_row + (g + 1) * NSA_HD],
                                             kg_ref[gain_idx], cos_t, sin_t)
            full = jnp.concatenate([r1, r2, rest], axis=0)
            dst_ref[0, dst_row + g * NSA_HD:dst_row + (g + 1) * NSA_HD, :] = full
            kvbf_ref[0, bf_row + g * NSA_HD:bf_row + (g + 1) * NSA_HD, :] = full.astype(BF16)

    w = KV_W
    rows_ref[0, 0:2 * w, :] = kv_t[0:2 * w]
    put_normed(2 * w, 0, rows_ref, 2 * w, 0)
    rows_ref[0, 3 * w:4 * w, :] = kv_t[3 * w:4 * w]
    kvbf_ref[0, w:2 * w, :] = kv_t[3 * w:4 * w].astype(BF16)
    put_normed(4 * w, 1, win_ref, 0, 2 * w)
    win_ref[0, w:2 * w, :] = kv_t[5 * w:6 * w]
    kvbf_ref[0, 3 * w:4 * w, :] = kv_t[5 * w:6 * w].astype(BF16)


def _nsa_proj(x, tm, g, wq_bf, wg_bf, wkv_t_bf, qgain2, kgain, tabs):
    bsz, t, d = x.shape
    c, s1, s2, cos_t, sin_t = tabs
    hq = NSA_HEADS * NSA_HD
    w = KV_W
    tok = lambda b, i: (b, i, 0)
    chan = lambda b, i: (b, 0, i)
    return pl.pallas_call(
        _nsa_proj_kernel,
        grid=(bsz, t // tm),
        in_specs=[pl.BlockSpec((1, tm, d), tok), _const_spec((1, d)),
                  _const_spec((d, hq)), _const_spec((d, LANES)), _const_spec((6 * w, d)),
                  _const_spec((LANES, LANES)), _const_spec((1, LANES)),
                  pl.BlockSpec((tm, LANES), lambda b, i: (i, 0)),
                  pl.BlockSpec((tm, LANES), lambda b, i: (i, 0)),
                  pl.BlockSpec((tm, LANES), lambda b, i: (i, 0)),
                  _const_spec((2, NSA_HD, 1)),
                  pl.BlockSpec((ROT_HALF, tm), lambda b, i: (0, i)),
                  pl.BlockSpec((ROT_HALF, tm), lambda b, i: (0, i))],
        out_specs=[pl.BlockSpec((1, tm, hq), tok), pl.BlockSpec((1, tm, LANES), tok),
                   pl.BlockSpec((1, 4 * w, tm), chan), pl.BlockSpec((1, 2 * w, tm), chan),
                   pl.BlockSpec((1, 4 * w, tm), chan)],
        out_shape=[jax.ShapeDtypeStruct((bsz, t, hq), F32),
                   jax.ShapeDtypeStruct((bsz, t, LANES), F32),
                   jax.ShapeDtypeStruct((bsz, 4 * w, t), F32),
                   jax.ShapeDtypeStruct((bsz, 2 * w, t), F32),
                   jax.ShapeDtypeStruct((bsz, 4 * w, t), BF16)],
        compiler_params=_cparams(("parallel", "parallel")),
        name="nsa_proj",
    )(x, g, wq_bf, wg_bf, wkv_t_bf, _seg_mean_matrix(), qgain2, c, s1, s2, kgain, cos_t, sin_t)


def _half_sums(xt, pe_ref, w1_ref, nh):
    out = []
    tiles = KV_W // LANES
    for kind in range(2):
        rows = [jnp.concatenate([xt[kind * tiles + c, pl.ds(j, nh, stride=CMP_STRIDE), :]
                                 for c in range(tiles)], axis=1) for j in range(CMP_STRIDE)]
        for half in range(2):
            lhs = jnp.concatenate(
                [(rows[j] + pe_ref[kind, half * CMP_STRIDE + j:half * CMP_STRIDE + j + 1, :]).astype(BF16)
                 for j in range(CMP_STRIDE)], axis=1)
            out.append(_dot(lhs, w1_ref[kind, half]))
    return out


def _finish_compress(a_k, b_k, a_v, b_v, w2_ref, kg_ref, bd, c, s1, s2):
    nh = a_k.shape[0]
    kc = _dot(_silu(a_k + pltpu.roll(b_k, nh - 1, 0)).astype(BF16), w2_ref[0])
    vc = _dot(_silu(a_v + pltpu.roll(b_v, nh - 1, 0)).astype(BF16), w2_ref[1])
    kc = jnp.concatenate([_headnorm_rope_tok(kc[:, h * LANES:(h + 1) * LANES], kg_ref[...], bd, c, s1, s2)
                          for h in range(KV_W // LANES)], axis=1)
    return kc, vc


def _p_compress_kernel(rows_ref, pe_ref, w1_ref, w2_ref, kg_ref, bd_ref, c_ref, s1_ref, s2_ref,
                       kc_ref, vc_ref, xt, *, nh):
    for c in range(2 * KV_W // LANES):
        xt[c] = rows_ref[0, c * LANES:(c + 1) * LANES, :].T
    a_k, b_k, a_v, b_v = _half_sums(xt, pe_ref, w1_ref, nh)
    kc, vc = _finish_compress(a_k, b_k, a_v, b_v, w2_ref, kg_ref, bd_ref[...], c_ref[...],
                              s1_ref[...], s2_ref[...])
    kc_ref[0] = kc.astype(BF16)
    vc_ref[0] = vc.astype(BF16)


def _p_compress(rows_t, pe4, w1bd, w2bd, kgain2, ctabs):
    bsz, _, t = rows_t.shape
    nh = t // CMP_STRIDE
    c, s1, s2 = ctabs
    w = KV_W
    return pl.pallas_call(
        functools.partial(_p_compress_kernel, nh=nh),
        grid=(bsz,),
        in_specs=[pl.BlockSpec((1, 2 * w, t), lambda b: (b, 0, 0)),
                  _const_spec((2, CMP_BLOCK, w)), _const_spec((2, 2, CMP_STRIDE * w, w)),
                  _const_spec((2, w, w)), _const_spec((1, LANES)), _const_spec((LANES, LANES)),
                  _const_spec((nh, LANES)), _const_spec((nh, LANES)), _const_spec((nh, LANES))],
        out_specs=[pl.BlockSpec((1, nh, w), lambda b: (b, 0, 0)),
                   pl.BlockSpec((1, nh, w), lambda b: (b, 0, 0))],
        out_shape=[jax.ShapeDtypeStruct((bsz, nh, w), BF16), jax.ShapeDtypeStruct((bsz, nh, w), BF16)],
        scratch_shapes=[pltpu.VMEM((2 * w // LANES, t, LANES), F32)],
        compiler_params=_cparams(("parallel",)),
        name="p_compress",
    )(rows_t, pe4, w1bd, w2bd, kgain2, _seg_mean_matrix(), c, s1, s2)


def _stack_heads(q, g):
    return jnp.concatenate([q[:, (g * NSA_HPG + hh) * NSA_HD:(g * NSA_HPG + hh + 1) * NSA_HD]
                            for hh in range(NSA_HPG)], axis=0)


def _stack_gate(gates, g, j):
    return jnp.concatenate([gates[:, (g * NSA_HPG + hh) * 3 + j:(g * NSA_HPG + hh) * 3 + j + 1]
                            for hh in range(NSA_HPG)], axis=0)


def _tile_rows(m, n):
    return jnp.concatenate([m] * n, axis=0)


def _topk_mask(imp, k):
    r, s = imp.shape
    idx = lax.broadcasted_iota(jnp.int32, (r, s), 1)
    rank = jnp.zeros((r, s), F32)
    for s2 in range(s):
        col = imp[:, s2:s2 + 1]
        beats = jnp.where(col > imp, 1.0, jnp.where((col == imp) & (idx > s2), 1.0, 0.0))
        rank = rank + beats
    return rank < k


def _topk_bias_t(imp_t, k):
    s, r = imp_t.shape
    idx = lax.broadcasted_iota(jnp.int32, (s, r), 0)
    rank = jnp.zeros((s, r), F32)
    for s2 in range(s):
        row = imp_t[s2:s2 + 1, :]
        rank = rank + jnp.where(row > imp_t, 1.0, jnp.where((row == imp_t) & (idx > s2), 1.0, 0.0))
    return jnp.where((rank < k) & (imp_t > 0.5 * NEG), 0.0, NEG)


SEL_COLS = 32
DEN_COL = NSA_HD + SEL_COLS


def _p_attn_kernel(q_ref, gate_ref, kc_ref, vc_ref, kvs_ref, *refs, tq, t_all, kch, n_wb):
    win_refs = refs[:n_wb]
    (movt_ref, epad_ref, spread_ref, wo_ref, x_ref, y_ref,
     o_scr, qa_scr, qbd_scr, oc_scr, os_scr) = refs[n_wb:]
    i = pl.program_id(1)
    q = q_ref[0]
    gates = gate_ref[0]
    nh = kc_ref.shape[1]
    n_slc = movt_ref.shape[0]
    pos_q = i * tq + lax.broadcasted_iota(jnp.int32, (tq, 1), 0)
    pos_q_row = i * tq + lax.broadcasted_iota(jnp.int32, (1, tq), 1)
    pos_c = lax.broadcasted_iota(jnp.int32, (tq, nh), 1) * CMP_STRIDE + (CMP_BLOCK - 1)
    ok_c = _tile_rows(pos_c <= pos_q, NSA_HEADS)
    blk_t = lax.broadcasted_iota(jnp.int32, (n_slc, NSA_KV * tq), 0)
    cur_t = jnp.concatenate([pos_q_row // SLC_BLOCK] * NSA_KV, axis=1)
    w = KV_W
    rows_g = NSA_HPG * tq

    q_heads = [_stack_heads(q, g) for g in range(NSA_KV)]
    def in_group_columns(qg, g):
        left = [jnp.zeros((rows_g, g * NSA_HD), F32)] if g else []
        right = [jnp.zeros((rows_g, (NSA_KV - 1 - g) * NSA_HD), F32)] if g < NSA_KV - 1 else []
        return jnp.concatenate(left + [qg] + right, axis=1)

    qbd_scr[...] = jnp.concatenate([in_group_columns(q_heads[g], g) for g in range(NSA_KV)],
                                   axis=0).astype(BF16)

    s = jnp.where(ok_c, _dot_nt(qbd_scr[...], kc_ref[0]), NEG)
    e = jnp.exp(s - jnp.max(s, axis=-1, keepdims=True))
    p = jnp.where(ok_c, e / jnp.sum(e, axis=-1, keepdims=True), 0.0)
    oc_scr[...] = _dot(p.astype(BF16), vc_ref[0])
    psum = []
    for g in range(NSA_KV):
        acc = p[g * rows_g:g * rows_g + tq]
        for hh in range(1, NSA_HPG):
            acc = acc + p[g * rows_g + hh * tq:g * rows_g + (hh + 1) * tq]
        psum.append(acc)
    imp_t = _dot_nt(movt_ref[...], jnp.concatenate(psum, axis=0), precision=HI)
    imp_t = jnp.where((blk_t == cur_t) | (blk_t == 0), BIG, imp_t)
    imp_t = jnp.where(blk_t > cur_t, NEG, imp_t)
    selb_t = _topk_bias_t(imp_t, SLC_TOP).astype(BF16)
    selb = _dot_nt(spread_ref[...], selb_t)
    pad = jnp.zeros((rows_g, LANES - NSA_HD - n_slc), F32)
    for g in range(NSA_KV):
        qa_scr[g] = jnp.concatenate([q_heads[g], selb[g * rows_g:(g + 1) * rows_g], pad],
                                    axis=1).astype(BF16)

    n_var = t_all // kch
    need = (i * tq) // kch + 1
    for nv in range(1, n_var + 1):
        @pl.when(need == nv)
        def _(nv=nv):
            klen = nv * kch
            head = klen - kch
            tail_pos = head + lax.broadcasted_iota(jnp.int32, (tq, kch), 1)
            cb4 = _tile_rows(jnp.where(tail_pos <= pos_q, 0.0, NEG), NSA_HPG)
            for g in range(NSA_KV):
                qa = qa_scr[g]
                ks = slice(g * NSA_HD, (g + 1) * NSA_HD)
                vs = slice(w + g * NSA_HD, w + (g + 1) * NSA_HD)
                k_tail = jnp.concatenate([kvs_ref[0, ks, head:klen], epad_ref[:, head:klen]], axis=0)
                s_t = _dot(qa, k_tail) + cb4
                m = jnp.max(s_t, axis=-1, keepdims=True)
                if head:
                    k_head = jnp.concatenate([kvs_ref[0, ks, 0:head], epad_ref[:, 0:head]], axis=0)
                    s_h = _dot(qa, k_head)
                    m = jnp.maximum(m, jnp.max(s_h, axis=-1, keepdims=True))
                v_tail = jnp.concatenate([kvs_ref[0, vs, head:klen], epad_ref[:, head:klen]], axis=0)
                o = _dot_nt(jnp.exp(s_t - m).astype(BF16), v_tail)
                if head:
                    v_head = jnp.concatenate([kvs_ref[0, vs, 0:head], epad_ref[:, 0:head]], axis=0)
                    o = o + _dot_nt(jnp.exp(s_h - m).astype(BF16), v_head)
                os_scr[g] = o[:, 0:NSA_HD] / o[:, DEN_COL:DEN_COL + 1]

    n_wb = len(win_refs)
    win_blocks = win_refs[::-1]
    wpos = (i - (n_wb - 1)) * tq + lax.broadcasted_iota(jnp.int32, (tq, n_wb * tq), 1)
    dpos = pos_q - wpos
    wb = _tile_rows(jnp.where((dpos >= 0) & (dpos < WINDOW) & (wpos >= 0), 0.0, NEG), NSA_HEADS)
    k_w = jnp.concatenate([r[0, 0:w, :] for r in win_blocks], axis=1)
    v_w = jnp.concatenate([r[0, w:2 * w, :] for r in win_blocks], axis=1)
    s = _dot(qbd_scr[...], k_w) + wb
    e = jnp.exp(s - jnp.max(s, axis=-1, keepdims=True))
    o_w = _dot_nt(e.astype(BF16), v_w) / jnp.sum(e, axis=-1, keepdims=True)
    for h in range(NSA_HEADS):
        g, hh = divmod(h, NSA_HPG)
        rs = slice(h * tq, (h + 1) * tq)
        cs = slice(g * NSA_HD, (g + 1) * NSA_HD)
        o_h = (gates[:, 3 * h:3 * h + 1] * oc_scr[rs, cs]
               + gates[:, 3 * h + 1:3 * h + 2] * os_scr[g, hh * tq:(hh + 1) * tq, :]
               + gates[:, 3 * h + 2:3 * h + 3] * o_w[rs, cs])
        o_scr[:, h * NSA_HD:(h + 1) * NSA_HD] = o_h.astype(BF16)

    y_ref[0] = x_ref[0] + _dot(o_scr[...], wo_ref[...])


def _p_attn(q, gates, kc, vc, kvbf, mov_t, epad, wo_bf, x, tq):
    bsz, t, hq = q.shape
    nh = kc.shape[1]
    w = KV_W
    n_slc = mov_t.shape[0]
    kch = min(512, t)
    rows = NSA_HPG * tq
    n_wb = -(-WINDOW // tq) + 1
    assert kch % tq == 0 and n_wb * tq <= t
    r_idx = jnp.arange(NSA_HEADS * tq)
    c_idx = jnp.arange(NSA_KV * tq)
    spread = ((r_idx[:, None] // rows == c_idx[None, :] // tq)
              & (r_idx[:, None] % tq == c_idx[None, :] % tq)).astype(BF16)
    tok = lambda b, i: (b, i, 0)
    win_specs = [pl.BlockSpec((1, 2 * w, tq), functools.partial(
        lambda b, i, k: (b, 1, jnp.maximum(i - k, 0)), k=k)) for k in range(n_wb)]
    return pl.pallas_call(
        functools.partial(_p_attn_kernel, tq=tq, t_all=t, kch=kch, n_wb=n_wb),
        grid=(bsz, t // tq),
        in_specs=[pl.BlockSpec((1, tq, hq), tok), pl.BlockSpec((1, tq, LANES), tok),
                  pl.BlockSpec((1, nh, w), lambda b, i: (b, 0, 0)),
                  pl.BlockSpec((1, nh, w), lambda b, i: (b, 0, 0)),
                  pl.BlockSpec((1, 2 * w, t), lambda b, i: (b, 0, 0))] + win_specs + [
                  _const_spec((n_slc, nh)), _const_spec((LANES - NSA_HD, t)),
                  _const_spec((NSA_HEADS * tq, NSA_KV * tq)), _const_spec((hq, D_MODEL)),
                  pl.BlockSpec((1, tq, D_MODEL), tok)],
        out_specs=pl.BlockSpec((1, tq, D_MODEL), tok),
        out_shape=jax.ShapeDtypeStruct((bsz, t, D_MODEL), F32),
        scratch_shapes=[pltpu.VMEM((tq, hq), BF16), pltpu.VMEM((NSA_KV, rows, LANES), BF16),
                        pltpu.VMEM((NSA_HEADS * tq, w), BF16), pltpu.VMEM((NSA_HEADS * tq, w), F32),
                        pltpu.VMEM((NSA_KV, rows, NSA_HD), F32)],
        compiler_params=_cparams(("parallel", "parallel")),
        name="p_attn",
    )(q, gates, kc, vc, kvbf, *([kvbf] * n_wb), mov_t, epad, spread, wo_bf, x)


MAX_PAGES_COMPRESS = 32
MAX_PAGES_SLC = 32


def _s_compress_kernel(pt_ref, *refs, nh, page, pps):
    pages = refs[:pps]
    pe_ref, w1_ref, a_ref, b_ref, xt = refs[pps:]
    for k, pg in enumerate(pages):
        for c in range(2 * KV_W // LANES):
            xt[c, k * page:(k + 1) * page, :] = pg[0, c * LANES:(c + 1) * LANES, :].T
    a_k, b_k, a_v, b_v = _half_sums(xt, pe_ref, w1_ref, nh)
    a_ref[0] = jnp.concatenate([a_k, a_v], axis=1)
    b_ref[0] = jnp.concatenate([b_k, b_v], axis=1)


def _s_compress(cache_t, page_table, pe4, w1bd):
    bsz, n_pages = page_table.shape
    page = cache_t.shape[2]
    w = KV_W
    pps = min(MAX_PAGES_COMPRESS, n_pages)
    steps = n_pages // pps
    nh = pps * page // CMP_STRIDE
    page_specs = [pl.BlockSpec((1, 2 * w, page), functools.partial(
        lambda b, s, pt, k: (pt[b, s * pps + k], 0, 0), k=k)) for k in range(pps)]
    grid_spec = pltpu.PrefetchScalarGridSpec(
        num_scalar_prefetch=1,
        grid=(bsz, steps),
        in_specs=page_specs + [pl.BlockSpec((2, CMP_BLOCK, w), lambda b, s, pt: (0, 0, 0)),
                               pl.BlockSpec((2, 2, CMP_STRIDE * w, w), lambda b, s, pt: (0, 0, 0, 0))],
        out_specs=[pl.BlockSpec((1, nh, 2 * w), lambda b, s, pt: (b, s, 0)),
                   pl.BlockSpec((1, nh, 2 * w), lambda b, s, pt: (b, s, 0))],
        scratch_shapes=[pltpu.VMEM((2 * w // LANES, pps * page, LANES), F32)])
    out = jax.ShapeDtypeStruct((bsz, steps * nh, 2 * w), F32)
    return pl.pallas_call(
        functools.partial(_s_compress_kernel, nh=nh, page=page, pps=pps),
        grid_spec=grid_spec,
        out_shape=[out, out],
        compiler_params=_cparams(("parallel", "arbitrary")),
        name="s_compress",
    )(page_table, *([cache_t] * pps), pe4, w1bd)


def _s_cmp_kernel(a_ref, b_ref, q_ref, w2_ref, kg_ref, bd_ref, c_ref, s1_ref, s2_ref, mov_ref,
                  oc_ref, sel_ref, *, ts, past):
    ab = a_ref[0]
    bb = b_ref[0]
    w = KV_W
    kc, vc = _finish_compress(ab[:, 0:w], bb[:, 0:w], ab[:, w:2 * w], bb[:, w:2 * w], w2_ref, kg_ref,
                              bd_ref[...], c_ref[...], s1_ref[...], s2_ref[...])
    kc = kc.astype(BF16)
    vc = vc.astype(BF16)
    nh = kc.shape[0]
    q = q_ref[0]
    pos_q = past + lax.broadcasted_iota(jnp.int32, (ts, 1), 0)
    pos_c = lax.broadcasted_iota(jnp.int32, (ts, nh), 1) * CMP_STRIDE + (CMP_BLOCK - 1)
    ok_c = _tile_rows(pos_c <= pos_q, NSA_HPG)
    n_blk = mov_ref.shape[1]
    blk = lax.broadcasted_iota(jnp.int32, (ts, n_blk), 1)
    for g in range(NSA_KV):
        qg = _stack_heads(q, g).astype(BF16)
        gs = slice(g * NSA_HD, (g + 1) * NSA_HD)
        s = jnp.where(ok_c, _dot_nt(qg, kc[:, gs]), NEG)
        e = jnp.exp(s - jnp.max(s, axis=-1, keepdims=True))
        p = jnp.where(ok_c, e / jnp.sum(e, axis=-1, keepdims=True), 0.0)
        o_c = _dot(p.astype(BF16), vc[:, gs])
        for hh in range(NSA_HPG):
            h = g * NSA_HPG + hh
            oc_ref[0, :, h * NSA_HD:(h + 1) * NSA_HD] = o_c[hh * ts:(hh + 1) * ts]
        psum = p[0:ts]
        for hh in range(1, NSA_HPG):
            psum = psum + p[hh * ts:(hh + 1) * ts]
        imp = jnp.dot(psum, mov_ref[...], precision=HI, preferred_element_type=F32)
        imp = jnp.where(blk == 0, BIG, imp)
        sel_ref[0, g * ts:(g + 1) * ts, :] = jnp.where(_topk_mask(imp, SLC_TOP - 1), 1.0, 0.0)


def _s_cmp(a, b, q, w2bd, kgain2, ctabs, mov, past):
    bsz, nh, _ = a.shape
    ts = q.shape[1]
    hq = q.shape[2]
    c, s1, s2 = ctabs
    w = KV_W
    n_blk = mov.shape[1]
    per_b = lambda i: (i, 0, 0)
    return pl.pallas_call(
        functools.partial(_s_cmp_kernel, ts=ts, past=past),
        grid=(bsz,),
        in_specs=[pl.BlockSpec((1, nh, 2 * w), per_b), pl.BlockSpec((1, nh, 2 * w), per_b),
                  pl.BlockSpec((1, ts, hq), per_b),
                  _const_spec((2, w, w)), _const_spec((1, LANES)), _const_spec((LANES, LANES)),
                  _const_spec((nh, LANES)), _const_spec((nh, LANES)), _const_spec((nh, LANES)),
                  _const_spec((nh, n_blk))],
        out_specs=[pl.BlockSpec((1, ts, hq), per_b), pl.BlockSpec((1, NSA_KV * ts, n_blk), per_b)],
        out_shape=[jax.ShapeDtypeStruct((bsz, ts, hq), F32),
                   jax.ShapeDtypeStruct((bsz, NSA_KV * ts, n_blk), F32)],
        compiler_params=_cparams(("parallel",)),
        name="s_cmp",
    )(a, b, q, w2bd, kgain2, _seg_mean_matrix(), c, s1, s2, mov)


def _s_slc_kernel(pt_ref, *refs, ts, pps):
    pages = refs[:pps]
    q_ref, sel_ref, new_ref, exp_ref, o_ref, m_scr, l_scr, acc = refs[pps:]
    s_id = pl.program_id(1)
    w = KV_W
    rows = NSA_HPG * ts

    @pl.when(s_id == 0)
    def _():
        m_scr[...] = jnp.full(m_scr.shape, NEG, F32)
        l_scr[...] = jnp.zeros(l_scr.shape, F32)
        acc[...] = jnp.zeros(acc.shape, F32)

    q = q_ref[0]
    sel = sel_ref[0, 0]
    for g in range(NSA_KV):
        qg = _stack_heads(q, g).astype(BF16)
        k_t = jnp.concatenate([pg[0, g * NSA_HD:(g + 1) * NSA_HD, :] for pg in pages], axis=1).astype(BF16)
        v_t = jnp.concatenate([pg[0, w + g * NSA_HD:w + (g + 1) * NSA_HD, :] for pg in pages],
                              axis=1).astype(BF16)
        selk = _dot(sel[g * ts:(g + 1) * ts].astype(BF16), exp_ref[...])
        ok = _tile_rows(selk > 0.5, NSA_HPG)
        s = jnp.where(ok, _dot(qg, k_t), NEG)
        m_old = m_scr[g]
        m_new = jnp.maximum(m_old, jnp.max(s, axis=-1, keepdims=True))
        alpha = jnp.exp(m_old - m_new)
        e = jnp.where(ok, jnp.exp(s - m_new), 0.0)
        l_scr[g] = alpha * l_scr[g] + jnp.sum(e, axis=-1, keepdims=True)
        acc[g] = alpha * acc[g] + _dot_nt(e.astype(BF16), v_t)
        m_scr[g] = m_new

    @pl.when(s_id == pl.num_programs(1) - 1)
    def _():
        t_row = lax.broadcasted_iota(jnp.int32, (ts, ts), 0)
        t_col = lax.broadcasted_iota(jnp.int32, (ts, ts), 1)
        ok_n = _tile_rows(t_col <= t_row, NSA_HPG)
        for g in range(NSA_KV):
            qg = _stack_heads(q, g).astype(BF16)
            k_n = new_ref[0, g * NSA_HD:(g + 1) * NSA_HD, :].astype(BF16)
            v_n = new_ref[0, w + g * NSA_HD:w + (g + 1) * NSA_HD, :].astype(BF16)
            s = jnp.where(ok_n, _dot(qg, k_n), NEG)
            m_old = m_scr[g]
            m_new = jnp.maximum(m_old, jnp.max(s, axis=-1, keepdims=True))
            alpha = jnp.exp(m_old - m_new)
            e = jnp.where(ok_n, jnp.exp(s - m_new), 0.0)
            l_fin = alpha * l_scr[g] + jnp.sum(e, axis=-1, keepdims=True)
            o = (alpha * acc[g] + _dot_nt(e.astype(BF16), v_n)) / l_fin
            for hh in range(NSA_HPG):
                h = g * NSA_HPG + hh
                o_ref[0, :, h * NSA_HD:(h + 1) * NSA_HD] = o[hh * ts:(hh + 1) * ts]


def _s_slc(cache_t, page_table, q, sel_steps, new_t, expand, pps):
    bsz, n_pages = page_table.shape
    page = cache_t.shape[2]
    ts, hq = q.shape[1], q.shape[2]
    w = KV_W
    steps = n_pages // pps
    bps = sel_steps.shape[3]
    page_specs = [pl.BlockSpec((1, 2 * w, page), functools.partial(
        lambda b, s, pt, k: (pt[b, s * pps + k], 1, 0), k=k)) for k in range(pps)]
    rows = NSA_HPG * ts
    grid_spec = pltpu.PrefetchScalarGridSpec(
        num_scalar_prefetch=1,
        grid=(bsz, steps),
        in_specs=page_specs + [
            pl.BlockSpec((1, ts, hq), lambda b, s, pt: (b, 0, 0)),
            pl.BlockSpec((1, 1, NSA_KV * ts, bps), lambda b, s, pt: (b, s, 0, 0)),
            pl.BlockSpec((1, 2 * w, ts), lambda b, s, pt: (b, 0, 0)),
            pl.BlockSpec((bps, pps * page), lambda b, s, pt: (0, 0))],
        out_specs=pl.BlockSpec((1, ts, hq), lambda b, s, pt: (b, 0, 0)),
        scratch_shapes=[pltpu.VMEM((NSA_KV, rows, 1), F32), pltpu.VMEM((NSA_KV, rows, 1), F32),
                        pltpu.VMEM((NSA_KV, rows, NSA_HD), F32)])
    return pl.pallas_call(
        functools.partial(_s_slc_kernel, ts=ts, pps=pps),
        grid_spec=grid_spec,
        out_shape=jax.ShapeDtypeStruct((bsz, ts, hq), F32),
        compiler_params=_cparams(("parallel", "arbitrary")),
        name="s_slc",
    )(page_table, *([cache_t] * pps), q, sel_steps, new_t, expand)


def _s_win_kernel(q_ref, gate_ref, oc_ref, os_ref, cache_ref, new_ref, o_ref, wout_ref, *, ts):
    q = q_ref[0]
    gates = gate_ref[0]
    cache = cache_ref[0]
    new = new_ref[0]
    wb = cache.shape[1]
    w = KV_W
    t_row = lax.broadcasted_iota(jnp.int32, (ts, wb), 0)
    idx = lax.broadcasted_iota(jnp.int32, (ts, wb), 1)
    dpos = t_row + wb - idx
    ok_old = _tile_rows((dpos >= 0) & (dpos < WINDOW), NSA_HPG)
    n_row = lax.broadcasted_iota(jnp.int32, (ts, ts), 0)
    n_col = lax.broadcasted_iota(jnp.int32, (ts, ts), 1)
    ok_new = _tile_rows((n_col <= n_row) & (n_row - n_col < WINDOW), NSA_HPG)
    for g in range(NSA_KV):
        qg = _stack_heads(q, g).astype(BF16)
        gk = slice(g * NSA_HD, (g + 1) * NSA_HD)
        gv = slice(w + g * NSA_HD, w + (g + 1) * NSA_HD)
        s_old = jnp.where(ok_old, _dot(qg, cache[gk].astype(BF16)), NEG)
        s_new = jnp.where(ok_new, _dot(qg, new[gk].astype(BF16)), NEG)
        m = jnp.maximum(jnp.max(s_old, axis=-1, keepdims=True), jnp.max(s_new, axis=-1, keepdims=True))
        e_old = jnp.exp(s_old - m)
        e_new = jnp.exp(s_new - m)
        den = jnp.sum(e_old, axis=-1, keepdims=True) + jnp.sum(e_new, axis=-1, keepdims=True)
        o_w = (_dot_nt(e_old.astype(BF16), cache[gv].astype(BF16))
               + _dot_nt(e_new.astype(BF16), new[gv].astype(BF16))) / den
        for hh in range(NSA_HPG):
            h = g * NSA_HPG + hh
            hs = slice(h * NSA_HD, (h + 1) * NSA_HD)
            o_ref[0, :, hs] = (gates[:, 3 * h:3 * h + 1] * oc_ref[0, :, hs]
                               + gates[:, 3 * h + 1:3 * h + 2] * os_ref[0, :, hs]
                               + gates[:, 3 * h + 2:3 * h + 3] * o_w[hh * ts:(hh + 1) * ts])
    wout_ref[0] = pltpu.roll(cache, wb - ts, 1)
    wout_ref[0, :, wb - ts:wb] = new


def _s_win(q, gates, o_c, o_s, cache_win_t, new_win_t):
    bsz, ts, hq = q.shape
    wb = cache_win_t.shape[2]
    w = KV_W
    per_b = lambda b: (b, 0, 0)
    return pl.pallas_call(
        functools.partial(_s_win_kernel, ts=ts),
        grid=(bsz,),
        in_specs=[pl.BlockSpec((1, ts, hq), per_b), pl.BlockSpec((1, ts, LANES), per_b),
                  pl.BlockSpec((1, ts, hq), per_b), pl.BlockSpec((1, ts, hq), per_b),
                  pl.BlockSpec((1, 2 * w, wb), per_b), pl.BlockSpec((1, 2 * w, ts), per_b)],
        out_specs=[pl.BlockSpec((1, ts, hq), per_b), pl.BlockSpec((1, 2 * w, wb), per_b)],
        out_shape=[jax.ShapeDtypeStruct((bsz, ts, hq), F32),
                   jax.ShapeDtypeStruct((bsz, 2 * w, wb), F32)],
        compiler_params=_cparams(("parallel",)),
        name="s_win",
    )(q, gates, o_c, o_s, cache_win_t, new_win_t)


def _block_overlap(n_cmp_rows, n_cmp, n_slc):
    cs = jnp.arange(n_cmp_rows)[:, None] * CMP_STRIDE
    ss = jnp.arange(n_slc)[None, :] * SLC_BLOCK
    ov = jnp.minimum(cs + CMP_BLOCK, ss + SLC_BLOCK) - jnp.maximum(cs, ss)
    ov = jnp.clip(ov, 0, None).astype(F32) / CMP_BLOCK
    return jnp.where(jnp.arange(n_cmp_rows)[:, None] < n_cmp, ov, 0.0)


def _expand_matrix(n_blk, n_keys):
    return (jnp.arange(n_keys)[None, :] // SLC_BLOCK == jnp.arange(n_blk)[:, None]).astype(BF16)


def _pad_rows(a, rows):
    return jnp.pad(a, ((0, rows - a.shape[0]), (0, 0)))


def _pad_lanes(a, lanes=LANES):
    return jnp.pad(a, ((0, 0), (0, lanes - a.shape[1])))


def _blockdiag4(m):
    eye = jnp.eye(NSA_KV, dtype=m.dtype)
    out = jnp.einsum('gh,...de->...gdhe', eye, m)
    return out.reshape(m.shape[:-2] + (KV_W, KV_W))


def _layer_tail(x, layer, kv4, tq_mem, tm_ffn, p):
    bsz, t, d = x.shape
    x = _memattn(x, tq_mem, p['norm_mem'][layer][None], p['mem_w_q'][layer].astype(BF16),
                 p['mem_q_norm'][layer][None], kv4, layer, p['mem_w_o'][layer].astype(BF16))
    x2 = _ffn(x.reshape(bsz * t, d), tm_ffn, p['norm_ffn'][layer][None],
              p['ffn_w_gate'][layer].astype(BF16), p['ffn_w_up'][layer].astype(BF16),
              p['ffn_w_down'][layer].astype(BF16))
    return x2.reshape(bsz, t, d)


def _run_mixer0(x, tq, sconv_st, ssdc_st, ssd_st, p):
    w_in = _pad_lanes(p['ab_w_in'][0], IN0_PAD).astype(BF16)
    front = lambda s: jnp.pad(s, ((0, 0), (CARRY - s.shape[1], 0), (0, 0)))
    y, sa, sb, hn = _mixer0(
        x, tq, p['norm_mix'][0][None], w_in,
        _pad_rows(p['ab_sconv_w'][0], SUBLANES), _pad_rows(p['ab_ssd_conv_w'][0], SUBLANES),
        p['ab_ssd_conv_b'], _pad_lanes(p['ab_dt_bias']), _pad_lanes(p['ab_a_log']),
        _pad_lanes(p['ab_d']), p['ab_ssd_norm'], p['ab_w_out'][0].astype(BF16),
        front(sconv_st), front(ssdc_st), ssd_st)
    return y, sa[:, CARRY - (SCONV_W - 1):], sb[:, CARRY - (SSD_CONV_W - 1):], hn


def _nsa_weights(p):
    w_in = p['nsa_w_in'][0]
    hq = NSA_HEADS * NSA_HD
    wq = w_in[:, :hq].astype(BF16)
    wkv_t = w_in[:, hq:hq + 6 * KV_W].T.astype(BF16)
    wg = _pad_lanes(w_in[:, hq + 6 * KV_W:]).astype(BF16)
    qgain2 = jnp.tile(p['nsa_q_norm'][0], 2)[None]
    kn = p['nsa_k_norm'][0]
    kgain = jnp.stack([kn[1], kn[2]])[:, :, None]
    kcgain2 = jnp.tile(kn[0], 2)[None]
    pe4 = jnp.tile(p['nsa_cmp_pe'][0], (1, 1, NSA_KV))
    w1bd = _blockdiag4(p['nsa_cmp_w1'][0]).astype(BF16)
    w1bd = w1bd.reshape(2, 2, CMP_STRIDE * KV_W, KV_W)
    w2bd = _blockdiag4(p['nsa_cmp_w2'][0]).astype(BF16)
    return wq, wg, wkv_t, qgain2, kgain, kcgain2, pe4, w1bd, w2bd


def kernel(x_prompt, x_sample, mem_prompt, state_sconv, state_ssd_conv, state_ssd, cache_nsa_kv,
           cache_nsa_win, cache_mem_kv, page_table, norm_mix, norm_mem, norm_memsrc, norm_ffn,
           ab_w_in, ab_sconv_w, ab_ssd_conv_w, ab_ssd_conv_b, ab_dt_bias, ab_a_log, ab_d, ab_ssd_norm,
           ab_w_out, nsa_w_in, nsa_q_norm, nsa_k_norm, nsa_cmp_pe, nsa_cmp_w1, nsa_cmp_w2, nsa_w_out,
           mem_w_q, mem_q_norm, mem_w_kv, mem_k_norm, mem_w_o, ffn_w_gate, ffn_w_up, ffn_w_down):
    p = dict(norm_mix=norm_mix, norm_mem=norm_mem, norm_ffn=norm_ffn, ab_w_in=ab_w_in,
             ab_sconv_w=ab_sconv_w, ab_ssd_conv_w=ab_ssd_conv_w, ab_ssd_conv_b=ab_ssd_conv_b,
             ab_dt_bias=ab_dt_bias, ab_a_log=ab_a_log, ab_d=ab_d, ab_ssd_norm=ab_ssd_norm,
             ab_w_out=ab_w_out, nsa_w_in=nsa_w_in, nsa_q_norm=nsa_q_norm, nsa_k_norm=nsa_k_norm,
             nsa_cmp_pe=nsa_cmp_pe, nsa_cmp_w1=nsa_cmp_w1, nsa_cmp_w2=nsa_cmp_w2, nsa_w_out=nsa_w_out,
             mem_w_q=mem_w_q, mem_q_norm=mem_q_norm, mem_w_o=mem_w_o,
             ffn_w_gate=ffn_w_gate, ffn_w_up=ffn_w_up, ffn_w_down=ffn_w_down)
    bp, t, d = x_prompt.shape
    bs, ts, _ = x_sample.shape
    n_mem = mem_prompt.shape[1]
    depth = norm_mix.shape[0]
    assert depth == 2 and ab_w_in.shape[0] == 1 and nsa_w_in.shape[0] == 1
    w = KV_W
    hq = NSA_HEADS * NSA_HD
    wq, wg, wkv_t, qgain2, kgain, kcgain2, pe4, w1bd, w2bd = _nsa_weights(p)
    wo_nsa = nsa_w_out[0].astype(BF16)

    mem2d = mem_prompt.reshape(bp * n_mem, d)
    p_mem = _memkv(mem2d, norm_memsrc[:, None], mem_w_kv.astype(BF16), mem_k_norm[:, None])
    p_mem = p_mem.reshape(depth, bp, n_mem * MEM_SLOTS, MEM_HD)
    p_mem_kv = p_mem.reshape(depth, bp, n_mem, 2, MEM_HEADS, MEM_HD)
    s_mem = cache_mem_kv.reshape(depth, bs, n_mem * MEM_SLOTS, MEM_HD)

    tq0 = min(256, t)
    tq_mem = min(512, t)
    tm_ffn = min(512, bp * t)
    zeros = lambda *s: jnp.zeros(s, F32)
    x, p_sconv, p_ssdc, p_ssd = _run_mixer0(
        x_prompt, tq0, zeros(bp, SCONV_W - 1, d), zeros(bp, SSD_CONV_W - 1, SSD_CONV_DIM),
        zeros(bp, SSD_HEADS, SSD_HD, SSD_N), p)
    x = _layer_tail(x, 0, p_mem, tq_mem, tm_ffn, p)

    pos_p = jnp.arange(t)
    tabs_p = _rope_tables(pos_p)
    tm_proj = min(512, t)
    q, gates, rows_t, win_t, kvbf = _nsa_proj(x, tm_proj, norm_mix[1][None], wq, wg, wkv_t, qgain2,
                                              kgain, tabs_p)
    nh = t // CMP_STRIDE
    n_cmp = nh - 1
    pos_c = jnp.arange(nh) * CMP_STRIDE + CMP_BLOCK - 1
    kc, vc = _p_compress(rows_t, pe4, w1bd, w2bd, kcgain2, _rope_tables(pos_c)[:3])
    n_slc = -(-t // SLC_BLOCK)
    tq_a = min(128, t)
    assert n_slc <= SEL_COLS
    epad = jnp.concatenate([_pad_rows(_expand_matrix(n_slc, t), SEL_COLS), jnp.ones((1, t), BF16),
                            jnp.zeros((LANES - DEN_COL - 1, t), BF16)], axis=0)
    x = _p_attn(q, gates, kc, vc, kvbf, _block_overlap(nh, n_cmp, n_slc).T, epad, wo_nsa, x, tq_a)
    y_prompt = _layer_tail(x, 1, p_mem, tq_mem, tm_ffn, p)
    p_rows = rows_t.reshape(bp, 1, 4, NSA_KV, NSA_HD, t).transpose(0, 5, 1, 2, 3, 4)
    wl = min(WINDOW, t)
    p_win = win_t[:, :, t - wl:].reshape(1, bp, 2, NSA_KV, NSA_HD, wl).transpose(0, 1, 5, 2, 3, 4)

    n_pool, page = cache_nsa_kv.shape[0], cache_nsa_kv.shape[1]
    n_pages = page_table.shape[1]
    past = n_pages * page
    pps_slc = min(MAX_PAGES_SLC, n_pages)
    assert ts <= CMP_STRIDE and past % SLC_BLOCK == 0
    assert n_pages % pps_slc == 0 and n_pages % min(MAX_PAGES_COMPRESS, n_pages) == 0
    xs, s_sconv, s_ssdc, s_ssd = _run_mixer0(x_sample, ts, state_sconv[0], state_ssd_conv[0],
                                             state_ssd[0], p)
    xs = _layer_tail(xs, 0, s_mem, ts, bs * ts, p)

    pos_s = jnp.tile(past + jnp.arange(ts), bs)
    qs, gates_s, rows_s, win_s, _ = _nsa_proj(xs.reshape(1, bs * ts, d), bs * ts, norm_mix[1][None],
                                              wq, wg, wkv_t, qgain2, kgain, _rope_tables(pos_s))
    qs = qs.reshape(bs, ts, hq)
    gates_s = gates_s.reshape(bs, ts, LANES)
    rows_s = rows_s[0].reshape(4 * w, bs, ts).transpose(1, 0, 2)
    win_s = win_s[0].reshape(2 * w, bs, ts).transpose(1, 0, 2)
    cache_t = cache_nsa_kv.transpose(0, 2, 3, 4, 5, 1).reshape(n_pool, 4 * w, page)
    a_sum, b_sum = _s_compress(cache_t, page_table, pe4, w1bd)
    nh_s = past // CMP_STRIDE
    n_cmp_s = (past + ts) // CMP_STRIDE - 1
    pos_cs = jnp.arange(nh_s) * CMP_STRIDE + CMP_BLOCK - 1
    n_blk = past // SLC_BLOCK
    o_c, sel = _s_cmp(a_sum, b_sum, qs, w2bd, kcgain2, _rope_tables(pos_cs)[:3],
                      _block_overlap(nh_s, n_cmp_s, n_blk), past)
    steps = n_pages // pps_slc
    bps = n_blk // steps
    sel_steps = sel.reshape(bs, NSA_KV * ts, steps, bps).transpose(0, 2, 1, 3)
    o_s = _s_slc(cache_t, page_table, qs, sel_steps, rows_s[:, 2 * w:],
                 _expand_matrix(bps, pps_slc * page), pps_slc)
    wb = cache_nsa_win.shape[2]
    cache_win_t = cache_nsa_win[0].transpose(0, 2, 3, 4, 1).reshape(bs, 2 * w, wb)
    o_att, new_win_t = _s_win(qs, gates_s, o_c, o_s, cache_win_t, win_s)
    xs = _proj_res(o_att.reshape(bs * ts, hq), wo_nsa, xs.reshape(bs * ts, d)).reshape(bs, ts, d)
    y_sample = _layer_tail(xs, 1, s_mem, ts, bs * ts, p)
    s_rows = rows_s.transpose(0, 2, 1).reshape(bs, ts, 1, 4, NSA_KV, NSA_HD)
    s_win = new_win_t.reshape(1, bs, 2, NSA_KV, NSA_HD, wb).transpose(0, 1, 5, 2, 3, 4)

    return (y_prompt, y_sample, p_sconv[None], p_ssdc[None], p_ssd[None], p_rows, p_win, p_mem_kv,
            s_sconv[None], s_ssdc[None], s_ssd[None], s_rows, s_win)
```

```python
import functools
import math

import jax
import jax.numpy as jnp
from jax import lax
from jax.experimental import pallas as pl
from jax.experimental.pallas import tpu as pltpu

F32 = jnp.float32
BF16 = jnp.bfloat16
HI = lax.Precision.HIGHEST

EPS = 1e-6
ROPE_THETA = 500000.0
NEG = -1e30
BIG = 1e9

LANES = 128
SUBLANES = 8
VMEM_LIMIT = 56 * 1024 * 1024

D_MODEL = 1024
SCONV_W = 3
SSD_HEADS = 16
SSD_HD = 64
SSD_DIM = SSD_HEADS * SSD_HD
SSD_GROUPS = 4
SSD_HPG = SSD_HEADS // SSD_GROUPS
SSD_N = 128
SSD_CONV_W = 4
SSD_CONV_DIM = SSD_DIM + 2 * SSD_GROUPS * SSD_N
IN0_DIM = 3 * D_MODEL + SSD_DIM + SSD_CONV_DIM + SSD_HEADS
IN0_PAD = 3 * D_MODEL + SSD_DIM + SSD_CONV_DIM + LANES
NSA_HEADS = 16
NSA_KV = 4
NSA_HD = 64
NSA_HPG = NSA_HEADS // NSA_KV
ROT_HALF = NSA_HD // 8
CMP_BLOCK = 32
CMP_STRIDE = 16
SLC_BLOCK = 64
SLC_TOP = 16
WINDOW = 512
KV_W = NSA_KV * NSA_HD
MEM_HEADS = 4
MEM_HD = 128
CARRY = SUBLANES


def _cparams(sem):
    return pltpu.CompilerParams(dimension_semantics=sem, vmem_limit_bytes=VMEM_LIMIT)


def _const_spec(shape, single_buffer=False):
    n = len(shape)
    if single_buffer:
        return pl.BlockSpec(shape, lambda *_: (0,) * n, pipeline_mode=pl.Buffered(1))
    return pl.BlockSpec(shape, lambda *_: (0,) * n)


def _rms(x, g):
    ms = jnp.mean(x * x, axis=-1, keepdims=True)
    return x * lax.rsqrt(ms + EPS) * g


def _dot(a, b):
    return jnp.dot(a, b, preferred_element_type=F32)


def _dot_nt(a, b, precision=None):
    return lax.dot_general(a, b, (((1,), (1,)), ((), ())), precision=precision,
                           preferred_element_type=F32)


def _dot_tn(a, b):
    return lax.dot_general(a, b, (((0,), (0,)), ((), ())), preferred_element_type=F32)


def _split_bf16(x, terms):
    parts = []
    for _ in range(terms - 1):
        p = x.astype(BF16)
        parts.append(p)
        x = x - p.astype(F32)
    parts.append(x.astype(BF16))
    return parts


def _dot_exact_lhs(a_bf, x, terms=3):
    out = None
    for p in _split_bf16(x, terms):
        d = _dot(a_bf, p)
        out = d if out is None else out + d
    return out


def _dot_nt_exact_lhs(a_bf, x, terms=3):
    out = None
    for p in _split_bf16(x, terms):
        d = _dot_nt(a_bf, p)
        out = d if out is None else out + d
    return out


def _dot_exact_rhs(x, b_bf, terms=3):
    out = None
    for p in _split_bf16(x, terms):
        d = _dot(p, b_bf)
        out = d if out is None else out + d
    return out


def _silu(x):
    return x * jax.nn.sigmoid(x)


def _softplus(x):
    return jnp.maximum(x, 0.0) + jnp.log1p(jnp.exp(-jnp.abs(x)))


def _seg_mean_matrix():
    r = lax.broadcasted_iota(jnp.int32, (LANES, LANES), 0) // NSA_HD
    c = lax.broadcasted_iota(jnp.int32, (LANES, LANES), 1) // NSA_HD
    return jnp.where(r == c, 1.0 / NSA_HD, 0.0).astype(F32)


def _headnorm_rope_tok(x, gain, bd, c, s1, s2):
    ms = _dot_exact_rhs(x * x, bd.astype(BF16), terms=2)
    xn = x * lax.rsqrt(ms + EPS) * gain
    return xn * c + pltpu.roll(xn, LANES - ROT_HALF, 1) * s1 + pltpu.roll(xn, ROT_HALF, 1) * s2


def _headnorm_rope_ch(x, gain_col, cos_t, sin_t):
    ms = jnp.mean(x * x, axis=0, keepdims=True)
    xn = x * lax.rsqrt(ms + EPS) * gain_col
    x1 = xn[0:ROT_HALF]
    x2 = xn[ROT_HALF:2 * ROT_HALF]
    return x1 * cos_t - x2 * sin_t, x2 * cos_t + x1 * sin_t, xn[2 * ROT_HALF:]


def _rope_tables(pos):
    inv = ROPE_THETA ** (-jnp.arange(ROT_HALF, dtype=F32) / ROT_HALF)
    ang = pos.astype(F32)[:, None] * inv[None, :]
    cos, sin = jnp.cos(ang), jnp.sin(ang)
    n = pos.shape[0]
    ones = jnp.ones((n, NSA_HD - 2 * ROT_HALF), F32)
    zeros8 = jnp.zeros((n, ROT_HALF), F32)
    zeros = jnp.zeros((n, NSA_HD - 2 * ROT_HALF), F32)
    c = jnp.concatenate([cos, cos, ones], axis=1)
    s1 = jnp.concatenate([-sin, zeros8, zeros], axis=1)
    s2 = jnp.concatenate([zeros8, sin, zeros], axis=1)
    tile2 = lambda t: jnp.concatenate([t, t], axis=1)
    return tile2(c), tile2(s1), tile2(s2), cos.T, sin.T


MEM_SLOTS = 2 * MEM_HEADS


def _memkv_kernel(x_ref, g_ref, w_ref, kn_ref, o_ref, *, tm):
    hn = _rms(x_ref[...], g_ref[0]).astype(BF16)
    kv = _dot(hn, w_ref[0])
    for c in range(MEM_SLOTS):
        blk = kv[:, c * MEM_HD:(c + 1) * MEM_HD]
        if c < MEM_HEADS:
            blk = _rms(blk, kn_ref[0])
        o_ref[0, pl.ds(c, tm, stride=MEM_SLOTS), :] = blk


def _memkv(mem2d, g, w_bf, kn):
    m = mem2d.shape[0]
    depth = w_bf.shape[0]
    tm = min(512, m)
    n = w_bf.shape[2]
    return pl.pallas_call(
        functools.partial(_memkv_kernel, tm=tm),
        grid=(depth, m // tm),
        in_specs=[pl.BlockSpec((tm, D_MODEL), lambda l, i: (i, 0)),
                  pl.BlockSpec((1, 1, D_MODEL), lambda l, i: (l, 0, 0)),
                  pl.BlockSpec((1, D_MODEL, n), lambda l, i: (l, 0, 0)),
                  pl.BlockSpec((1, 1, MEM_HD), lambda l, i: (l, 0, 0))],
        out_specs=pl.BlockSpec((1, tm * MEM_SLOTS, MEM_HD), lambda l, i: (l, i, 0)),
        out_shape=jax.ShapeDtypeStruct((depth, m * MEM_SLOTS, MEM_HD), F32),
        compiler_params=_cparams(("parallel", "parallel")),
        name="memkv",
    )(mem2d, g, w_bf, kn)


def _mixer0_project(x_ref, g_ref, win_ref, proj, n_col_chunks):
    hn = _rms(x_ref[0], g_ref[...]).astype(BF16)
    cw_cols = -(-IN0_PAD // n_col_chunks // LANES) * LANES

    def chunk(c):
        sl = slice(c * cw_cols, min((c + 1) * cw_cols, IN0_PAD))
        proj[:, sl] = _dot(hn, win_ref[:, sl])

    return [functools.partial(chunk, c) for c in range(-(-IN0_PAD // cw_cols))]


def _mixer0_mix(proj, x_ref, scw_ref, cw_ref, cb_ref, dtb_ref, alog_ref, dsk_ref, ssdn_ref, wout_ref,
                y_ref, ca, cbuf, hst, ycat, tq, side_work):
    side_work = list(side_work)

    def side(n=1):
        for _ in range(min(n, len(side_work))):
            side_work.pop(0)()

    x = x_ref[0]
    d = D_MODEL
    side(2)
    ca[CARRY:CARRY + tq, :] = proj[:, 2 * d:3 * d] * proj[:, 0:d]
    conv_u = ca[pl.ds(CARRY - 2, tq), :] * scw_ref[0:1, :]
    for k in range(1, SCONV_W):
        conv_u = conv_u + ca[pl.ds(CARRY - 2 + k, tq), :] * scw_ref[k:k + 1, :]
    ycat[:, 0:d] = (proj[:, d:2 * d] * conv_u).astype(BF16)

    side(4)
    x0 = 3 * d + SSD_DIM
    cbuf[CARRY:CARRY + tq, :] = proj[:, x0:x0 + SSD_CONV_DIM]
    xbc = cbuf[pl.ds(CARRY - 3, tq), :] * cw_ref[0:1, :]
    for k in range(1, SSD_CONV_W):
        xbc = xbc + cbuf[pl.ds(CARRY - 3 + k, tq), :] * cw_ref[k:k + 1, :]
    xbc = _silu(xbc + cb_ref[...])

    dt = _softplus(proj[:, x0 + SSD_CONV_DIM:IN0_PAD] + dtb_ref[...])
    a_neg = -jnp.exp(alog_ref[...])
    row = lax.broadcasted_iota(jnp.int32, (tq, tq), 0)
    col = lax.broadcasted_iota(jnp.int32, (tq, tq), 1)
    causal = row >= col
    tril = jnp.where(causal, 1.0, 0.0).astype(BF16)
    cum = _dot_exact_lhs(tril, dt * a_neg)
    eye = jnp.where(lax.broadcasted_iota(jnp.int32, (LANES, LANES), 0)
                    == lax.broadcasted_iota(jnp.int32, (LANES, LANES), 1), 1.0, 0.0).astype(BF16)
    cum_t = _dot_nt_exact_lhs(eye, cum)
    cum_last = cum[tq - 1:tq, :]
    e_cum = jnp.exp(cum)
    e_last = jnp.exp(cum_last)
    e_end = jnp.exp(cum_last - cum)
    z = proj[:, 3 * d:3 * d + SSD_DIM]
    dsk = dsk_ref[...]

    for g in range(SSD_GROUPS):
        b_g = xbc[:, SSD_DIM + g * SSD_N:SSD_DIM + (g + 1) * SSD_N].astype(BF16)
        c_g = xbc[:, SSD_DIM + (SSD_GROUPS + g) * SSD_N:
                  SSD_DIM + (SSD_GROUPS + g + 1) * SSD_N].astype(BF16)
        cb_g = _dot_nt(c_g, b_g)
        y_parts = []
        for hh in range(SSD_HPG):
            h = g * SSD_HPG + hh
            side()
            x_h = xbc[:, h * SSD_HD:(h + 1) * SSD_HD]
            xdt = x_h * dt[:, h:h + 1]
            lmat = jnp.where(causal, jnp.exp(cum[:, h:h + 1] - cum_t[h:h + 1, :]), 0.0)
            y_h = _dot((cb_g * lmat).astype(BF16), xdt.astype(BF16))
            st = hst[h]
            y_h = y_h + e_cum[:, h:h + 1] * _dot_nt(c_g, st.astype(BF16))
            hst[h] = e_last[:, h:h + 1] * st + _dot_tn((xdt * e_end[:, h:h + 1]).astype(BF16), b_g)
            y_parts.append(y_h + dsk[:, h:h + 1] * x_h)
        yg = jnp.concatenate(y_parts, axis=1)
        gs = slice(g * SSD_HPG * SSD_HD, (g + 1) * SSD_HPG * SSD_HD)
        yg = yg * _silu(z[:, gs])
        yg = yg * lax.rsqrt(jnp.mean(yg * yg, axis=-1, keepdims=True) + EPS)
        ycat[:, d + gs.start:d + gs.stop] = (yg * ssdn_ref[:, gs]).astype(BF16)

    side()
    y_ref[0] = x + _dot(ycat[...], wout_ref[...])
    while side_work:
        side()

    ca[0:CARRY, :] = ca[tq:tq + CARRY, :]
    cbuf[0:CARRY, :] = cbuf[tq:tq + CARRY, :]


def _mixer0_kernel(x_ref, g_ref, win_ref, scw_ref, cw_ref, cb_ref, dtb_ref, alog_ref,
                   dsk_ref, ssdn_ref, wout_ref, sa0_ref, sb0_ref, h0_ref,
                   y_ref, sa_ref, sb_ref, hout_ref,
                   proj, ca, cbuf, hst, ycat, *, tq, n_col_chunks):
    i = pl.program_id(1)

    @pl.when(i == 0)
    def _():
        ca[0:CARRY, :] = sa0_ref[0]
        cbuf[0:CARRY, :] = sb0_ref[0]
        hst[...] = h0_ref[0]

    for chunk in _mixer0_project(x_ref, g_ref, win_ref, proj, n_col_chunks):
        chunk()
    _mixer0_mix(proj, x_ref, scw_ref, cw_ref, cb_ref, dtb_ref, alog_ref, dsk_ref, ssdn_ref,
                wout_ref, y_ref, ca, cbuf, hst, ycat, tq, ())

    @pl.when(i == pl.num_programs(1) - 1)
    def _():
        sa_ref[0] = ca[0:CARRY, :]
        sb_ref[0] = cbuf[0:CARRY, :]
        hout_ref[0] = hst[...]


def _mixer0(x, tq, g, w_in_bf, scw, cw, cb, dtb, alog, dsk, ssdn, w_out_bf, sa0, sb0, h0):
    bsz, t, d = x.shape
    n_col_chunks = 7
    kern = functools.partial(_mixer0_kernel, tq=tq, n_col_chunks=n_col_chunks)
    per_b = lambda b, i: (b, 0, 0)
    return pl.pallas_call(
        kern,
        grid=(bsz, t // tq),
        in_specs=[pl.BlockSpec((1, tq, d), lambda b, i: (b, i, 0)),
                  _const_spec((1, d)), _const_spec((d, IN0_PAD)),
                  _const_spec((SUBLANES, d)), _const_spec((SUBLANES, SSD_CONV_DIM)),
                  _const_spec((1, SSD_CONV_DIM)), _const_spec((1, LANES)), _const_spec((1, LANES)),
                  _const_spec((1, LANES)), _const_spec((1, SSD_DIM)),
                  _const_spec((d + SSD_DIM, d)),
                  pl.BlockSpec((1, CARRY, d), per_b),
                  pl.BlockSpec((1, CARRY, SSD_CONV_DIM), per_b),
                  pl.BlockSpec((1, SSD_HEADS, SSD_HD, SSD_N), lambda b, i: (b, 0, 0, 0))],
        out_specs=[pl.BlockSpec((1, tq, d), lambda b, i: (b, i, 0)),
                   pl.BlockSpec((1, CARRY, d), per_b),
                   pl.BlockSpec((1, CARRY, SSD_CONV_DIM), per_b),
                   pl.BlockSpec((1, SSD_HEADS, SSD_HD, SSD_N), lambda b, i: (b, 0, 0, 0))],
        out_shape=[jax.ShapeDtypeStruct((bsz, t, d), F32),
                   jax.ShapeDtypeStruct((bsz, CARRY, d), F32),
                   jax.ShapeDtypeStruct((bsz, CARRY, SSD_CONV_DIM), F32),
                   jax.ShapeDtypeStruct((bsz, SSD_HEADS, SSD_HD, SSD_N), F32)],
        scratch_shapes=[pltpu.VMEM((tq, IN0_PAD), F32),
                        pltpu.VMEM((CARRY + tq, d), F32),
                        pltpu.VMEM((CARRY + tq, SSD_CONV_DIM), F32),
                        pltpu.VMEM((SSD_HEADS, SSD_HD, SSD_N), F32),
                        pltpu.VMEM((tq, d + SSD_DIM), BF16)],
        compiler_params=_cparams(("parallel", "arbitrary")),
        name="mixer0",
    )(x, g, w_in_bf, scw, cw, cb, dtb, alog, dsk, ssdn, w_out_bf, sa0, sb0, h0)


def _memattn_kernel(x_ref, g_ref, wq_ref, qn_ref, kv_ref, wo_ref, o_ref, *, n_mem):
    x = x_ref[0]
    hn = _rms(x, g_ref[...]).astype(BF16)
    q = _dot(hn, wq_ref[...])
    outs = []
    for h in range(MEM_HEADS):
        sl = slice(h * MEM_HD, (h + 1) * MEM_HD)
        k_h = kv_ref[0, 0, pl.ds(h, n_mem, stride=MEM_SLOTS), :].astype(BF16)
        v_h = kv_ref[0, 0, pl.ds(MEM_HEADS + h, n_mem, stride=MEM_SLOTS), :].astype(BF16)
        qh = _rms(q[:, sl], qn_ref[...]).astype(BF16)
        s = _dot_nt(qh, k_h) * (MEM_HD ** -0.5)
        m = jnp.max(s, axis=-1, keepdims=True)
        e = jnp.exp(s - m)
        p = e / jnp.sum(e, axis=-1, keepdims=True)
        outs.append(_dot(p.astype(BF16), v_h))
    o = jnp.concatenate(outs, axis=1).astype(BF16)
    o_ref[0] = x + _dot(o, wo_ref[...])


def _memattn(x, tq, g, wq_bf, qn, kv4, layer, wo_bf):
    bsz, t, d = x.shape
    rows = kv4.shape[2]
    hw = MEM_HEADS * MEM_HD
    return pl.pallas_call(
        functools.partial(_memattn_kernel, n_mem=rows // MEM_SLOTS),
        grid=(bsz, t // tq),
        in_specs=[pl.BlockSpec((1, tq, d), lambda b, i: (b, i, 0)),
                  _const_spec((1, d)), _const_spec((d, hw)), _const_spec((1, MEM_HD)),
                  pl.BlockSpec((1, 1, rows, MEM_HD), lambda b, i: (layer, b, 0, 0)),
                  _const_spec((hw, d))],
        out_specs=pl.BlockSpec((1, tq, d), lambda b, i: (b, i, 0)),
        out_shape=jax.ShapeDtypeStruct((bsz, t, d), F32),
        compiler_params=_cparams(("parallel", "parallel")),
        name="memattn",
    )(x, g, wq_bf, qn, kv4, wo_bf)


FFN_CHUNKS = 11


def _ffn_kernel(x_ref, g_ref, wg_ref, wu_ref, wd_ref, o_ref):
    x = x_ref[...]
    hn = _rms(x, g_ref[...]).astype(BF16)
    cf = wg_ref.shape[1] // FFN_CHUNKS
    acc = x
    for c in range(FFN_CHUNKS):
        sl = slice(c * cf, (c + 1) * cf)
        a = _silu(_dot(hn, wg_ref[:, sl])) * _dot(hn, wu_ref[:, sl])
        acc = acc + _dot(a.astype(BF16), wd_ref[sl, :])
    o_ref[...] = acc


def _ffn(x2d, tm, g, wg_bf, wu_bf, wd_bf):
    m, d = x2d.shape
    dff = wg_bf.shape[1]
    assert dff % (FFN_CHUNKS * LANES) == 0
    return pl.pallas_call(
        _ffn_kernel,
        grid=(m // tm,),
        in_specs=[pl.BlockSpec((tm, d), lambda i: (i, 0)), _const_spec((1, d)),
                  _const_spec((d, dff), single_buffer=True),
                  _const_spec((d, dff), single_buffer=True),
                  _const_spec((dff, d), single_buffer=True)],
        out_specs=pl.BlockSpec((tm, d), lambda i: (i, 0)),
        out_shape=jax.ShapeDtypeStruct((m, d), F32),
        compiler_params=_cparams(("parallel",)),
        name="ffn",
    )(x2d, g, wg_bf, wu_bf, wd_bf)


def _proj_res_kernel(a_ref, w_ref, x_ref, o_ref):
    o_ref[...] = x_ref[...] + _dot(a_ref[...].astype(BF16), w_ref[...])


def _proj_res(a2d, w_bf, x2d):
    m, k = a2d.shape
    n = w_bf.shape[1]
    tm = min(256, m)
    return pl.pallas_call(
        _proj_res_kernel,
        grid=(m // tm,),
        in_specs=[pl.BlockSpec((tm, k), lambda i: (i, 0)), _const_spec((k, n)),
                  pl.BlockSpec((tm, n), lambda i: (i, 0))],
        out_specs=pl.BlockSpec((tm, n), lambda i: (i, 0)),
        out_shape=jax.ShapeDtypeStruct((m, n), F32),
        compiler_params=_cparams(("parallel",)),
        name="proj_res",
    )(a2d, w_bf, x2d)


def _nsa_proj_kernel(x_ref, g_ref, wq_ref, wg_ref, wkv_ref, bd_ref, qg_ref, c_ref, s1_ref, s2_ref,
                     kg_ref, cos_ref, sin_ref,
                     q_ref, gate_ref, rows_ref, win_ref, kvbf_ref):
    hn = _rms(x_ref[0], g_ref[...]).astype(BF16)
    q = _dot(hn, wq_ref[...])
    bd = bd_ref[...]
    scale = NSA_HD ** -0.5
    for c in range(NSA_HEADS * NSA_HD // LANES):
        sl = slice(c * LANES, (c + 1) * LANES)
        q_ref[0, :, sl] = scale * _headnorm_rope_tok(q[:, sl], qg_ref[...], bd, c_ref[...],
                                                     s1_ref[...], s2_ref[...])
    gate_ref[0] = jax.nn.sigmoid(_dot(hn, wg_ref[...]))
    kv_t = _dot_nt(wkv_ref[...], hn)
    cos_t, sin_t = cos_ref[...], sin_ref[...]

    def put_normed(src_row, gain_idx, dst_ref, dst_row, bf_row):
        for g in range(NSA_KV):
            r1, r2, rest = _headnorm_rope_ch(kv_t[src_row + g * NSA_HD:src_row + (g + 1) * NSA_HD],
                                             kg_ref[gain_idx], cos_t, sin_t)
            full = jnp.concatenate([r1, r2, rest], axis=0)
            dst_ref[0, dst_row + g * NSA_HD:dst_row + (g + 1) * NSA_HD, :] = full
            kvbf_ref[0, bf_row + g * NSA_HD:bf_row + (g + 1) * NSA_HD, :] = full.astype(BF16)

    w = KV_W
    rows_ref[0, 0:2 * w, :] = kv_t[0:2 * w]
    put_normed(2 * w, 0, rows_ref, 2 * w, 0)
    rows_ref[0, 3 * w:4 * w, :] = kv_t[3 * w:4 * w]
    kvbf_ref[0, w:2 * w, :] = kv_t[3 * w:4 * w].astype(BF16)
    put_normed(4 * w, 1, win_ref, 0, 2 * w)
    win_ref[0, w:2 * w, :] = kv_t[5 * w:6 * w]
    kvbf_ref[0, 3 * w:4 * w, :] = kv_t[5 * w:6 * w].astype(BF16)


def _nsa_proj(x, tm, g, wq_bf, wg_bf, wkv_t_bf, qgain2, kgain, tabs):
    bsz, t, d = x.shape
    c, s1, s2, cos_t, sin_t = tabs
    hq = NSA_HEADS * NSA_HD
    w = KV_W
    tok = lambda b, i: (b, i, 0)
    chan = lambda b, i: (b, 0, i)
    return pl.pallas_call(
        _nsa_proj_kernel,
        grid=(bsz, t // tm),
        in_specs=[pl.BlockSpec((1, tm, d), tok), _const_spec((1, d)),
                  _const_spec((d, hq)), _const_spec((d, LANES)), _const_spec((6 * w, d)),
                  _const_spec((LANES, LANES)), _const_spec((1, LANES)),
                  pl.BlockSpec((tm, LANES), lambda b, i: (i, 0)),
                  pl.BlockSpec((tm, LANES), lambda b, i: (i, 0)),
                  pl.BlockSpec((tm, LANES), lambda b, i: (i, 0)),
                  _const_spec((2, NSA_HD, 1)),
                  pl.BlockSpec((ROT_HALF, tm), lambda b, i: (0, i)),
                  pl.BlockSpec((ROT_HALF, tm), lambda b, i: (0, i))],
        out_specs=[pl.BlockSpec((1, tm, hq), tok), pl.BlockSpec((1, tm, LANES), tok),
                   pl.BlockSpec((1, 4 * w, tm), chan), pl.BlockSpec((1, 2 * w, tm), chan),
                   pl.BlockSpec((1, 4 * w, tm), chan)],
        out_shape=[jax.ShapeDtypeStruct((bsz, t, hq), F32),
                   jax.ShapeDtypeStruct((bsz, t, LANES), F32),
                   jax.ShapeDtypeStruct((bsz, 4 * w, t), F32),
                   jax.ShapeDtypeStruct((bsz, 2 * w, t), F32),
                   jax.ShapeDtypeStruct((bsz, 4 * w, t), BF16)],
        compiler_params=_cparams(("parallel", "parallel")),
        name="nsa_proj",
    )(x, g, wq_bf, wg_bf, wkv_t_bf, _seg_mean_matrix(), qgain2, c, s1, s2, kgain, cos_t, sin_t)


def _half_sums(xt, pe_ref, w1_ref, nh):
    out = []
    tiles = KV_W // LANES
    for kind in range(2):
        rows = [jnp.concatenate([xt[kind * tiles + c, pl.ds(j, nh, stride=CMP_STRIDE), :]
                                 for c in range(tiles)], axis=1) for j in range(CMP_STRIDE)]
        for half in range(2):
            lhs = jnp.concatenate(
                [(rows[j] + pe_ref[kind, half * CMP_STRIDE + j:half * CMP_STRIDE + j + 1, :]).astype(BF16)
                 for j in range(CMP_STRIDE)], axis=1)
            out.append(_dot(lhs, w1_ref[kind, half]))
    return out


def _finish_compress(a_k, b_k, a_v, b_v, w2_ref, kg_ref, bd, c, s1, s2):
    nh = a_k.shape[0]
    kc = _dot(_silu(a_k + pltpu.roll(b_k, nh - 1, 0)).astype(BF16), w2_ref[0])
    vc = _dot(_silu(a_v + pltpu.roll(b_v, nh - 1, 0)).astype(BF16), w2_ref[1])
    kc = jnp.concatenate([_headnorm_rope_tok(kc[:, h * LANES:(h + 1) * LANES], kg_ref[...], bd, c, s1, s2)
                          for h in range(KV_W // LANES)], axis=1)
    return kc, vc


def _p_compress_kernel(rows_ref, pe_ref, w1_ref, w2_ref, kg_ref, bd_ref, c_ref, s1_ref, s2_ref,
                       kc_ref, vc_ref, xt, *, nh):
    for c in range(2 * KV_W // LANES):
        xt[c] = rows_ref[0, c * LANES:(c + 1) * LANES, :].T
    a_k, b_k, a_v, b_v = _half_sums(xt, pe_ref, w1_ref, nh)
    kc, vc = _finish_compress(a_k, b_k, a_v, b_v, w2_ref, kg_ref, bd_ref[...], c_ref[...],
                              s1_ref[...], s2_ref[...])
    kc_ref[0] = kc.astype(BF16)
    vc_ref[0] = vc.astype(BF16)


def _p_compress(rows_t, pe4, w1bd, w2bd, kgain2, ctabs):
    bsz, _, t = rows_t.shape
    nh = t // CMP_STRIDE
    c, s1, s2 = ctabs
    w = KV_W
    return pl.pallas_call(
        functools.partial(_p_compress_kernel, nh=nh),
        grid=(bsz,),
        in_specs=[pl.BlockSpec((1, 2 * w, t), lambda b: (b, 0, 0)),
                  _const_spec((2, CMP_BLOCK, w)), _const_spec((2, 2, CMP_STRIDE * w, w)),
                  _const_spec((2, w, w)), _const_spec((1, LANES)), _const_spec((LANES, LANES)),
                  _const_spec((nh, LANES)), _const_spec((nh, LANES)), _const_spec((nh, LANES))],
        out_specs=[pl.BlockSpec((1, nh, w), lambda b: (b, 0, 0)),
                   pl.BlockSpec((1, nh, w), lambda b: (b, 0, 0))],
        out_shape=[jax.ShapeDtypeStruct((bsz, nh, w), BF16), jax.ShapeDtypeStruct((bsz, nh, w), BF16)],
        scratch_shapes=[pltpu.VMEM((2 * w // LANES, t, LANES), F32)],
        compiler_params=_cparams(("parallel",)),
        name="p_compress",
    )(rows_t, pe4, w1bd, w2bd, kgain2, _seg_mean_matrix(), c, s1, s2)


def _stack_heads(q, g):
    return jnp.concatenate([q[:, (g * NSA_HPG + hh) * NSA_HD:(g * NSA_HPG + hh + 1) * NSA_HD]
                            for hh in range(NSA_HPG)], axis=0)


def _stack_gate(gates, g, j):
    return jnp.concatenate([gates[:, (g * NSA_HPG + hh) * 3 + j:(g * NSA_HPG + hh) * 3 + j + 1]
                            for hh in range(NSA_HPG)], axis=0)


def _tile_rows(m, n):
    return jnp.concatenate([m] * n, axis=0)


def _topk_mask(imp, k):
    r, s = imp.shape
    idx = lax.broadcasted_iota(jnp.int32, (r, s), 1)
    rank = jnp.zeros((r, s), F32)
    for s2 in range(s):
        col = imp[:, s2:s2 + 1]
        beats = jnp.where(col > imp, 1.0, jnp.where((col == imp) & (idx > s2), 1.0, 0.0))
        rank = rank + beats
    return rank < k


def _topk_bias_t(imp_t, k):
    s, r = imp_t.shape
    idx = lax.broadcasted_iota(jnp.int32, (s, r), 0)
    rank = jnp.zeros((s, r), F32)
    for s2 in range(s):
        row = imp_t[s2:s2 + 1, :]
        rank = rank + jnp.where(row > imp_t, 1.0, jnp.where((row == imp_t) & (idx > s2), 1.0, 0.0))
    return jnp.where((rank < k) & (imp_t > 0.5 * NEG), 0.0, NEG)


SEL_COLS = 32
DEN_COL = NSA_HD + SEL_COLS


def _p_attn_kernel(q_ref, gate_ref, kc_ref, vc_ref, kvs_ref, *refs, tq, t_all, kch, n_wb):
    win_refs = refs[:n_wb]
    (movt_ref, epad_ref, eye4_ref, wo_ref, x_ref, y_ref,
     o_scr, qa_scr, qbd_scr, oc_scr, os_scr) = refs[n_wb:]
    i = pl.program_id(1)
    q = q_ref[0]
    gates = gate_ref[0]
    nh = kc_ref.shape[1]
    n_slc = movt_ref.shape[0]
    pos_q = i * tq + lax.broadcasted_iota(jnp.int32, (tq, 1), 0)
    pos_q_row = i * tq + lax.broadcasted_iota(jnp.int32, (1, tq), 1)
    pos_c = lax.broadcasted_iota(jnp.int32, (tq, nh), 1) * CMP_STRIDE + (CMP_BLOCK - 1)
    ok_c = _tile_rows(pos_c <= pos_q, NSA_HEADS)
    blk_t = lax.broadcasted_iota(jnp.int32, (n_slc, NSA_KV * tq), 0)
    cur_t = jnp.concatenate([pos_q_row // SLC_BLOCK] * NSA_KV, axis=1)
    w = KV_W
    rows_g = NSA_HPG * tq

    q_heads = [_stack_heads(q, g) for g in range(NSA_KV)]
    def in_group_columns(qg, g):
        left = [jnp.zeros((rows_g, g * NSA_HD), F32)] if g else []
        right = [jnp.zeros((rows_g, (NSA_KV - 1 - g) * NSA_HD), F32)] if g < NSA_KV - 1 else []
        return jnp.concatenate(left + [qg] + right, axis=1)

    qbd_scr[...] = jnp.concatenate([in_group_columns(q_heads[g], g) for g in range(NSA_KV)],
                                   axis=0).astype(BF16)

    s = jnp.where(ok_c, _dot_nt(qbd_scr[...], kc_ref[0]), NEG)
    e = jnp.exp(s - jnp.max(s, axis=-1, keepdims=True))
    p = jnp.where(ok_c, e / jnp.sum(e, axis=-1, keepdims=True), 0.0)
    oc_scr[...] = _dot(p.astype(BF16), vc_ref[0])
    psum = []
    for g in range(NSA_KV):
        acc = p[g * rows_g:g * rows_g + tq]
        for hh in range(1, NSA_HPG):
            acc = acc + p[g * rows_g + hh * tq:g * rows_g + (hh + 1) * tq]
        psum.append(acc)
    imp_t = _dot_nt_exact_lhs(movt_ref[...].astype(BF16), jnp.concatenate(psum, axis=0))
    imp_t = jnp.where((blk_t == cur_t) | (blk_t == 0), BIG, imp_t)
    imp_t = jnp.where(blk_t > cur_t, NEG, imp_t)
    selb_t = _topk_bias_t(imp_t, SLC_TOP).astype(BF16)
    pad = jnp.zeros((rows_g, LANES - NSA_HD - n_slc), F32)
    for g in range(NSA_KV):
        selb = _dot_nt(eye4_ref[...], selb_t[:, g * tq:(g + 1) * tq])
        qa_scr[g] = jnp.concatenate([q_heads[g], selb, pad], axis=1).astype(BF16)

    def k_rows(g):
        return slice(g * NSA_HD, (g + 1) * NSA_HD)

    def v_rows(g):
        return slice(w + g * NSA_HD, w + (g + 1) * NSA_HD)

    n_var = t_all // kch
    need = (i * tq) // kch + 1
    for nv in range(1, n_var + 1):
        @pl.when(need == nv)
        def _(nv=nv):
            klen = nv * kch
            head = klen - kch
            tail_pos = head + lax.broadcasted_iota(jnp.int32, (tq, kch), 1)
            cb4 = _tile_rows(jnp.where(tail_pos <= pos_q, 0.0, NEG), NSA_HPG)

            for g in range(NSA_KV):
                qa = qa_scr[g]
                k_tail = jnp.concatenate([kvs_ref[0, k_rows(g), head:klen], epad_ref[:, head:klen]], axis=0)
                s_t = _dot(qa, k_tail) + cb4
                m = jnp.max(s_t, axis=-1, keepdims=True)
                if head:
                    k_head = jnp.concatenate([kvs_ref[0, k_rows(g), 0:head], epad_ref[:, 0:head]], axis=0)
                    s_h = _dot(qa, k_head)
                    m = jnp.maximum(m, jnp.max(s_h, axis=-1, keepdims=True))
                v_tail = jnp.concatenate([kvs_ref[0, v_rows(g), head:klen], epad_ref[:, head:klen]], axis=0)
                o = _dot_nt(jnp.exp(s_t - m).astype(BF16), v_tail)
                if head:
                    v_head = jnp.concatenate([kvs_ref[0, v_rows(g), 0:head], epad_ref[:, 0:head]], axis=0)
                    o = o + _dot_nt(jnp.exp(s_h - m).astype(BF16), v_head)
                os_scr[g] = o[:, 0:NSA_HD] / o[:, DEN_COL:DEN_COL + 1]

    n_wb = len(win_refs)
    win_blocks = win_refs[::-1]
    wpos = (i - (n_wb - 1)) * tq + lax.broadcasted_iota(jnp.int32, (tq, n_wb * tq), 1)
    dpos = pos_q - wpos
    wb = _tile_rows(jnp.where((dpos >= 0) & (dpos < WINDOW) & (wpos >= 0), 0.0, NEG), NSA_HEADS)
    k_w = jnp.concatenate([r[0, 0:w, :] for r in win_blocks], axis=1)
    v_w = jnp.concatenate([r[0, w:2 * w, :] for r in win_blocks], axis=1)
    s = _dot(qbd_scr[...], k_w) + wb
    e = jnp.exp(s - jnp.max(s, axis=-1, keepdims=True))
    o_w = _dot_nt(e.astype(BF16), v_w) / jnp.sum(e, axis=-1, keepdims=True)
    for h in range(NSA_HEADS):
        g, hh = divmod(h, NSA_HPG)
        rs = slice(h * tq, (h + 1) * tq)
        cs = slice(g * NSA_HD, (g + 1) * NSA_HD)
        o_h = (gates[:, 3 * h:3 * h + 1] * oc_scr[rs, cs]
               + gates[:, 3 * h + 1:3 * h + 2] * os_scr[g, hh * tq:(hh + 1) * tq, :]
               + gates[:, 3 * h + 2:3 * h + 3] * o_w[rs, cs])
        o_scr[:, h * NSA_HD:(h + 1) * NSA_HD] = o_h.astype(BF16)

    y_ref[0] = x_ref[0] + _dot(o_scr[...], wo_ref[...])


def _p_attn(q, gates, kc, vc, kvbf, mov_t, epad, wo_bf, x, tq):
    bsz, t, hq = q.shape
    nh = kc.shape[1]
    w = KV_W
    n_slc = mov_t.shape[0]
    kch = min(512, t)
    rows = NSA_HPG * tq
    n_wb = -(-WINDOW // tq) + 1
    assert kch % tq == 0 and n_wb * tq <= t
    eye4 = jnp.tile(jnp.eye(tq, dtype=BF16), (NSA_HPG, 1))
    tok = lambda b, i: (b, i, 0)
    win_specs = [pl.BlockSpec((1, 2 * w, tq), functools.partial(
        lambda b, i, k: (b, 1, jnp.maximum(i - k, 0)), k=k)) for k in range(n_wb)]
    return pl.pallas_call(
        functools.partial(_p_attn_kernel, tq=tq, t_all=t, kch=kch, n_wb=n_wb),
        grid=(bsz, t // tq),
        in_specs=[pl.BlockSpec((1, tq, hq), tok), pl.BlockSpec((1, tq, LANES), tok),
                  pl.BlockSpec((1, nh, w), lambda b, i: (b, 0, 0)),
                  pl.BlockSpec((1, nh, w), lambda b, i: (b, 0, 0)),
                  pl.BlockSpec((1, 2 * w, t), lambda b, i: (b, 0, 0))] + win_specs + [
                  _const_spec((n_slc, nh)), _const_spec((LANES - NSA_HD, t)),
                  _const_spec((rows, tq)), _const_spec((hq, D_MODEL)),
                  pl.BlockSpec((1, tq, D_MODEL), tok)],
        out_specs=pl.BlockSpec((1, tq, D_MODEL), tok),
        out_shape=jax.ShapeDtypeStruct((bsz, t, D_MODEL), F32),
        scratch_shapes=[pltpu.VMEM((tq, hq), BF16), pltpu.VMEM((NSA_KV, rows, LANES), BF16),
                        pltpu.VMEM((NSA_HEADS * tq, w), BF16), pltpu.VMEM((NSA_HEADS * tq, w), F32),
                        pltpu.VMEM((NSA_KV, rows, NSA_HD), F32)],
        compiler_params=_cparams(("parallel", "parallel")),
        name="p_attn",
    )(q, gates, kc, vc, kvbf, *([kvbf] * n_wb), mov_t, epad, eye4, wo_bf, x)


MAX_PAGES_COMPRESS = 32
MAX_PAGES_SLC = 32


def _s_compress_kernel(pt_ref, *refs, nh, page, pps):
    pages = refs[:pps]
    pe_ref, w1_ref, a_ref, b_ref, xt = refs[pps:]
    for k, pg in enumerate(pages):
        for c in range(2 * KV_W // LANES):
            xt[c, k * page:(k + 1) * page, :] = pg[0, c * LANES:(c + 1) * LANES, :].T
    a_k, b_k, a_v, b_v = _half_sums(xt, pe_ref, w1_ref, nh)
    a_ref[0] = jnp.concatenate([a_k, a_v], axis=1)
    b_ref[0] = jnp.concatenate([b_k, b_v], axis=1)


def _s_compress(cache_t, page_table, pe4, w1bd):
    bsz, n_pages = page_table.shape
    page = cache_t.shape[2]
    w = KV_W
    pps = min(MAX_PAGES_COMPRESS, n_pages)
    steps = n_pages // pps
    nh = pps * page // CMP_STRIDE
    page_specs = [pl.BlockSpec((1, 2 * w, page), functools.partial(
        lambda b, s, pt, k: (pt[b, s * pps + k], 0, 0), k=k)) for k in range(pps)]
    grid_spec = pltpu.PrefetchScalarGridSpec(
        num_scalar_prefetch=1,
        grid=(bsz, steps),
        in_specs=page_specs + [pl.BlockSpec((2, CMP_BLOCK, w), lambda b, s, pt: (0, 0, 0)),
                               pl.BlockSpec((2, 2, CMP_STRIDE * w, w), lambda b, s, pt: (0, 0, 0, 0))],
        out_specs=[pl.BlockSpec((1, nh, 2 * w), lambda b, s, pt: (b, s, 0)),
                   pl.BlockSpec((1, nh, 2 * w), lambda b, s, pt: (b, s, 0))],
        scratch_shapes=[pltpu.VMEM((2 * w // LANES, pps * page, LANES), F32)])
    out = jax.ShapeDtypeStruct((bsz, steps * nh, 2 * w), F32)
    return pl.pallas_call(
        functools.partial(_s_compress_kernel, nh=nh, page=page, pps=pps),
        grid_spec=grid_spec,
        out_shape=[out, out],
        compiler_params=_cparams(("parallel", "arbitrary")),
        name="s_compress",
    )(page_table, *([cache_t] * pps), pe4, w1bd)


def _s_cmp_kernel(a_ref, b_ref, q_ref, w2_ref, kg_ref, bd_ref, c_ref, s1_ref, s2_ref, mov_ref,
                  oc_ref, sel_ref, *, ts, past):
    ab = a_ref[0]
    bb = b_ref[0]
    w = KV_W
    kc, vc = _finish_compress(ab[:, 0:w], bb[:, 0:w], ab[:, w:2 * w], bb[:, w:2 * w], w2_ref, kg_ref,
                              bd_ref[...], c_ref[...], s1_ref[...], s2_ref[...])
    kc = kc.astype(BF16)
    vc = vc.astype(BF16)
    nh = kc.shape[0]
    q = q_ref[0]
    pos_q = past + lax.broadcasted_iota(jnp.int32, (ts, 1), 0)
    pos_c = lax.broadcasted_iota(jnp.int32, (ts, nh), 1) * CMP_STRIDE + (CMP_BLOCK - 1)
    ok_c = _tile_rows(pos_c <= pos_q, NSA_HPG)
    n_blk = mov_ref.shape[1]
    blk = lax.broadcasted_iota(jnp.int32, (ts, n_blk), 1)
    imps = []
    for g in range(NSA_KV):
        qg = _stack_heads(q, g).astype(BF16)
        gs = slice(g * NSA_HD, (g + 1) * NSA_HD)
        s = jnp.where(ok_c, _dot_nt(qg, kc[:, gs]), NEG)
        e = jnp.exp(s - jnp.max(s, axis=-1, keepdims=True))
        p = jnp.where(ok_c, e / jnp.sum(e, axis=-1, keepdims=True), 0.0)
        o_c = _dot(p.astype(BF16), vc[:, gs])
        for hh in range(NSA_HPG):
            h = g * NSA_HPG + hh
            oc_ref[0, :, h * NSA_HD:(h + 1) * NSA_HD] = o_c[hh * ts:(hh + 1) * ts]
        psum = p[0:ts]
        for hh in range(1, NSA_HPG):
            psum = psum + p[hh * ts:(hh + 1) * ts]
        imps.append(psum)
    imp = _dot_exact_rhs(jnp.concatenate(imps, axis=0), mov_ref[...].astype(BF16))
    imp = jnp.where(_tile_rows(blk, NSA_KV) == 0, BIG, imp)
    sel_ref[0] = jnp.where(_topk_mask(imp, SLC_TOP - 1), 1.0, 0.0)


def _s_cmp(a, b, q, w2bd, kgain2, ctabs, mov, past):
    bsz, nh, _ = a.shape
    ts = q.shape[1]
    hq = q.shape[2]
    c, s1, s2 = ctabs
    w = KV_W
    n_blk = mov.shape[1]
    per_b = lambda i: (i, 0, 0)
    return pl.pallas_call(
        functools.partial(_s_cmp_kernel, ts=ts, past=past),
        grid=(bsz,),
        in_specs=[pl.BlockSpec((1, nh, 2 * w), per_b), pl.BlockSpec((1, nh, 2 * w), per_b),
                  pl.BlockSpec((1, ts, hq), per_b),
                  _const_spec((2, w, w)), _const_spec((1, LANES)), _const_spec((LANES, LANES)),
                  _const_spec((nh, LANES)), _const_spec((nh, LANES)), _const_spec((nh, LANES)),
                  _const_spec((nh, n_blk))],
        out_specs=[pl.BlockSpec((1, ts, hq), per_b), pl.BlockSpec((1, NSA_KV * ts, n_blk), per_b)],
        out_shape=[jax.ShapeDtypeStruct((bsz, ts, hq), F32),
                   jax.ShapeDtypeStruct((bsz, NSA_KV * ts, n_blk), F32)],
        compiler_params=_cparams(("parallel",)),
        name="s_cmp",
    )(a, b, q, w2bd, kgain2, _seg_mean_matrix(), c, s1, s2, mov)


def _s_slc_kernel(pt_ref, *refs, ts, pps):
    pages = refs[:pps]
    q_ref, sel_ref, new_ref, exp_ref, o_ref, m_scr, l_scr, acc = refs[pps:]
    s_id = pl.program_id(1)
    w = KV_W
    rows = NSA_HPG * ts

    @pl.when(s_id == 0)
    def _():
        m_scr[...] = jnp.full(m_scr.shape, NEG, F32)
        l_scr[...] = jnp.zeros(l_scr.shape, F32)
        acc[...] = jnp.zeros(acc.shape, F32)

    q = q_ref[0]
    sel = sel_ref[0, 0]
    for g in range(NSA_KV):
        qg = _stack_heads(q, g).astype(BF16)
        k_t = jnp.concatenate([pg[0, g * NSA_HD:(g + 1) * NSA_HD, :] for pg in pages], axis=1).astype(BF16)
        v_t = jnp.concatenate([pg[0, w + g * NSA_HD:w + (g + 1) * NSA_HD, :] for pg in pages],
                              axis=1).astype(BF16)
        selk = _dot(sel[g * ts:(g + 1) * ts].astype(BF16), exp_ref[...])
        ok = _tile_rows(selk > 0.5, NSA_HPG)
        s = jnp.where(ok, _dot(qg, k_t), NEG)
        m_old = m_scr[g]
        m_new = jnp.maximum(m_old, jnp.max(s, axis=-1, keepdims=True))
        alpha = jnp.exp(m_old - m_new)
        e = jnp.where(ok, jnp.exp(s - m_new), 0.0)
        l_scr[g] = alpha * l_scr[g] + jnp.sum(e, axis=-1, keepdims=True)
        acc[g] = alpha * acc[g] + _dot_nt(e.astype(BF16), v_t)
        m_scr[g] = m_new

    @pl.when(s_id == pl.num_programs(1) - 1)
    def _():
        t_row = lax.broadcasted_iota(jnp.int32, (ts, ts), 0)
        t_col = lax.broadcasted_iota(jnp.int32, (ts, ts), 1)
        ok_n = _tile_rows(t_col <= t_row, NSA_HPG)
        for g in range(NSA_KV):
            qg = _stack_heads(q, g).astype(BF16)
            k_n = new_ref[0, g * NSA_HD:(g + 1) * NSA_HD, :].astype(BF16)
            v_n = new_ref[0, w + g * NSA_HD:w + (g + 1) * NSA_HD, :].astype(BF16)
            s = jnp.where(ok_n, _dot(qg, k_n), NEG)
            m_old = m_scr[g]
            m_new = jnp.maximum(m_old, jnp.max(s, axis=-1, keepdims=True))
            alpha = jnp.exp(m_old - m_new)
            e = jnp.where(ok_n, jnp.exp(s - m_new), 0.0)
            l_fin = alpha * l_scr[g] + jnp.sum(e, axis=-1, keepdims=True)
            o = (alpha * acc[g] + _dot_nt(e.astype(BF16), v_n)) / l_fin
            for hh in range(NSA_HPG):
                h = g * NSA_HPG + hh
                o_ref[0, :, h * NSA_HD:(h + 1) * NSA_HD] = o[hh * ts:(hh + 1) * ts]


def _s_slc(cache_t, page_table, q, sel_steps, new_t, expand, pps):
    bsz, n_pages = page_table.shape
    page = cache_t.shape[2]
    ts, hq = q.shape[1], q.shape[2]
    w = KV_W
    steps = n_pages // pps
    bps = sel_steps.shape[3]
    page_specs = [pl.BlockSpec((1, 2 * w, page), functools.partial(
        lambda b, s, pt, k: (pt[b, s * pps + k], 1, 0), k=k)) for k in range(pps)]
    rows = NSA_HPG * ts
    grid_spec = pltpu.PrefetchScalarGridSpec(
        num_scalar_prefetch=1,
        grid=(bsz, steps),
        in_specs=page_specs + [
            pl.BlockSpec((1, ts, hq), lambda b, s, pt: (b, 0, 0)),
            pl.BlockSpec((1, 1, NSA_KV * ts, bps), lambda b, s, pt: (b, s, 0, 0)),
            pl.BlockSpec((1, 2 * w, ts), lambda b, s, pt: (b, 0, 0)),
            pl.BlockSpec((bps, pps * page), lambda b, s, pt: (0, 0))],
        out_specs=pl.BlockSpec((1, ts, hq), lambda b, s, pt: (b, 0, 0)),
        scratch_shapes=[pltpu.VMEM((NSA_KV, rows, 1), F32), pltpu.VMEM((NSA_KV, rows, 1), F32),
                        pltpu.VMEM((NSA_KV, rows, NSA_HD), F32)])
    return pl.pallas_call(
        functools.partial(_s_slc_kernel, ts=ts, pps=pps),
        grid_spec=grid_spec,
        out_shape=jax.ShapeDtypeStruct((bsz, ts, hq), F32),
        compiler_params=_cparams(("parallel", "arbitrary")),
        name="s_slc",
    )(page_table, *([cache_t] * pps), q, sel_steps, new_t, expand)


def _s_win_kernel(q_ref, gate_ref, oc_ref, os_ref, cache_ref, new_ref, o_ref, wout_ref, *, ts):
    q = q_ref[0]
    gates = gate_ref[0]
    cache = cache_ref[0]
    new = new_ref[0]
    wb = cache.shape[1]
    w = KV_W
    t_row = lax.broadcasted_iota(jnp.int32, (ts, wb), 0)
    idx = lax.broadcasted_iota(jnp.int32, (ts, wb), 1)
    dpos = t_row + wb - idx
    ok_old = _tile_rows((dpos >= 0) & (dpos < WINDOW), NSA_HPG)
    n_row = lax.broadcasted_iota(jnp.int32, (ts, ts), 0)
    n_col = lax.broadcasted_iota(jnp.int32, (ts, ts), 1)
    ok_new = _tile_rows((n_col <= n_row) & (n_row - n_col < WINDOW), NSA_HPG)
    for g in range(NSA_KV):
        qg = _stack_heads(q, g).astype(BF16)
        gk = slice(g * NSA_HD, (g + 1) * NSA_HD)
        gv = slice(w + g * NSA_HD, w + (g + 1) * NSA_HD)
        s_old = jnp.where(ok_old, _dot(qg, cache[gk].astype(BF16)), NEG)
        s_new = jnp.where(ok_new, _dot(qg, new[gk].astype(BF16)), NEG)
        m = jnp.maximum(jnp.max(s_old, axis=-1, keepdims=True), jnp.max(s_new, axis=-1, keepdims=True))
        e_old = jnp.exp(s_old - m)
        e_new = jnp.exp(s_new - m)
        den = jnp.sum(e_old, axis=-1, keepdims=True) + jnp.sum(e_new, axis=-1, keepdims=True)
        o_w = (_dot_nt(e_old.astype(BF16), cache[gv].astype(BF16))
               + _dot_nt(e_new.astype(BF16), new[gv].astype(BF16))) / den
        for hh in range(NSA_HPG):
            h = g * NSA_HPG + hh
            hs = slice(h * NSA_HD, (h + 1) * NSA_HD)
            o_ref[0, :, hs] = (gates[:, 3 * h:3 * h + 1] * oc_ref[0, :, hs]
                               + gates[:, 3 * h + 1:3 * h + 2] * os_ref[0, :, hs]
                               + gates[:, 3 * h + 2:3 * h + 3] * o_w[hh * ts:(hh + 1) * ts])
    wout_ref[0] = pltpu.roll(cache, wb - ts, 1)
    wout_ref[0, :, wb - ts:wb] = new


def _s_win(q, gates, o_c, o_s, cache_win_t, new_win_t):
    bsz, ts, hq = q.shape
    wb = cache_win_t.shape[2]
    w = KV_W
    per_b = lambda b: (b, 0, 0)
    return pl.pallas_call(
        functools.partial(_s_win_kernel, ts=ts),
        grid=(bsz,),
        in_specs=[pl.BlockSpec((1, ts, hq), per_b), pl.BlockSpec((1, ts, LANES), per_b),
                  pl.BlockSpec((1, ts, hq), per_b), pl.BlockSpec((1, ts, hq), per_b),
                  pl.BlockSpec((1, 2 * w, wb), per_b), pl.BlockSpec((1, 2 * w, ts), per_b)],
        out_specs=[pl.BlockSpec((1, ts, hq), per_b), pl.BlockSpec((1, 2 * w, wb), per_b)],
        out_shape=[jax.ShapeDtypeStruct((bsz, ts, hq), F32),
                   jax.ShapeDtypeStruct((bsz, 2 * w, wb), F32)],
        compiler_params=_cparams(("parallel",)),
        name="s_win",
    )(q, gates, o_c, o_s, cache_win_t, new_win_t)


def _block_overlap(n_cmp_rows, n_cmp, n_slc):
    cs = jnp.arange(n_cmp_rows)[:, None] * CMP_STRIDE
    ss = jnp.arange(n_slc)[None, :] * SLC_BLOCK
    ov = jnp.minimum(cs + CMP_BLOCK, ss + SLC_BLOCK) - jnp.maximum(cs, ss)
    ov = jnp.clip(ov, 0, None).astype(F32) / CMP_BLOCK
    return jnp.where(jnp.arange(n_cmp_rows)[:, None] < n_cmp, ov, 0.0)


def _expand_matrix(n_blk, n_keys):
    return (jnp.arange(n_keys)[None, :] // SLC_BLOCK == jnp.arange(n_blk)[:, None]).astype(BF16)


def _pad_rows(a, rows):
    return jnp.pad(a, ((0, rows - a.shape[0]), (0, 0)))


def _pad_lanes(a, lanes=LANES):
    return jnp.pad(a, ((0, 0), (0, lanes - a.shape[1])))


def _blockdiag4(m):
    eye = jnp.eye(NSA_KV, dtype=m.dtype)
    out = jnp.einsum('gh,...de->...gdhe', eye, m)
    return out.reshape(m.shape[:-2] + (KV_W, KV_W))


def _layer_tail(x, layer, kv4, tq_mem, tm_ffn, p):
    bsz, t, d = x.shape
    x = _memattn(x, tq_mem, p['norm_mem'][layer][None], p['mem_w_q'][layer].astype(BF16),
                 p['mem_q_norm'][layer][None], kv4, layer, p['mem_w_o'][layer].astype(BF16))
    x2 = _ffn(x.reshape(bsz * t, d), tm_ffn, p['norm_ffn'][layer][None],
              p['ffn_w_gate'][layer].astype(BF16), p['ffn_w_up'][layer].astype(BF16),
              p['ffn_w_down'][layer].astype(BF16))
    return x2.reshape(bsz, t, d)


def _run_mixer0(x, tq, sconv_st, ssdc_st, ssd_st, p):
    w_in = _pad_lanes(p['ab_w_in'][0], IN0_PAD).astype(BF16)
    front = lambda s: jnp.pad(s, ((0, 0), (CARRY - s.shape[1], 0), (0, 0)))
    y, sa, sb, hn = _mixer0(
        x, tq, p['norm_mix'][0][None], w_in,
        _pad_rows(p['ab_sconv_w'][0], SUBLANES), _pad_rows(p['ab_ssd_conv_w'][0], SUBLANES),
        p['ab_ssd_conv_b'], _pad_lanes(p['ab_dt_bias']), _pad_lanes(p['ab_a_log']),
        _pad_lanes(p['ab_d']), p['ab_ssd_norm'], p['ab_w_out'][0].astype(BF16),
        front(sconv_st), front(ssdc_st), ssd_st)
    return y, sa[:, CARRY - (SCONV_W - 1):], sb[:, CARRY - (SSD_CONV_W - 1):], hn


def _nsa_weights(p):
    w_in = p['nsa_w_in'][0]
    hq = NSA_HEADS * NSA_HD
    wq = w_in[:, :hq].astype(BF16)
    wkv_t = w_in[:, hq:hq + 6 * KV_W].T.astype(BF16)
    wg = _pad_lanes(w_in[:, hq + 6 * KV_W:]).astype(BF16)
    qgain2 = jnp.tile(p['nsa_q_norm'][0], 2)[None]
    kn = p['nsa_k_norm'][0]
    kgain = jnp.stack([kn[1], kn[2]])[:, :, None]
    kcgain2 = jnp.tile(kn[0], 2)[None]
    pe4 = jnp.tile(p['nsa_cmp_pe'][0], (1, 1, NSA_KV))
    w1bd = _blockdiag4(p['nsa_cmp_w1'][0]).astype(BF16)
    w1bd = w1bd.reshape(2, 2, CMP_STRIDE * KV_W, KV_W)
    w2bd = _blockdiag4(p['nsa_cmp_w2'][0]).astype(BF16)
    return wq, wg, wkv_t, qgain2, kgain, kcgain2, pe4, w1bd, w2bd


def kernel(x_prompt, x_sample, mem_prompt, state_sconv, state_ssd_conv, state_ssd, cache_nsa_kv,
           cache_nsa_win, cache_mem_kv, page_table, norm_mix, norm_mem, norm_memsrc, norm_ffn,
           ab_w_in, ab_sconv_w, ab_ssd_conv_w, ab_ssd_conv_b, ab_dt_bias, ab_a_log, ab_d, ab_ssd_norm,
           ab_w_out, nsa_w_in, nsa_q_norm, nsa_k_norm, nsa_cmp_pe, nsa_cmp_w1, nsa_cmp_w2, nsa_w_out,
           mem_w_q, mem_q_norm, mem_w_kv, mem_k_norm, mem_w_o, ffn_w_gate, ffn_w_up, ffn_w_down):
    p = dict(norm_mix=norm_mix, norm_mem=norm_mem, norm_ffn=norm_ffn, ab_w_in=ab_w_in,
             ab_sconv_w=ab_sconv_w, ab_ssd_conv_w=ab_ssd_conv_w, ab_ssd_conv_b=ab_ssd_conv_b,
             ab_dt_bias=ab_dt_bias, ab_a_log=ab_a_log, ab_d=ab_d, ab_ssd_norm=ab_ssd_norm,
             ab_w_out=ab_w_out, nsa_w_in=nsa_w_in, nsa_q_norm=nsa_q_norm, nsa_k_norm=nsa_k_norm,
             nsa_cmp_pe=nsa_cmp_pe, nsa_cmp_w1=nsa_cmp_w1, nsa_cmp_w2=nsa_cmp_w2, nsa_w_out=nsa_w_out,
             mem_w_q=mem_w_q, mem_q_norm=mem_q_norm, mem_w_o=mem_w_o,
             ffn_w_gate=ffn_w_gate, ffn_w_up=ffn_w_up, ffn_w_down=ffn_w_down)
    bp, t, d = x_prompt.shape
    bs, ts, _ = x_sample.shape
    n_mem = mem_prompt.shape[1]
    depth = norm_mix.shape[0]
    assert depth == 2 and ab_w_in.shape[0] == 1 and nsa_w_in.shape[0] == 1
    w = KV_W
    hq = NSA_HEADS * NSA_HD
    wq, wg, wkv_t, qgain2, kgain, kcgain2, pe4, w1bd, w2bd = _nsa_weights(p)
    wo_nsa = nsa_w_out[0].astype(BF16)

    mem2d = mem_prompt.reshape(bp * n_mem, d)
    p_mem = _memkv(mem2d, norm_memsrc[:, None], mem_w_kv.astype(BF16), mem_k_norm[:, None])
    p_mem = p_mem.reshape(depth, bp, n_mem * MEM_SLOTS, MEM_HD)
    p_mem_kv = p_mem.reshape(depth, bp, n_mem, 2, MEM_HEADS, MEM_HD)
    s_mem = cache_mem_kv.reshape(depth, bs, n_mem * MEM_SLOTS, MEM_HD)

    tq0 = min(256, t)
    tq_mem = min(512, t)
    tm_ffn = min(512, bp * t)
    zeros = lambda *s: jnp.zeros(s, F32)
    x, p_sconv, p_ssdc, p_ssd = _run_mixer0(
        x_prompt, tq0, zeros(bp, SCONV_W - 1, d), zeros(bp, SSD_CONV_W - 1, SSD_CONV_DIM),
        zeros(bp, SSD_HEADS, SSD_HD, SSD_N), p)
    x = _layer_tail(x, 0, p_mem, tq_mem, tm_ffn, p)

    pos_p = jnp.arange(t)
    tabs_p = _rope_tables(pos_p)
    tm_proj = min(512, t)
    q, gates, rows_t, win_t, kvbf = _nsa_proj(x, tm_proj, norm_mix[1][None], wq, wg, wkv_t, qgain2,
                                              kgain, tabs_p)
    nh = t // CMP_STRIDE
    n_cmp = nh - 1
    pos_c = jnp.arange(nh) * CMP_STRIDE + CMP_BLOCK - 1
    kc, vc = _p_compress(rows_t, pe4, w1bd, w2bd, kcgain2, _rope_tables(pos_c)[:3])
    n_slc = -(-t // SLC_BLOCK)
    tq_a = min(128, t)
    assert n_slc <= SEL_COLS
    epad = jnp.concatenate([_pad_rows(_expand_matrix(n_slc, t), SEL_COLS), jnp.ones((1, t), BF16),
                            jnp.zeros((LANES - DEN_COL - 1, t), BF16)], axis=0)
    x = _p_attn(q, gates, kc, vc, kvbf, _block_overlap(nh, n_cmp, n_slc).T, epad, wo_nsa, x, tq_a)
    y_prompt = _layer_tail(x, 1, p_mem, tq_mem, tm_ffn, p)
    p_rows = rows_t.reshape(bp, 1, 4, NSA_KV, NSA_HD, t).transpose(0, 5, 1, 2, 3, 4)
    wl = min(WINDOW, t)
    p_win = win_t[:, :, t - wl:].reshape(1, bp, 2, NSA_KV, NSA_HD, wl).transpose(0, 1, 5, 2, 3, 4)

    n_pool, page = cache_nsa_kv.shape[0], cache_nsa_kv.shape[1]
    n_pages = page_table.shape[1]
    past = n_pages * page
    pps_slc = min(MAX_PAGES_SLC, n_pages)
    assert ts <= CMP_STRIDE and past % SLC_BLOCK == 0
    assert n_pages % pps_slc == 0 and n_pages % min(MAX_PAGES_COMPRESS, n_pages) == 0
    xs, s_sconv, s_ssdc, s_ssd = _run_mixer0(x_sample, ts, state_sconv[0], state_ssd_conv[0],
                                             state_ssd[0], p)
    xs = _layer_tail(xs, 0, s_mem, ts, bs * ts, p)

    pos_s = jnp.tile(past + jnp.arange(ts), bs)
    qs, gates_s, rows_s, win_s, _ = _nsa_proj(xs.reshape(1, bs * ts, d), bs * ts, norm_mix[1][None],
                                              wq, wg, wkv_t, qgain2, kgain, _rope_tables(pos_s))
    qs = qs.reshape(bs, ts, hq)
    gates_s = gates_s.reshape(bs, ts, LANES)
    rows_s = rows_s[0].reshape(4 * w, bs, ts).transpose(1, 0, 2)
    win_s = win_s[0].reshape(2 * w, bs, ts).transpose(1, 0, 2)
    cache_t = cache_nsa_kv.transpose(0, 2, 3, 4, 5, 1).reshape(n_pool, 4 * w, page)
    a_sum, b_sum = _s_compress(cache_t, page_table, pe4, w1bd)
    nh_s = past // CMP_STRIDE
    n_cmp_s = (past + ts) // CMP_STRIDE - 1
    pos_cs = jnp.arange(nh_s) * CMP_STRIDE + CMP_BLOCK - 1
    n_blk = past // SLC_BLOCK
    o_c, sel = _s_cmp(a_sum, b_sum, qs, w2bd, kcgain2, _rope_tables(pos_cs)[:3],
                      _block_overlap(nh_s, n_cmp_s, n_blk), past)
    steps = n_pages // pps_slc
    bps = n_blk // steps
    sel_steps = sel.reshape(bs, NSA_KV * ts, steps, bps).transpose(0, 2, 1, 3)
    o_s = _s_slc(cache_t, page_table, qs, sel_steps, rows_s[:, 2 * w:],
                 _expand_matrix(bps, pps_slc * page), pps_slc)
    wb = cache_nsa_win.shape[2]
    cache_win_t = cache_nsa_win[0].transpose(0, 2, 3, 4, 1).reshape(bs, 2 * w, wb)
    o_att, new_win_t = _s_win(qs, gates_s, o_c, o_s, cache_win_t, win_s)
    xs = _proj_res(o_att.reshape(bs * ts, hq), wo_nsa, xs.reshape(bs * ts, d)).reshape(bs, ts, d)
    y_sample = _layer_tail(xs, 1, s_mem, ts, bs * ts, p)
    s_rows = rows_s.transpose(0, 2, 1).reshape(bs, ts, 1, 4, NSA_KV, NSA_HD)
    s_win = new_win_t.reshape(1, bs, 2, NSA_KV, NSA_HD, wb).transpose(0, 1, 5, 2, 3, 4)

    return (y_prompt, y_sample, p_sconv[None], p_ssdc[None], p_ssd[None], p_rows, p_win, p_mem_kv,
            s_sconv[None], s_ssdc[None], s_ssd[None], s_rows, s_win)
```

```python
import functools

import jax
import jax.numpy as jnp
from jax import lax
from jax.experimental import pallas as pl
from jax.experimental.pallas import tpu as pltpu

F32 = jnp.float32
BF16 = jnp.bfloat16

EPS = 1e-6
ROPE_THETA = 500000.0
NEG = -1e30
BIG = 1e9

LANES = 128
SUBLANES = 8
VMEM_LIMIT = 56 * 1024 * 1024

D_MODEL = 1024
SCONV_W = 3
SSD_HEADS = 16
SSD_HD = 64
SSD_DIM = SSD_HEADS * SSD_HD
SSD_GROUPS = 4
SSD_HPG = SSD_HEADS // SSD_GROUPS
SSD_N = 128
SSD_CONV_W = 4
SSD_CONV_DIM = SSD_DIM + 2 * SSD_GROUPS * SSD_N
IN0_DIM = 3 * D_MODEL + SSD_DIM + SSD_CONV_DIM + SSD_HEADS
IN0_PAD = 3 * D_MODEL + SSD_DIM + SSD_CONV_DIM + LANES
NSA_HEADS = 16
NSA_KV = 4
NSA_HD = 64
NSA_HPG = NSA_HEADS // NSA_KV
ROT_HALF = NSA_HD // 8
CMP_BLOCK = 32
CMP_STRIDE = 16
SLC_BLOCK = 64
SLC_TOP = 16
WINDOW = 512
KV_W = NSA_KV * NSA_HD
MEM_HEADS = 4
MEM_HD = 128
CARRY = SUBLANES


def _cparams(sem):
    return pltpu.CompilerParams(dimension_semantics=sem, vmem_limit_bytes=VMEM_LIMIT)


def _const_spec(shape, single_buffer=False):
    n = len(shape)
    if single_buffer:
        return pl.BlockSpec(shape, lambda *_: (0,) * n, pipeline_mode=pl.Buffered(1))
    return pl.BlockSpec(shape, lambda *_: (0,) * n)


def _rms(x, g):
    ms = jnp.mean(x * x, axis=-1, keepdims=True)
    return x * lax.rsqrt(ms + EPS) * g


def _dot(a, b):
    return jnp.dot(a, b, preferred_element_type=F32)


def _dot_nt(a, b):
    return lax.dot_general(a, b, (((1,), (1,)), ((), ())), preferred_element_type=F32)


def _dot_tn(a, b):
    return lax.dot_general(a, b, (((0,), (0,)), ((), ())), preferred_element_type=F32)


def _split_bf16(x, terms):
    parts = []
    for _ in range(terms - 1):
        p = x.astype(BF16)
        parts.append(p)
        x = x - p.astype(F32)
    parts.append(x.astype(BF16))
    return parts


def _dot_exact_lhs(a_bf, x, terms=3):
    out = None
    for p in _split_bf16(x, terms):
        d = _dot(a_bf, p)
        out = d if out is None else out + d
    return out


def _dot_nt_exact_lhs(a_bf, x, terms=3):
    out = None
    for p in _split_bf16(x, terms):
        d = _dot_nt(a_bf, p)
        out = d if out is None else out + d
    return out


def _dot_exact_rhs(x, b_bf, terms=3):
    out = None
    for p in _split_bf16(x, terms):
        d = _dot(p, b_bf)
        out = d if out is None else out + d
    return out


def _silu(x):
    return x * jax.nn.sigmoid(x)


def _softplus(x):
    return jnp.maximum(x, 0.0) + jnp.log1p(jnp.exp(-jnp.abs(x)))


def _seg_mean_matrix():
    r = lax.broadcasted_iota(jnp.int32, (LANES, LANES), 0) // NSA_HD
    c = lax.broadcasted_iota(jnp.int32, (LANES, LANES), 1) // NSA_HD
    return jnp.where(r == c, 1.0 / NSA_HD, 0.0).astype(F32)


def _headnorm_rope_tok(x, gain, bd, c, s1, s2):
    ms = _dot_exact_rhs(x * x, bd.astype(BF16), terms=2)
    xn = x * lax.rsqrt(ms + EPS) * gain
    return xn * c + pltpu.roll(xn, LANES - ROT_HALF, 1) * s1 + pltpu.roll(xn, ROT_HALF, 1) * s2


def _headnorm_rope_ch(x, gain_col, cos_t, sin_t):
    ms = jnp.mean(x * x, axis=0, keepdims=True)
    xn = x * lax.rsqrt(ms + EPS) * gain_col
    x1 = xn[0:ROT_HALF]
    x2 = xn[ROT_HALF:2 * ROT_HALF]
    return x1 * cos_t - x2 * sin_t, x2 * cos_t + x1 * sin_t, xn[2 * ROT_HALF:]


def _rope_tables(pos):
    inv = ROPE_THETA ** (-jnp.arange(ROT_HALF, dtype=F32) / ROT_HALF)
    ang = pos.astype(F32)[:, None] * inv[None, :]
    cos, sin = jnp.cos(ang), jnp.sin(ang)
    n = pos.shape[0]
    ones = jnp.ones((n, NSA_HD - 2 * ROT_HALF), F32)
    zeros8 = jnp.zeros((n, ROT_HALF), F32)
    zeros = jnp.zeros((n, NSA_HD - 2 * ROT_HALF), F32)
    c = jnp.concatenate([cos, cos, ones], axis=1)
    s1 = jnp.concatenate([-sin, zeros8, zeros], axis=1)
    s2 = jnp.concatenate([zeros8, sin, zeros], axis=1)
    tile2 = lambda t: jnp.concatenate([t, t], axis=1)
    return tile2(c), tile2(s1), tile2(s2), cos.T, sin.T


MEM_SLOTS = 2 * MEM_HEADS


def _memkv_kernel(x_ref, g_ref, w_ref, kn_ref, o_ref, *, tm):
    hn = _rms(x_ref[...], g_ref[0]).astype(BF16)
    kv = _dot(hn, w_ref[0])
    for c in range(MEM_SLOTS):
        blk = kv[:, c * MEM_HD:(c + 1) * MEM_HD]
        if c < MEM_HEADS:
            blk = _rms(blk, kn_ref[0])
        o_ref[0, pl.ds(c, tm, stride=MEM_SLOTS), :] = blk


def _memkv(mem2d, g, w_bf, kn):
    m = mem2d.shape[0]
    depth = w_bf.shape[0]
    tm = min(512, m)
    n = w_bf.shape[2]
    return pl.pallas_call(
        functools.partial(_memkv_kernel, tm=tm),
        grid=(depth, m // tm),
        in_specs=[pl.BlockSpec((tm, D_MODEL), lambda l, i: (i, 0)),
                  pl.BlockSpec((1, 1, D_MODEL), lambda l, i: (l, 0, 0)),
                  pl.BlockSpec((1, D_MODEL, n), lambda l, i: (l, 0, 0)),
                  pl.BlockSpec((1, 1, MEM_HD), lambda l, i: (l, 0, 0))],
        out_specs=pl.BlockSpec((1, tm * MEM_SLOTS, MEM_HD), lambda l, i: (l, i, 0)),
        out_shape=jax.ShapeDtypeStruct((depth, m * MEM_SLOTS, MEM_HD), F32),
        compiler_params=_cparams(("parallel", "parallel")),
        name="memkv",
    )(mem2d, g, w_bf, kn)


def _mixer0_project(x_ref, g_ref, win_ref, proj, n_col_chunks):
    hn = _rms(x_ref[0], g_ref[...]).astype(BF16)
    cw_cols = IN0_PAD // n_col_chunks
    for c in range(n_col_chunks):
        sl = slice(c * cw_cols, (c + 1) * cw_cols)
        proj[:, sl] = _dot(hn, win_ref[:, sl])


def _mixer0_mix(proj, x_ref, scw_ref, cw_ref, cb_ref, dtb_ref, alog_ref, dsk_ref, ssdn_ref, wout_ref,
                y_ref, ca, cbuf, hst, ycat, tq):
    x = x_ref[0]
    d = D_MODEL
    ca[CARRY:CARRY + tq, :] = proj[:, 2 * d:3 * d] * proj[:, 0:d]
    conv_u = ca[pl.ds(CARRY - 2, tq), :] * scw_ref[0:1, :]
    for k in range(1, SCONV_W):
        conv_u = conv_u + ca[pl.ds(CARRY - 2 + k, tq), :] * scw_ref[k:k + 1, :]
    ycat[:, 0:d] = (proj[:, d:2 * d] * conv_u).astype(BF16)

    x0 = 3 * d + SSD_DIM
    cbuf[CARRY:CARRY + tq, :] = proj[:, x0:x0 + SSD_CONV_DIM]
    xbc = cbuf[pl.ds(CARRY - 3, tq), :] * cw_ref[0:1, :]
    for k in range(1, SSD_CONV_W):
        xbc = xbc + cbuf[pl.ds(CARRY - 3 + k, tq), :] * cw_ref[k:k + 1, :]
    xbc = _silu(xbc + cb_ref[...])

    dt = _softplus(proj[:, x0 + SSD_CONV_DIM:IN0_PAD] + dtb_ref[...])
    a_neg = -jnp.exp(alog_ref[...])
    row = lax.broadcasted_iota(jnp.int32, (tq, tq), 0)
    col = lax.broadcasted_iota(jnp.int32, (tq, tq), 1)
    causal = row >= col
    tril = jnp.where(causal, 1.0, 0.0).astype(BF16)
    cum = _dot_exact_lhs(tril, dt * a_neg)
    eye = jnp.where(lax.broadcasted_iota(jnp.int32, (LANES, LANES), 0)
                    == lax.broadcasted_iota(jnp.int32, (LANES, LANES), 1), 1.0, 0.0).astype(BF16)
    cum_t = _dot_nt_exact_lhs(eye, cum)
    cum_last = cum[tq - 1:tq, :]
    e_cum = jnp.exp(cum)
    e_last = jnp.exp(cum_last)
    e_end = jnp.exp(cum_last - cum)
    z = proj[:, 3 * d:3 * d + SSD_DIM]
    dsk = dsk_ref[...]

    for g in range(SSD_GROUPS):
        b_g = xbc[:, SSD_DIM + g * SSD_N:SSD_DIM + (g + 1) * SSD_N].astype(BF16)
        c_g = xbc[:, SSD_DIM + (SSD_GROUPS + g) * SSD_N:
                  SSD_DIM + (SSD_GROUPS + g + 1) * SSD_N].astype(BF16)
        cb_g = _dot_nt(c_g, b_g)
        y_parts = []
        for hh in range(SSD_HPG):
            h = g * SSD_HPG + hh
            x_h = xbc[:, h * SSD_HD:(h + 1) * SSD_HD]
            xdt = x_h * dt[:, h:h + 1]
            lmat = jnp.where(causal, jnp.exp(cum[:, h:h + 1] - cum_t[h:h + 1, :]), 0.0)
            y_h = _dot((cb_g * lmat).astype(BF16), xdt.astype(BF16))
            st = hst[h]
            y_h = y_h + e_cum[:, h:h + 1] * _dot_nt(c_g, st.astype(BF16))
            hst[h] = e_last[:, h:h + 1] * st + _dot_tn((xdt * e_end[:, h:h + 1]).astype(BF16), b_g)
            y_parts.append(y_h + dsk[:, h:h + 1] * x_h)
        yg = jnp.concatenate(y_parts, axis=1)
        gs = slice(g * SSD_HPG * SSD_HD, (g + 1) * SSD_HPG * SSD_HD)
        yg = yg * _silu(z[:, gs])
        yg = yg * lax.rsqrt(jnp.mean(yg * yg, axis=-1, keepdims=True) + EPS)
        ycat[:, d + gs.start:d + gs.stop] = (yg * ssdn_ref[:, gs]).astype(BF16)

    y_ref[0] = x + _dot(ycat[...], wout_ref[...])

    ca[0:CARRY, :] = ca[tq:tq + CARRY, :]
    cbuf[0:CARRY, :] = cbuf[tq:tq + CARRY, :]


def _mixer0_kernel(x_ref, g_ref, win_ref, scw_ref, cw_ref, cb_ref, dtb_ref, alog_ref,
                   dsk_ref, ssdn_ref, wout_ref, sa0_ref, sb0_ref, h0_ref,
                   y_ref, sa_ref, sb_ref, hout_ref,
                   proj, ca, cbuf, hst, ycat, *, tq, n_col_chunks):
    i = pl.program_id(1)

    @pl.when(i == 0)
    def _():
        ca[0:CARRY, :] = sa0_ref[0]
        cbuf[0:CARRY, :] = sb0_ref[0]
        hst[...] = h0_ref[0]

    _mixer0_project(x_ref, g_ref, win_ref, proj, n_col_chunks)
    _mixer0_mix(proj, x_ref, scw_ref, cw_ref, cb_ref, dtb_ref, alog_ref, dsk_ref, ssdn_ref,
                wout_ref, y_ref, ca, cbuf, hst, ycat, tq)

    @pl.when(i == pl.num_programs(1) - 1)
    def _():
        sa_ref[0] = ca[0:CARRY, :]
        sb_ref[0] = cbuf[0:CARRY, :]
        hout_ref[0] = hst[...]


def _mixer0(x, tq, g, w_in_bf, scw, cw, cb, dtb, alog, dsk, ssdn, w_out_bf, sa0, sb0, h0):
    bsz, t, d = x.shape
    n_col_chunks = 7
    kern = functools.partial(_mixer0_kernel, tq=tq, n_col_chunks=n_col_chunks)
    per_b = lambda b, i: (b, 0, 0)
    return pl.pallas_call(
        kern,
        grid=(bsz, t // tq),
        in_specs=[pl.BlockSpec((1, tq, d), lambda b, i: (b, i, 0)),
                  _const_spec((1, d)), _const_spec((d, IN0_PAD)),
                  _const_spec((SUBLANES, d)), _const_spec((SUBLANES, SSD_CONV_DIM)),
                  _const_spec((1, SSD_CONV_DIM)), _const_spec((1, LANES)), _const_spec((1, LANES)),
                  _const_spec((1, LANES)), _const_spec((1, SSD_DIM)),
                  _const_spec((d + SSD_DIM, d)),
                  pl.BlockSpec((1, CARRY, d), per_b),
                  pl.BlockSpec((1, CARRY, SSD_CONV_DIM), per_b),
                  pl.BlockSpec((1, SSD_HEADS, SSD_HD, SSD_N), lambda b, i: (b, 0, 0, 0))],
        out_specs=[pl.BlockSpec((1, tq, d), lambda b, i: (b, i, 0)),
                   pl.BlockSpec((1, CARRY, d), per_b),
                   pl.BlockSpec((1, CARRY, SSD_CONV_DIM), per_b),
                   pl.BlockSpec((1, SSD_HEADS, SSD_HD, SSD_N), lambda b, i: (b, 0, 0, 0))],
        out_shape=[jax.ShapeDtypeStruct((bsz, t, d), F32),
                   jax.ShapeDtypeStruct((bsz, CARRY, d), F32),
                   jax.ShapeDtypeStruct((bsz, CARRY, SSD_CONV_DIM), F32),
                   jax.ShapeDtypeStruct((bsz, SSD_HEADS, SSD_HD, SSD_N), F32)],
        scratch_shapes=[pltpu.VMEM((tq, IN0_PAD), F32),
                        pltpu.VMEM((CARRY + tq, d), F32),
                        pltpu.VMEM((CARRY + tq, SSD_CONV_DIM), F32),
                        pltpu.VMEM((SSD_HEADS, SSD_HD, SSD_N), F32),
                        pltpu.VMEM((tq, d + SSD_DIM), BF16)],
        compiler_params=_cparams(("parallel", "arbitrary")),
        name="mixer0",
    )(x, g, w_in_bf, scw, cw, cb, dtb, alog, dsk, ssdn, w_out_bf, sa0, sb0, h0)


def _memattn_body(x, g_ref, wq_ref, qn_ref, kv_ref, wo_ref, n_mem):
    hn = _rms(x, g_ref[...]).astype(BF16)
    q = _dot(hn, wq_ref[...])
    outs = []
    for h in range(MEM_HEADS):
        sl = slice(h * MEM_HD, (h + 1) * MEM_HD)
        k_h = kv_ref[0, 0, pl.ds(h, n_mem, stride=MEM_SLOTS), :].astype(BF16)
        v_h = kv_ref[0, 0, pl.ds(MEM_HEADS + h, n_mem, stride=MEM_SLOTS), :].astype(BF16)
        qh = _rms(q[:, sl], qn_ref[...]).astype(BF16)
        s = _dot_nt(qh, k_h) * (MEM_HD ** -0.5)
        m = jnp.max(s, axis=-1, keepdims=True)
        e = jnp.exp(s - m)
        p = e / jnp.sum(e, axis=-1, keepdims=True)
        outs.append(_dot(p.astype(BF16), v_h))
    o = jnp.concatenate(outs, axis=1).astype(BF16)
    return x + _dot(o, wo_ref[...])


def _memattn_kernel(x_ref, g_ref, wq_ref, qn_ref, kv_ref, wo_ref, o_ref, *, n_mem):
    o_ref[0] = _memattn_body(x_ref[0], g_ref, wq_ref, qn_ref, kv_ref, wo_ref, n_mem)


def _memattn(x, tq, g, wq_bf, qn, kv4, layer, wo_bf):
    bsz, t, d = x.shape
    rows = kv4.shape[2]
    hw = MEM_HEADS * MEM_HD
    return pl.pallas_call(
        functools.partial(_memattn_kernel, n_mem=rows // MEM_SLOTS),
        grid=(bsz, t // tq),
        in_specs=[pl.BlockSpec((1, tq, d), lambda b, i: (b, i, 0)),
                  _const_spec((1, d)), _const_spec((d, hw)), _const_spec((1, MEM_HD)),
                  pl.BlockSpec((1, 1, rows, MEM_HD), lambda b, i: (layer, b, 0, 0)),
                  _const_spec((hw, d))],
        out_specs=pl.BlockSpec((1, tq, d), lambda b, i: (b, i, 0)),
        out_shape=jax.ShapeDtypeStruct((bsz, t, d), F32),
        compiler_params=_cparams(("parallel", "parallel")),
        name="memattn",
    )(x, g, wq_bf, qn, kv4, wo_bf)


FFN_CHUNKS = 11


def _ffn_body(x, g_ref, wg_ref, wu_ref, wd_ref):
    hn = _rms(x, g_ref[...]).astype(BF16)
    cf = wg_ref.shape[1] // FFN_CHUNKS
    acc = x
    for c in range(FFN_CHUNKS):
        sl = slice(c * cf, (c + 1) * cf)
        a = _silu(_dot(hn, wg_ref[:, sl])) * _dot(hn, wu_ref[:, sl])
        acc = acc + _dot(a.astype(BF16), wd_ref[sl, :])
    return acc


def _ffn_kernel(x_ref, g_ref, wg_ref, wu_ref, wd_ref, o_ref):
    o_ref[...] = _ffn_body(x_ref[...], g_ref, wg_ref, wu_ref, wd_ref)


def _memffn_kernel(x_ref, gm_ref, wq_ref, qn_ref, kv_ref, wo_ref, gf_ref, wg_ref, wu_ref, wd_ref,
                   o_ref, *, n_mem):
    x = _memattn_body(x_ref[0], gm_ref, wq_ref, qn_ref, kv_ref, wo_ref, n_mem)
    o_ref[0] = _ffn_body(x, gf_ref, wg_ref, wu_ref, wd_ref)


def _memffn(x, tq, gm, wq_bf, qn, kv4, layer, wo_bf, gf, wg_bf, wu_bf, wd_bf):
    bsz, t, d = x.shape
    rows = kv4.shape[2]
    hw = MEM_HEADS * MEM_HD
    dff = wg_bf.shape[1]
    assert dff % (FFN_CHUNKS * LANES) == 0
    tok = lambda b, i: (b, i, 0)
    return pl.pallas_call(
        functools.partial(_memffn_kernel, n_mem=rows // MEM_SLOTS),
        grid=(bsz, t // tq),
        in_specs=[pl.BlockSpec((1, tq, d), tok),
                  _const_spec((1, d)), _const_spec((d, hw), single_buffer=True), _const_spec((1, MEM_HD)),
                  pl.BlockSpec((1, 1, rows, MEM_HD), lambda b, i: (layer, b, 0, 0)),
                  _const_spec((hw, d), single_buffer=True), _const_spec((1, d)),
                  _const_spec((d, dff), single_buffer=True), _const_spec((d, dff), single_buffer=True),
                  _const_spec((dff, d), single_buffer=True)],
        out_specs=pl.BlockSpec((1, tq, d), tok),
        out_shape=jax.ShapeDtypeStruct((bsz, t, d), F32),
        compiler_params=_cparams(("parallel", "parallel")),
        name="memffn",
    )(x, gm, wq_bf, qn, kv4, wo_bf, gf, wg_bf, wu_bf, wd_bf)


def _ffn(x2d, tm, g, wg_bf, wu_bf, wd_bf):
    m, d = x2d.shape
    dff = wg_bf.shape[1]
    assert dff % (FFN_CHUNKS * LANES) == 0
    return pl.pallas_call(
        _ffn_kernel,
        grid=(m // tm,),
        in_specs=[pl.BlockSpec((tm, d), lambda i: (i, 0)), _const_spec((1, d)),
                  _const_spec((d, dff), single_buffer=True),
                  _const_spec((d, dff), single_buffer=True),
                  _const_spec((dff, d), single_buffer=True)],
        out_specs=pl.BlockSpec((tm, d), lambda i: (i, 0)),
        out_shape=jax.ShapeDtypeStruct((m, d), F32),
        compiler_params=_cparams(("parallel",)),
        name="ffn",
    )(x2d, g, wg_bf, wu_bf, wd_bf)


def _proj_res_kernel(a_ref, w_ref, x_ref, o_ref):
    o_ref[...] = x_ref[...] + _dot(a_ref[...].astype(BF16), w_ref[...])


def _proj_res(a2d, w_bf, x2d):
    m, k = a2d.shape
    n = w_bf.shape[1]
    tm = min(256, m)
    return pl.pallas_call(
        _proj_res_kernel,
        grid=(m // tm,),
        in_specs=[pl.BlockSpec((tm, k), lambda i: (i, 0)), _const_spec((k, n)),
                  pl.BlockSpec((tm, n), lambda i: (i, 0))],
        out_specs=pl.BlockSpec((tm, n), lambda i: (i, 0)),
        out_shape=jax.ShapeDtypeStruct((m, n), F32),
        compiler_params=_cparams(("parallel",)),
        name="proj_res",
    )(a2d, w_bf, x2d)


def _nsa_proj_kernel(x_ref, g_ref, wq_ref, wg_ref, wkv_ref, bd_ref, qg_ref, c_ref, s1_ref, s2_ref,
                     kg_ref, cos_ref, sin_ref,
                     q_ref, gate_ref, rows_ref, win_ref, kvbf_ref):
    hn = _rms(x_ref[0], g_ref[...]).astype(BF16)
    q = _dot(hn, wq_ref[...])
    bd = bd_ref[...]
    scale = NSA_HD ** -0.5
    for c in range(NSA_HEADS * NSA_HD // LANES):
        sl = slice(c * LANES, (c + 1) * LANES)
        q_ref[0, :, sl] = scale * _headnorm_rope_tok(q[:, sl], qg_ref[...], bd, c_ref[...],
                                                     s1_ref[...], s2_ref[...])
    gate_ref[0] = jax.nn.sigmoid(_dot(hn, wg_ref[...]))
    kv_t = _dot_nt(wkv_ref[...], hn)
    cos_t, sin_t = cos_ref[...], sin_ref[...]

    def put_normed(src_row, gain_idx, dst_ref, dst_row, bf_row):
        for g in range(NSA_KV):
            r1, r2, rest = _headnorm_rope_ch(kv_t[src_row + g * NSA_HD:src_row + (g + 1) * NSA_HD],
                                             kg_ref[gain_idx], cos_t, sin_t)
            full = jnp.concatenate([r1, r2, rest], axis=0)
            dst_ref[0, dst_row + g * NSA_HD:dst_row + (g + 1) * NSA_HD, :] = full
            kvbf_ref[0, bf_row + g * NSA_HD:bf_row + (g + 1) * NSA_HD, :] = full.astype(BF16)

    w = KV_W
    rows_ref[0, 0:2 * w, :] = kv_t[0:2 * w]
    put_normed(2 * w, 0, rows_ref, 2 * w, 0)
    rows_ref[0, 3 * w:4 * w, :] = kv_t[3 * w:4 * w]
    kvbf_ref[0, w:2 * w, :] = kv_t[3 * w:4 * w].astype(BF16)
    put_normed(4 * w, 1, win_ref, 0, 2 * w)
    win_ref[0, w:2 * w, :] = kv_t[5 * w:6 * w]
    kvbf_ref[0, 3 * w:4 * w, :] = kv_t[5 * w:6 * w].astype(BF16)


def _nsa_proj(x, tm, g, wq_bf, wg_bf, wkv_t_bf, qgain2, kgain, tabs):
    bsz, t, d = x.shape
    c, s1, s2, cos_t, sin_t = tabs
    hq = NSA_HEADS * NSA_HD
    w = KV_W
    tok = lambda b, i: (b, i, 0)
    chan = lambda b, i: (b, 0, i)
    return pl.pallas_call(
        _nsa_proj_kernel,
        grid=(bsz, t // tm),
        in_specs=[pl.BlockSpec((1, tm, d), tok), _const_spec((1, d)),
                  _const_spec((d, hq)), _const_spec((d, LANES)), _const_spec((6 * w, d)),
                  _const_spec((LANES, LANES)), _const_spec((1, LANES)),
                  pl.BlockSpec((tm, LANES), lambda b, i: (i, 0)),
                  pl.BlockSpec((tm, LANES), lambda b, i: (i, 0)),
                  pl.BlockSpec((tm, LANES), lambda b, i: (i, 0)),
                  _const_spec((2, NSA_HD, 1)),
                  pl.BlockSpec((ROT_HALF, tm), lambda b, i: (0, i)),
                  pl.BlockSpec((ROT_HALF, tm), lambda b, i: (0, i))],
        out_specs=[pl.BlockSpec((1, tm, hq), tok), pl.BlockSpec((1, tm, LANES), tok),
                   pl.BlockSpec((1, 4 * w, tm), chan), pl.BlockSpec((1, 2 * w, tm), chan),
                   pl.BlockSpec((1, 4 * w, tm), chan)],
        out_shape=[jax.ShapeDtypeStruct((bsz, t, hq), F32),
                   jax.ShapeDtypeStruct((bsz, t, LANES), F32),
                   jax.ShapeDtypeStruct((bsz, 4 * w, t), F32),
                   jax.ShapeDtypeStruct((bsz, 2 * w, t), F32),
                   jax.ShapeDtypeStruct((bsz, 4 * w, t), BF16)],
        compiler_params=_cparams(("parallel", "parallel")),
        name="nsa_proj",
    )(x, g, wq_bf, wg_bf, wkv_t_bf, _seg_mean_matrix(), qgain2, c, s1, s2, kgain, cos_t, sin_t)


def _half_sums(xt, pe_ref, w1_ref, nh):
    out = []
    tiles = KV_W // LANES
    for kind in range(2):
        rows = [jnp.concatenate([xt[kind * tiles + c, pl.ds(j, nh, stride=CMP_STRIDE), :]
                                 for c in range(tiles)], axis=1) for j in range(CMP_STRIDE)]
        for half in range(2):
            lhs = jnp.concatenate(
                [(rows[j] + pe_ref[kind, half * CMP_STRIDE + j:half * CMP_STRIDE + j + 1, :]).astype(BF16)
                 for j in range(CMP_STRIDE)], axis=1)
            out.append(_dot(lhs, w1_ref[kind, half]))
    return out


def _finish_compress(a_k, b_k, a_v, b_v, w2_ref, kg_ref, bd, c, s1, s2):
    nh = a_k.shape[0]
    kc = _dot(_silu(a_k + pltpu.roll(b_k, nh - 1, 0)).astype(BF16), w2_ref[0])
    vc = _dot(_silu(a_v + pltpu.roll(b_v, nh - 1, 0)).astype(BF16), w2_ref[1])
    kc = jnp.concatenate([_headnorm_rope_tok(kc[:, h * LANES:(h + 1) * LANES], kg_ref[...], bd, c, s1, s2)
                          for h in range(KV_W // LANES)], axis=1)
    return kc, vc


def _p_compress_kernel(rows_ref, pe_ref, w1_ref, w2_ref, kg_ref, bd_ref, c_ref, s1_ref, s2_ref,
                       kc_ref, vc_ref, xt, *, nh):
    for c in range(2 * KV_W // LANES):
        xt[c] = rows_ref[0, c * LANES:(c + 1) * LANES, :].T
    a_k, b_k, a_v, b_v = _half_sums(xt, pe_ref, w1_ref, nh)
    kc, vc = _finish_compress(a_k, b_k, a_v, b_v, w2_ref, kg_ref, bd_ref[...], c_ref[...],
                              s1_ref[...], s2_ref[...])
    kc_ref[0] = kc.astype(BF16)
    vc_ref[0] = vc.astype(BF16)


def _p_compress(rows_t, pe4, w1bd, w2bd, kgain2, ctabs):
    bsz, _, t = rows_t.shape
    nh = t // CMP_STRIDE
    c, s1, s2 = ctabs
    w = KV_W
    return pl.pallas_call(
        functools.partial(_p_compress_kernel, nh=nh),
        grid=(bsz,),
        in_specs=[pl.BlockSpec((1, 2 * w, t), lambda b: (b, 0, 0)),
                  _const_spec((2, CMP_BLOCK, w)), _const_spec((2, 2, CMP_STRIDE * w, w)),
                  _const_spec((2, w, w)), _const_spec((1, LANES)), _const_spec((LANES, LANES)),
                  _const_spec((nh, LANES)), _const_spec((nh, LANES)), _const_spec((nh, LANES))],
        out_specs=[pl.BlockSpec((1, nh, w), lambda b: (b, 0, 0)),
                   pl.BlockSpec((1, nh, w), lambda b: (b, 0, 0))],
        out_shape=[jax.ShapeDtypeStruct((bsz, nh, w), BF16), jax.ShapeDtypeStruct((bsz, nh, w), BF16)],
        scratch_shapes=[pltpu.VMEM((2 * w // LANES, t, LANES), F32)],
        compiler_params=_cparams(("parallel",)),
        name="p_compress",
    )(rows_t, pe4, w1bd, w2bd, kgain2, _seg_mean_matrix(), c, s1, s2)


def _stack_heads(q, g):
    return jnp.concatenate([q[:, (g * NSA_HPG + hh) * NSA_HD:(g * NSA_HPG + hh + 1) * NSA_HD]
                            for hh in range(NSA_HPG)], axis=0)


def _tile_rows(m, n):
    return jnp.concatenate([m] * n, axis=0)


def _topk_mask(imp, k):
    r, s = imp.shape
    idx = lax.broadcasted_iota(jnp.int32, (r, s), 1)
    rank = jnp.zeros((r, s), F32)
    for s2 in range(s):
        col = imp[:, s2:s2 + 1]
        beats = jnp.where(col > imp, 1.0, jnp.where((col == imp) & (idx > s2), 1.0, 0.0))
        rank = rank + beats
    return rank < k


def _topk_bias_t(imp_t, k):
    s, r = imp_t.shape
    idx = lax.broadcasted_iota(jnp.int32, (s, r), 0)
    rank = jnp.zeros((s, r), F32)
    for s2 in range(s):
        row = imp_t[s2:s2 + 1, :]
        rank = rank + jnp.where(row > imp_t, 1.0, jnp.where((row == imp_t) & (idx > s2), 1.0, 0.0))
    return jnp.where((rank < k) & (imp_t > 0.5 * NEG), 0.0, NEG)


SEL_COLS = 32
DEN_COL = NSA_HD + SEL_COLS


def _p_attn_kernel(q_ref, gate_ref, kc_ref, vc_ref, kvs_ref, *refs, tq, t_all, kch, n_wb):
    win_refs = refs[:n_wb]
    (movt_ref, epad_ref, eye4_ref, wo_ref, x_ref, y_ref,
     o_scr, qa_scr, qbd_scr, oc_scr, os_scr) = refs[n_wb:]
    i = pl.program_id(1)
    q = q_ref[0]
    gates = gate_ref[0]
    nh = kc_ref.shape[1]
    n_slc = movt_ref.shape[0]
    pos_q = i * tq + lax.broadcasted_iota(jnp.int32, (tq, 1), 0)
    pos_q_row = i * tq + lax.broadcasted_iota(jnp.int32, (1, tq), 1)
    pos_c = lax.broadcasted_iota(jnp.int32, (tq, nh), 1) * CMP_STRIDE + (CMP_BLOCK - 1)
    ok_c = _tile_rows(pos_c <= pos_q, NSA_HEADS)
    blk_t = lax.broadcasted_iota(jnp.int32, (n_slc, NSA_KV * tq), 0)
    cur_t = jnp.concatenate([pos_q_row // SLC_BLOCK] * NSA_KV, axis=1)
    w = KV_W
    rows_g = NSA_HPG * tq

    q_heads = [_stack_heads(q, g) for g in range(NSA_KV)]
    def in_group_columns(qg, g):
        left = [jnp.zeros((rows_g, g * NSA_HD), F32)] if g else []
        right = [jnp.zeros((rows_g, (NSA_KV - 1 - g) * NSA_HD), F32)] if g < NSA_KV - 1 else []
        return jnp.concatenate(left + [qg] + right, axis=1)

    qbd_scr[...] = jnp.concatenate([in_group_columns(q_heads[g], g) for g in range(NSA_KV)],
                                   axis=0).astype(BF16)

    s = jnp.where(ok_c, _dot_nt(qbd_scr[...], kc_ref[0]), NEG)
    e = jnp.exp(s - jnp.max(s, axis=-1, keepdims=True))
    p = jnp.where(ok_c, e / jnp.sum(e, axis=-1, keepdims=True), 0.0)
    oc_scr[...] = _dot(p.astype(BF16), vc_ref[0])
    psum = []
    for g in range(NSA_KV):
        acc = p[g * rows_g:g * rows_g + tq]
        for hh in range(1, NSA_HPG):
            acc = acc + p[g * rows_g + hh * tq:g * rows_g + (hh + 1) * tq]
        psum.append(acc)
    imp_t = _dot_nt_exact_lhs(movt_ref[...].astype(BF16), jnp.concatenate(psum, axis=0))
    imp_t = jnp.where((blk_t == cur_t) | (blk_t == 0), BIG, imp_t)
    imp_t = jnp.where(blk_t > cur_t, NEG, imp_t)
    selb_t = _topk_bias_t(imp_t, SLC_TOP).astype(BF16)
    pad = jnp.zeros((rows_g, LANES - NSA_HD - n_slc), F32)
    for g in range(NSA_KV):
        selb = _dot_nt(eye4_ref[...], selb_t[:, g * tq:(g + 1) * tq])
        qa_scr[g] = jnp.concatenate([q_heads[g], selb, pad], axis=1).astype(BF16)

    def k_rows(g):
        return slice(g * NSA_HD, (g + 1) * NSA_HD)

    def v_rows(g):
        return slice(w + g * NSA_HD, w + (g + 1) * NSA_HD)

    n_var = t_all // kch
    need = (i * tq) // kch + 1
    for nv in range(1, n_var + 1):
        @pl.when(need == nv)
        def _(nv=nv):
            klen = nv * kch
            head = klen - kch
            tail_pos = head + lax.broadcasted_iota(jnp.int32, (tq, kch), 1)
            cb4 = _tile_rows(jnp.where(tail_pos <= pos_q, 0.0, NEG), NSA_HPG)

            for g in range(NSA_KV):
                qa = qa_scr[g]
                k_tail = jnp.concatenate([kvs_ref[0, k_rows(g), head:klen], epad_ref[:, head:klen]], axis=0)
                s_t = _dot(qa, k_tail) + cb4
                m = jnp.max(s_t, axis=-1, keepdims=True)
                if head:
                    k_head = jnp.concatenate([kvs_ref[0, k_rows(g), 0:head], epad_ref[:, 0:head]], axis=0)
                    s_h = _dot(qa, k_head)
                    m = jnp.maximum(m, jnp.max(s_h, axis=-1, keepdims=True))
                v_tail = jnp.concatenate([kvs_ref[0, v_rows(g), head:klen], epad_ref[:, head:klen]], axis=0)
                o = _dot_nt(jnp.exp(s_t - m).astype(BF16), v_tail)
                if head:
                    v_head = jnp.concatenate([kvs_ref[0, v_rows(g), 0:head], epad_ref[:, 0:head]], axis=0)
                    o = o + _dot_nt(jnp.exp(s_h - m).astype(BF16), v_head)
                os_scr[g] = o[:, 0:NSA_HD] / o[:, DEN_COL:DEN_COL + 1]

    n_wb = len(win_refs)
    win_blocks = win_refs[::-1]
    wpos = (i - (n_wb - 1)) * tq + lax.broadcasted_iota(jnp.int32, (tq, n_wb * tq), 1)
    dpos = pos_q - wpos
    wb = _tile_rows(jnp.where((dpos >= 0) & (dpos < WINDOW) & (wpos >= 0), 0.0, NEG), NSA_HEADS)
    k_w = jnp.concatenate([r[0, 0:w, :] for r in win_blocks], axis=1)
    v_w = jnp.concatenate([r[0, w:2 * w, :] for r in win_blocks], axis=1)
    s = _dot(qbd_scr[...], k_w) + wb
    e = jnp.exp(s - jnp.max(s, axis=-1, keepdims=True))
    o_w = _dot_nt(e.astype(BF16), v_w) / jnp.sum(e, axis=-1, keepdims=True)
    for h in range(NSA_HEADS):
        g, hh = divmod(h, NSA_HPG)
        rs = slice(h * tq, (h + 1) * tq)
        cs = slice(g * NSA_HD, (g + 1) * NSA_HD)
        o_h = (gates[:, 3 * h:3 * h + 1] * oc_scr[rs, cs]
               + gates[:, 3 * h + 1:3 * h + 2] * os_scr[g, hh * tq:(hh + 1) * tq, :]
               + gates[:, 3 * h + 2:3 * h + 3] * o_w[rs, cs])
        o_scr[:, h * NSA_HD:(h + 1) * NSA_HD] = o_h.astype(BF16)

    y_ref[0] = x_ref[0] + _dot(o_scr[...], wo_ref[...])


def _p_attn(q, gates, kc, vc, kvbf, mov_t, epad, wo_bf, x, tq):
    bsz, t, hq = q.shape
    nh = kc.shape[1]
    w = KV_W
    n_slc = mov_t.shape[0]
    kch = min(512, t)
    rows = NSA_HPG * tq
    n_wb = -(-WINDOW // tq) + 1
    assert kch % tq == 0 and n_wb * tq <= t
    eye4 = jnp.tile(jnp.eye(tq, dtype=BF16), (NSA_HPG, 1))
    tok = lambda b, i: (b, i, 0)
    win_specs = [pl.BlockSpec((1, 2 * w, tq), functools.partial(
        lambda b, i, k: (b, 1, jnp.maximum(i - k, 0)), k=k)) for k in range(n_wb)]
    return pl.pallas_call(
        functools.partial(_p_attn_kernel, tq=tq, t_all=t, kch=kch, n_wb=n_wb),
        grid=(bsz, t // tq),
        in_specs=[pl.BlockSpec((1, tq, hq), tok), pl.BlockSpec((1, tq, LANES), tok),
                  pl.BlockSpec((1, nh, w), lambda b, i: (b, 0, 0)),
                  pl.BlockSpec((1, nh, w), lambda b, i: (b, 0, 0)),
                  pl.BlockSpec((1, 2 * w, t), lambda b, i: (b, 0, 0))] + win_specs + [
                  _const_spec((n_slc, nh)), _const_spec((LANES - NSA_HD, t)),
                  _const_spec((rows, tq)), _const_spec((hq, D_MODEL)),
                  pl.BlockSpec((1, tq, D_MODEL), tok)],
        out_specs=pl.BlockSpec((1, tq, D_MODEL), tok),
        out_shape=jax.ShapeDtypeStruct((bsz, t, D_MODEL), F32),
        scratch_shapes=[pltpu.VMEM((tq, hq), BF16), pltpu.VMEM((NSA_KV, rows, LANES), BF16),
                        pltpu.VMEM((NSA_HEADS * tq, w), BF16), pltpu.VMEM((NSA_HEADS * tq, w), F32),
                        pltpu.VMEM((NSA_KV, rows, NSA_HD), F32)],
        compiler_params=_cparams(("parallel", "parallel")),
        name="p_attn",
    )(q, gates, kc, vc, kvbf, *([kvbf] * n_wb), mov_t, epad, eye4, wo_bf, x)


MAX_PAGES_COMPRESS = 32
MAX_PAGES_SLC = 32


def _s_compress_kernel(pt_ref, *refs, nh, page, pps):
    pages = refs[:pps]
    pe_ref, w1_ref, a_ref, b_ref, xt = refs[pps:]
    for k, pg in enumerate(pages):
        for c in range(2 * KV_W // LANES):
            xt[c, k * page:(k + 1) * page, :] = pg[0, c * LANES:(c + 1) * LANES, :].T
    a_k, b_k, a_v, b_v = _half_sums(xt, pe_ref, w1_ref, nh)
    a_ref[0] = jnp.concatenate([a_k, a_v], axis=1)
    b_ref[0] = jnp.concatenate([b_k, b_v], axis=1)


def _s_compress(cache_t, page_table, pe4, w1bd):
    bsz, n_pages = page_table.shape
    page = cache_t.shape[2]
    w = KV_W
    pps = min(MAX_PAGES_COMPRESS, n_pages)
    steps = n_pages // pps
    nh = pps * page // CMP_STRIDE
    page_specs = [pl.BlockSpec((1, 2 * w, page), functools.partial(
        lambda b, s, pt, k: (pt[b, s * pps + k], 0, 0), k=k)) for k in range(pps)]
    grid_spec = pltpu.PrefetchScalarGridSpec(
        num_scalar_prefetch=1,
        grid=(bsz, steps),
        in_specs=page_specs + [pl.BlockSpec((2, CMP_BLOCK, w), lambda b, s, pt: (0, 0, 0)),
                               pl.BlockSpec((2, 2, CMP_STRIDE * w, w), lambda b, s, pt: (0, 0, 0, 0))],
        out_specs=[pl.BlockSpec((1, nh, 2 * w), lambda b, s, pt: (b, s, 0)),
                   pl.BlockSpec((1, nh, 2 * w), lambda b, s, pt: (b, s, 0))],
        scratch_shapes=[pltpu.VMEM((2 * w // LANES, pps * page, LANES), F32)])
    out = jax.ShapeDtypeStruct((bsz, steps * nh, 2 * w), F32)
    return pl.pallas_call(
        functools.partial(_s_compress_kernel, nh=nh, page=page, pps=pps),
        grid_spec=grid_spec,
        out_shape=[out, out],
        compiler_params=_cparams(("parallel", "arbitrary")),
        name="s_compress",
    )(page_table, *([cache_t] * pps), pe4, w1bd)


def _s_cmp_kernel(a_ref, b_ref, q_ref, w2_ref, kg_ref, bd_ref, c_ref, s1_ref, s2_ref, mov_ref,
                  oc_ref, sel_ref, *, ts, past):
    ab = a_ref[0]
    bb = b_ref[0]
    w = KV_W
    kc, vc = _finish_compress(ab[:, 0:w], bb[:, 0:w], ab[:, w:2 * w], bb[:, w:2 * w], w2_ref, kg_ref,
                              bd_ref[...], c_ref[...], s1_ref[...], s2_ref[...])
    kc = kc.astype(BF16)
    vc = vc.astype(BF16)
    nh = kc.shape[0]
    q = q_ref[0]
    pos_q = past + lax.broadcasted_iota(jnp.int32, (ts, 1), 0)
    pos_c = lax.broadcasted_iota(jnp.int32, (ts, nh), 1) * CMP_STRIDE + (CMP_BLOCK - 1)
    ok_c = _tile_rows(pos_c <= pos_q, NSA_HPG)
    n_blk = mov_ref.shape[1]
    blk = lax.broadcasted_iota(jnp.int32, (ts, n_blk), 1)
    imps = []
    for g in range(NSA_KV):
        qg = _stack_heads(q, g).astype(BF16)
        gs = slice(g * NSA_HD, (g + 1) * NSA_HD)
        s = jnp.where(ok_c, _dot_nt(qg, kc[:, gs]), NEG)
        e = jnp.exp(s - jnp.max(s, axis=-1, keepdims=True))
        p = jnp.where(ok_c, e / jnp.sum(e, axis=-1, keepdims=True), 0.0)
        o_c = _dot(p.astype(BF16), vc[:, gs])
        for hh in range(NSA_HPG):
            h = g * NSA_HPG + hh
            oc_ref[0, :, h * NSA_HD:(h + 1) * NSA_HD] = o_c[hh * ts:(hh + 1) * ts]
        psum = p[0:ts]
        for hh in range(1, NSA_HPG):
            psum = psum + p[hh * ts:(hh + 1) * ts]
        imps.append(psum)
    imp = _dot_exact_rhs(jnp.concatenate(imps, axis=0), mov_ref[...].astype(BF16))
    imp = jnp.where(_tile_rows(blk, NSA_KV) == 0, BIG, imp)
    sel_ref[0] = jnp.where(_topk_mask(imp, SLC_TOP - 1), 1.0, 0.0)


def _s_cmp(a, b, q, w2bd, kgain2, ctabs, mov, past):
    bsz, nh, _ = a.shape
    ts = q.shape[1]
    hq = q.shape[2]
    c, s1, s2 = ctabs
    w = KV_W
    n_blk = mov.shape[1]
    per_b = lambda i: (i, 0, 0)
    return pl.pallas_call(
        functools.partial(_s_cmp_kernel, ts=ts, past=past),
        grid=(bsz,),
        in_specs=[pl.BlockSpec((1, nh, 2 * w), per_b), pl.BlockSpec((1, nh, 2 * w), per_b),
                  pl.BlockSpec((1, ts, hq), per_b),
                  _const_spec((2, w, w)), _const_spec((1, LANES)), _const_spec((LANES, LANES)),
                  _const_spec((nh, LANES)), _const_spec((nh, LANES)), _const_spec((nh, LANES)),
                  _const_spec((nh, n_blk))],
        out_specs=[pl.BlockSpec((1, ts, hq), per_b), pl.BlockSpec((1, NSA_KV * ts, n_blk), per_b)],
        out_shape=[jax.ShapeDtypeStruct((bsz, ts, hq), F32),
                   jax.ShapeDtypeStruct((bsz, NSA_KV * ts, n_blk), F32)],
        compiler_params=_cparams(("parallel",)),
        name="s_cmp",
    )(a, b, q, w2bd, kgain2, _seg_mean_matrix(), c, s1, s2, mov)


def _s_slc_kernel(pt_ref, *refs, ts, pps):
    pages = refs[:pps]
    q_ref, sel_ref, new_ref, exp_ref, o_ref, m_scr, l_scr, acc = refs[pps:]
    s_id = pl.program_id(1)
    w = KV_W
    rows = NSA_HPG * ts

    @pl.when(s_id == 0)
    def _():
        m_scr[...] = jnp.full(m_scr.shape, NEG, F32)
        l_scr[...] = jnp.zeros(l_scr.shape, F32)
        acc[...] = jnp.zeros(acc.shape, F32)

    q = q_ref[0]
    sel = sel_ref[0, 0]
    for g in range(NSA_KV):
        qg = _stack_heads(q, g).astype(BF16)
        k_t = jnp.concatenate([pg[0, g * NSA_HD:(g + 1) * NSA_HD, :] for pg in pages], axis=1).astype(BF16)
        v_t = jnp.concatenate([pg[0, w + g * NSA_HD:w + (g + 1) * NSA_HD, :] for pg in pages],
                              axis=1).astype(BF16)
        selk = _dot(sel[g * ts:(g + 1) * ts].astype(BF16), exp_ref[...])
        ok = _tile_rows(selk > 0.5, NSA_HPG)
        s = jnp.where(ok, _dot(qg, k_t), NEG)
        m_old = m_scr[g]
        m_new = jnp.maximum(m_old, jnp.max(s, axis=-1, keepdims=True))
        alpha = jnp.exp(m_old - m_new)
        e = jnp.where(ok, jnp.exp(s - m_new), 0.0)
        l_scr[g] = alpha * l_scr[g] + jnp.sum(e, axis=-1, keepdims=True)
        acc[g] = alpha * acc[g] + _dot_nt(e.astype(BF16), v_t)
        m_scr[g] = m_new

    @pl.when(s_id == pl.num_programs(1) - 1)
    def _():
        t_row = lax.broadcasted_iota(jnp.int32, (ts, ts), 0)
        t_col = lax.broadcasted_iota(jnp.int32, (ts, ts), 1)
        ok_n = _tile_rows(t_col <= t_row, NSA_HPG)
        for g in range(NSA_KV):
            qg = _stack_heads(q, g).astype(BF16)
            k_n = new_ref[0, g * NSA_HD:(g + 1) * NSA_HD, :].astype(BF16)
            v_n = new_ref[0, w + g * NSA_HD:w + (g + 1) * NSA_HD, :].astype(BF16)
            s = jnp.where(ok_n, _dot(qg, k_n), NEG)
            m_old = m_scr[g]
            m_new = jnp.maximum(m_old, jnp.max(s, axis=-1, keepdims=True))
            alpha = jnp.exp(m_old - m_new)
            e = jnp.where(ok_n, jnp.exp(s - m_new), 0.0)
            l_fin = alpha * l_scr[g] + jnp.sum(e, axis=-1, keepdims=True)
            o = (alpha * acc[g] + _dot_nt(e.astype(BF16), v_n)) / l_fin
            for hh in range(NSA_HPG):
                h = g * NSA_HPG + hh
                o_ref[0, :, h * NSA_HD:(h + 1) * NSA_HD] = o[hh * ts:(hh + 1) * ts]


def _s_slc(cache_t, page_table, q, sel_steps, new_t, expand, pps):
    bsz, n_pages = page_table.shape
    page = cache_t.shape[2]
    ts, hq = q.shape[1], q.shape[2]
    w = KV_W
    steps = n_pages // pps
    bps = sel_steps.shape[3]
    page_specs = [pl.BlockSpec((1, 2 * w, page), functools.partial(
        lambda b, s, pt, k: (pt[b, s * pps + k], 1, 0), k=k)) for k in range(pps)]
    rows = NSA_HPG * ts
    grid_spec = pltpu.PrefetchScalarGridSpec(
        num_scalar_prefetch=1,
        grid=(bsz, steps),
        in_specs=page_specs + [
            pl.BlockSpec((1, ts, hq), lambda b, s, pt: (b, 0, 0)),
            pl.BlockSpec((1, 1, NSA_KV * ts, bps), lambda b, s, pt: (b, s, 0, 0)),
            pl.BlockSpec((1, 2 * w, ts), lambda b, s, pt: (b, 0, 0)),
            pl.BlockSpec((bps, pps * page), lambda b, s, pt: (0, 0))],
        out_specs=pl.BlockSpec((1, ts, hq), lambda b, s, pt: (b, 0, 0)),
        scratch_shapes=[pltpu.VMEM((NSA_KV, rows, 1), F32), pltpu.VMEM((NSA_KV, rows, 1), F32),
                        pltpu.VMEM((NSA_KV, rows, NSA_HD), F32)])
    return pl.pallas_call(
        functools.partial(_s_slc_kernel, ts=ts, pps=pps),
        grid_spec=grid_spec,
        out_shape=jax.ShapeDtypeStruct((bsz, ts, hq), F32),
        compiler_params=_cparams(("parallel", "arbitrary")),
        name="s_slc",
    )(page_table, *([cache_t] * pps), q, sel_steps, new_t, expand)


def _s_win_kernel(q_ref, gate_ref, oc_ref, os_ref, cache_ref, new_ref, o_ref, wout_ref, *, ts):
    q = q_ref[0]
    gates = gate_ref[0]
    cache = cache_ref[0]
    new = new_ref[0]
    wb = cache.shape[1]
    w = KV_W
    t_row = lax.broadcasted_iota(jnp.int32, (ts, wb), 0)
    idx = lax.broadcasted_iota(jnp.int32, (ts, wb), 1)
    dpos = t_row + wb - idx
    ok_old = _tile_rows((dpos >= 0) & (dpos < WINDOW), NSA_HPG)
    n_row = lax.broadcasted_iota(jnp.int32, (ts, ts), 0)
    n_col = lax.broadcasted_iota(jnp.int32, (ts, ts), 1)
    ok_new = _tile_rows((n_col <= n_row) & (n_row - n_col < WINDOW), NSA_HPG)
    for g in range(NSA_KV):
        qg = _stack_heads(q, g).astype(BF16)
        gk = slice(g * NSA_HD, (g + 1) * NSA_HD)
        gv = slice(w + g * NSA_HD, w + (g + 1) * NSA_HD)
        s_old = jnp.where(ok_old, _dot(qg, cache[gk].astype(BF16)), NEG)
        s_new = jnp.where(ok_new, _dot(qg, new[gk].astype(BF16)), NEG)
        m = jnp.maximum(jnp.max(s_old, axis=-1, keepdims=True), jnp.max(s_new, axis=-1, keepdims=True))
        e_old = jnp.exp(s_old - m)
        e_new = jnp.exp(s_new - m)
        den = jnp.sum(e_old, axis=-1, keepdims=True) + jnp.sum(e_new, axis=-1, keepdims=True)
        o_w = (_dot_nt(e_old.astype(BF16), cache[gv].astype(BF16))
               + _dot_nt(e_new.astype(BF16), new[gv].astype(BF16))) / den
        for hh in range(NSA_HPG):
            h = g * NSA_HPG + hh
            hs = slice(h * NSA_HD, (h + 1) * NSA_HD)
            o_ref[0, :, hs] = (gates[:, 3 * h:3 * h + 1] * oc_ref[0, :, hs]
                               + gates[:, 3 * h + 1:3 * h + 2] * os_ref[0, :, hs]
                               + gates[:, 3 * h + 2:3 * h + 3] * o_w[hh * ts:(hh + 1) * ts])
    wout_ref[0] = pltpu.roll(cache, wb - ts, 1)
    wout_ref[0, :, wb - ts:wb] = new


def _s_win(q, gates, o_c, o_s, cache_win_t, new_win_t):
    bsz, ts, hq = q.shape
    wb = cache_win_t.shape[2]
    w = KV_W
    per_b = lambda b: (b, 0, 0)
    return pl.pallas_call(
        functools.partial(_s_win_kernel, ts=ts),
        grid=(bsz,),
        in_specs=[pl.BlockSpec((1, ts, hq), per_b), pl.BlockSpec((1, ts, LANES), per_b),
                  pl.BlockSpec((1, ts, hq), per_b), pl.BlockSpec((1, ts, hq), per_b),
                  pl.BlockSpec((1, 2 * w, wb), per_b), pl.BlockSpec((1, 2 * w, ts), per_b)],
        out_specs=[pl.BlockSpec((1, ts, hq), per_b), pl.BlockSpec((1, 2 * w, wb), per_b)],
        out_shape=[jax.ShapeDtypeStruct((bsz, ts, hq), F32),
                   jax.ShapeDtypeStruct((bsz, 2 * w, wb), F32)],
        compiler_params=_cparams(("parallel",)),
        name="s_win",
    )(q, gates, o_c, o_s, cache_win_t, new_win_t)


def _block_overlap(n_cmp_rows, n_cmp, n_slc):
    cs = jnp.arange(n_cmp_rows)[:, None] * CMP_STRIDE
    ss = jnp.arange(n_slc)[None, :] * SLC_BLOCK
    ov = jnp.minimum(cs + CMP_BLOCK, ss + SLC_BLOCK) - jnp.maximum(cs, ss)
    ov = jnp.clip(ov, 0, None).astype(F32) / CMP_BLOCK
    return jnp.where(jnp.arange(n_cmp_rows)[:, None] < n_cmp, ov, 0.0)


def _expand_matrix(n_blk, n_keys):
    return (jnp.arange(n_keys)[None, :] // SLC_BLOCK == jnp.arange(n_blk)[:, None]).astype(BF16)


def _pad_rows(a, rows):
    return jnp.pad(a, ((0, rows - a.shape[0]), (0, 0)))


def _pad_lanes(a, lanes=LANES):
    return jnp.pad(a, ((0, 0), (0, lanes - a.shape[1])))


def _blockdiag4(m):
    eye = jnp.eye(NSA_KV, dtype=m.dtype)
    out = jnp.einsum('gh,...de->...gdhe', eye, m)
    return out.reshape(m.shape[:-2] + (KV_W, KV_W))


def _layer_tail(x, layer, kv4, tq_mem, tm_ffn, p):
    bsz, t, d = x.shape
    mem_args = (p['norm_mem'][layer][None], p['mem_w_q'][layer].astype(BF16),
                p['mem_q_norm'][layer][None], kv4, layer, p['mem_w_o'][layer].astype(BF16))
    ffn_args = (p['norm_ffn'][layer][None], p['ffn_w_gate'][layer].astype(BF16),
                p['ffn_w_up'][layer].astype(BF16), p['ffn_w_down'][layer].astype(BF16))
    if tq_mem == tm_ffn:
        return _memffn(x, tq_mem, *mem_args, *ffn_args)
    x = _memattn(x, tq_mem, *mem_args)
    return _ffn(x.reshape(bsz * t, d), tm_ffn, *ffn_args).reshape(bsz, t, d)


def _run_mixer0(x, tq, sconv_st, ssdc_st, ssd_st, p):
    w_in = _pad_lanes(p['ab_w_in'][0], IN0_PAD).astype(BF16)
    front = lambda s: jnp.pad(s, ((0, 0), (CARRY - s.shape[1], 0), (0, 0)))
    y, sa, sb, hn = _mixer0(
        x, tq, p['norm_mix'][0][None], w_in,
        _pad_rows(p['ab_sconv_w'][0], SUBLANES), _pad_rows(p['ab_ssd_conv_w'][0], SUBLANES),
        p['ab_ssd_conv_b'], _pad_lanes(p['ab_dt_bias']), _pad_lanes(p['ab_a_log']),
        _pad_lanes(p['ab_d']), p['ab_ssd_norm'], p['ab_w_out'][0].astype(BF16),
        front(sconv_st), front(ssdc_st), ssd_st)
    return y, sa[:, CARRY - (SCONV_W - 1):], sb[:, CARRY - (SSD_CONV_W - 1):], hn


def _nsa_weights(p):
    w_in = p['nsa_w_in'][0]
    hq = NSA_HEADS * NSA_HD
    wq = w_in[:, :hq].astype(BF16)
    wkv_t = w_in[:, hq:hq + 6 * KV_W].T.astype(BF16)
    wg = _pad_lanes(w_in[:, hq + 6 * KV_W:]).astype(BF16)
    qgain2 = jnp.tile(p['nsa_q_norm'][0], 2)[None]
    kn = p['nsa_k_norm'][0]
    kgain = jnp.stack([kn[1], kn[2]])[:, :, None]
    kcgain2 = jnp.tile(kn[0], 2)[None]
    pe4 = jnp.tile(p['nsa_cmp_pe'][0], (1, 1, NSA_KV))
    w1bd = _blockdiag4(p['nsa_cmp_w1'][0]).astype(BF16)
    w1bd = w1bd.reshape(2, 2, CMP_STRIDE * KV_W, KV_W)
    w2bd = _blockdiag4(p['nsa_cmp_w2'][0]).astype(BF16)
    return wq, wg, wkv_t, qgain2, kgain, kcgain2, pe4, w1bd, w2bd


def kernel(x_prompt, x_sample, mem_prompt, state_sconv, state_ssd_conv, state_ssd, cache_nsa_kv,
           cache_nsa_win, cache_mem_kv, page_table, norm_mix, norm_mem, norm_memsrc, norm_ffn,
           ab_w_in, ab_sconv_w, ab_ssd_conv_w, ab_ssd_conv_b, ab_dt_bias, ab_a_log, ab_d, ab_ssd_norm,
           ab_w_out, nsa_w_in, nsa_q_norm, nsa_k_norm, nsa_cmp_pe, nsa_cmp_w1, nsa_cmp_w2, nsa_w_out,
           mem_w_q, mem_q_norm, mem_w_kv, mem_k_norm, mem_w_o, ffn_w_gate, ffn_w_up, ffn_w_down):
    p = dict(norm_mix=norm_mix, norm_mem=norm_mem, norm_ffn=norm_ffn, ab_w_in=ab_w_in,
             ab_sconv_w=ab_sconv_w, ab_ssd_conv_w=ab_ssd_conv_w, ab_ssd_conv_b=ab_ssd_conv_b,
             ab_dt_bias=ab_dt_bias, ab_a_log=ab_a_log, ab_d=ab_d, ab_ssd_norm=ab_ssd_norm,
             ab_w_out=ab_w_out, nsa_w_in=nsa_w_in, nsa_q_norm=nsa_q_norm, nsa_k_norm=nsa_k_norm,
             nsa_cmp_pe=nsa_cmp_pe, nsa_cmp_w1=nsa_cmp_w1, nsa_cmp_w2=nsa_cmp_w2, nsa_w_out=nsa_w_out,
             mem_w_q=mem_w_q, mem_q_norm=mem_q_norm, mem_w_o=mem_w_o,
             ffn_w_gate=ffn_w_gate, ffn_w_up=ffn_w_up, ffn_w_down=ffn_w_down)
    bp, t, d = x_prompt.shape
    bs, ts, _ = x_sample.shape
    n_mem = mem_prompt.shape[1]
    depth = norm_mix.shape[0]
    assert depth == 2 and ab_w_in.shape[0] == 1 and nsa_w_in.shape[0] == 1
    w = KV_W
    hq = NSA_HEADS * NSA_HD
    wq, wg, wkv_t, qgain2, kgain, kcgain2, pe4, w1bd, w2bd = _nsa_weights(p)
    wo_nsa = nsa_w_out[0].astype(BF16)

    mem2d = mem_prompt.reshape(bp * n_mem, d)
    p_mem = _memkv(mem2d, norm_memsrc[:, None], mem_w_kv.astype(BF16), mem_k_norm[:, None])
    p_mem = p_mem.reshape(depth, bp, n_mem * MEM_SLOTS, MEM_HD)
    p_mem_kv = p_mem.reshape(depth, bp, n_mem, 2, MEM_HEADS, MEM_HD)
    s_mem = cache_mem_kv.reshape(depth, bs, n_mem * MEM_SLOTS, MEM_HD)

    tq0 = min(256, t)
    tq_mem = min(512, t)
    tm_ffn = min(512, bp * t)
    zeros = lambda *s: jnp.zeros(s, F32)
    x, p_sconv, p_ssdc, p_ssd = _run_mixer0(
        x_prompt, tq0, zeros(bp, SCONV_W - 1, d), zeros(bp, SSD_CONV_W - 1, SSD_CONV_DIM),
        zeros(bp, SSD_HEADS, SSD_HD, SSD_N), p)
    x = _layer_tail(x, 0, p_mem, tq_mem, tm_ffn, p)

    pos_p = jnp.arange(t)
    tabs_p = _rope_tables(pos_p)
    tm_proj = min(512, t)
    q, gates, rows_t, win_t, kvbf = _nsa_proj(x, tm_proj, norm_mix[1][None], wq, wg, wkv_t, qgain2,
                                              kgain, tabs_p)
    nh = t // CMP_STRIDE
    n_cmp = nh - 1
    pos_c = jnp.arange(nh) * CMP_STRIDE + CMP_BLOCK - 1
    kc, vc = _p_compress(rows_t, pe4, w1bd, w2bd, kcgain2, _rope_tables(pos_c)[:3])
    n_slc = -(-t // SLC_BLOCK)
    tq_a = min(128, t)
    assert n_slc <= SEL_COLS
    epad = jnp.concatenate([_pad_rows(_expand_matrix(n_slc, t), SEL_COLS), jnp.ones((1, t), BF16),
                            jnp.zeros((LANES - DEN_COL - 1, t), BF16)], axis=0)
    x = _p_attn(q, gates, kc, vc, kvbf, _block_overlap(nh, n_cmp, n_slc).T, epad, wo_nsa, x, tq_a)
    y_prompt = _layer_tail(x, 1, p_mem, tq_mem, tm_ffn, p)
    p_rows = rows_t.reshape(bp, 1, 4, NSA_KV, NSA_HD, t).transpose(0, 5, 1, 2, 3, 4)
    wl = min(WINDOW, t)
    p_win = win_t[:, :, t - wl:].reshape(1, bp, 2, NSA_KV, NSA_HD, wl).transpose(0, 1, 5, 2, 3, 4)

    n_pool, page = cache_nsa_kv.shape[0], cache_nsa_kv.shape[1]
    n_pages = page_table.shape[1]
    past = n_pages * page
    pps_slc = min(MAX_PAGES_SLC, n_pages)
    assert ts <= CMP_STRIDE and past % SLC_BLOCK == 0
    assert n_pages % pps_slc == 0 and n_pages % min(MAX_PAGES_COMPRESS, n_pages) == 0
    xs, s_sconv, s_ssdc, s_ssd = _run_mixer0(x_sample, ts, state_sconv[0], state_ssd_conv[0],
                                             state_ssd[0], p)
    xs = _layer_tail(xs, 0, s_mem, ts, bs * ts, p)

    pos_s = jnp.tile(past + jnp.arange(ts), bs)
    qs, gates_s, rows_s, win_s, _ = _nsa_proj(xs.reshape(1, bs * ts, d), bs * ts, norm_mix[1][None],
                                              wq, wg, wkv_t, qgain2, kgain, _rope_tables(pos_s))
    qs = qs.reshape(bs, ts, hq)
    gates_s = gates_s.reshape(bs, ts, LANES)
    rows_s = rows_s[0].reshape(4 * w, bs, ts).transpose(1, 0, 2)
    win_s = win_s[0].reshape(2 * w, bs, ts).transpose(1, 0, 2)
    cache_t = cache_nsa_kv.transpose(0, 2, 3, 4, 5, 1).reshape(n_pool, 4 * w, page)
    a_sum, b_sum = _s_compress(cache_t, page_table, pe4, w1bd)
    nh_s = past // CMP_STRIDE
    n_cmp_s = (past + ts) // CMP_STRIDE - 1
    pos_cs = jnp.arange(nh_s) * CMP_STRIDE + CMP_BLOCK - 1
    n_blk = past // SLC_BLOCK
    o_c, sel = _s_cmp(a_sum, b_sum, qs, w2bd, kcgain2, _rope_tables(pos_cs)[:3],
                      _block_overlap(nh_s, n_cmp_s, n_blk), past)
    steps = n_pages // pps_slc
    bps = n_blk // steps
    sel_steps = sel.reshape(bs, NSA_KV * ts, steps, bps).transpose(0, 2, 1, 3)
    o_s = _s_slc(cache_t, page_table, qs, sel_steps, rows_s[:, 2 * w:],
                 _expand_matrix(bps, pps_slc * page), pps_slc)
    wb = cache_nsa_win.shape[2]
    cache_win_t = cache_nsa_win[0].transpose(0, 2, 3, 4, 1).reshape(bs, 2 * w, wb)
    o_att, new_win_t = _s_win(qs, gates_s, o_c, o_s, cache_win_t, win_s)
    xs = _proj_res(o_att.reshape(bs * ts, hq), wo_nsa, xs.reshape(bs * ts, d)).reshape(bs, ts, d)
    y_sample = _layer_tail(xs, 1, s_mem, ts, bs * ts, p)
    s_rows = rows_s.transpose(0, 2, 1).reshape(bs, ts, 1, 4, NSA_KV, NSA_HD)
    s_win = new_win_t.reshape(1, bs, 2, NSA_KV, NSA_HD, wb).transpose(0, 1, 5, 2, 3, 4)

    return (y_prompt, y_sample, p_sconv[None], p_ssdc[None], p_ssd[None], p_rows, p_win, p_mem_kv,
            s_sconv[None], s_ssdc[None], s_ssd[None], s_rows, s_win)
```
